```python
import math, functools
import jax, jax.numpy as jnp
from jax import lax
import numpy as np

D_MODEL = 1024
BATCH = 4
SEQ = 4096
DEPTH = 1
DEC_BATCH = 32
DEC_SEQ = 4
PAST_LEN = 8192
PAGE_SIZE = 128

D_A = D_MODEL // 2
G_A = 4
C_G = D_A // G_A
CHUNK = 128
HEAD_DIM = 64
H_G = 8
GROUPS = ((128, 1), (512, 4), (2048, 16))
N_GROUPS = 3
N_HEADS_B = H_G * N_GROUPS
D_QKV = N_HEADS_B * HEAD_DIM
D_B_OUT = H_G * HEAD_DIM
N_KEYS = 129
SUB_BLOCK = 128
ROT_DIM = HEAD_DIM // 4
ROPE_THETA = 500000.0
D_FF = 4 * D_MODEL
D_PLE = 256
D_IN = 2 * D_A + 3 * D_QKV + 2 * D_MODEL
EPS = 1e-6
NEG = -1e30

kernel_name = "gated_gmlp_dilated_attn_decoder_step"


def _rmsnorm(x, g):
    xf = x.astype(jnp.float32)
    y = xf * lax.rsqrt(jnp.mean(xf * xf, axis=-1, keepdims=True) + EPS) * g.astype(jnp.float32)
    return y.astype(x.dtype)


def _rotary(x, pos):
    half = ROT_DIM // 2
    inv = ROPE_THETA ** (-jnp.arange(half, dtype=jnp.float32) * 2.0 / ROT_DIM)
    ang = pos.astype(jnp.float32)[:, None] * inv[None, :]
    cos = jnp.cos(ang)[:, None, :]
    sin = jnp.sin(ang)[:, None, :]
    x1 = x[..., :half].astype(jnp.float32)
    x2 = x[..., half:ROT_DIM].astype(jnp.float32)
    rot = jnp.concatenate([x1 * cos - x2 * sin, x2 * cos + x1 * sin], axis=-1).astype(x.dtype)
    return jnp.concatenate([rot, x[..., ROT_DIM:]], axis=-1)


def _chunk_mix_prompt(u, v, w_s, b_s):
    B, S, _ = u.shape
    ws = jnp.tril(w_s.astype(jnp.float32))
    vc = v.astype(jnp.float32).reshape(B, S // CHUNK, CHUNK, G_A, C_G)
    mix = jnp.einsum('gts,bcsgd->bctgd', ws, vc) + b_s.astype(jnp.float32).T[None, None, :, :, None]
    return (u.astype(jnp.float32) * mix.reshape(B, S, D_A)).astype(u.dtype)


def _chunk_mix_sample(u, v, w_s, b_s):
    B, T, _ = u.shape
    ws = jnp.tril(w_s.astype(jnp.float32))[:, :T, :T]
    vc = v.astype(jnp.float32).reshape(B, T, G_A, C_G)
    mix = jnp.einsum('gts,bsgd->btgd', ws, vc) + b_s.astype(jnp.float32)[:, :T].T[None, :, :, None]
    return (u.astype(jnp.float32) * mix.reshape(B, T, D_A)).astype(u.dtype)


def _dilated_group_prompt(q, k, v, d):
    B, S, H, Dh = q.shape
    span = SUB_BLOCK * d
    L = -(-S // span) * span
    nb = L // span

    def to_blocks(t):
        t = jnp.pad(t.astype(jnp.float32), ((0, 0), (0, L - S), (0, 0), (0, 0)))
        t = t.reshape(B, L // d, d, H, Dh).transpose(0, 2, 1, 3, 4)
        return t.reshape(B, d, nb, SUB_BLOCK, H, Dh)

    def with_prev(t):
        prev = jnp.pad(t[:, :, :-1], ((0, 0), (0, 0), (1, 0), (0, 0), (0, 0), (0, 0)))
        return jnp.concatenate([prev, t], axis=3)

    qb = to_blocks(q)
    kk = with_prev(to_blocks(k))
    vv = with_prev(to_blocks(v))
    s = jnp.einsum('brnqhd,brnkhd->brnhqk', qb, kk) / math.sqrt(Dh)
    i = jnp.arange(SUB_BLOCK)[:, None]
    j = jnp.arange(2 * SUB_BLOCK)[None, :]
    rel = SUB_BLOCK + i - j
    band = (rel >= 0) & (rel <= N_KEYS - 1)
    exists = (jnp.arange(nb)[:, None, None] > 0) | (j >= SUB_BLOCK)[None]
    mask = (band[None] & exists)[None, None, :, None]
    s = jnp.where(mask, s, NEG)
    m = jnp.max(s, axis=-1, keepdims=True)
    e = jnp.exp(s - m)
    den = jnp.sum(e, axis=-1, keepdims=True)
    o = jnp.einsum('brnhqk,brnkhd->brnhqd', e, vv) / den
    lse = (m + jnp.log(den))[..., 0]
    o = o.transpose(0, 1, 2, 4, 3, 5).reshape(B, d, L // d, H, Dh)
    o = o.transpose(0, 2, 1, 3, 4).reshape(B, L, H, Dh)[:, :S]
    lse = lse.transpose(0, 1, 2, 4, 3).reshape(B, d, L // d, H)
    lse = lse.transpose(0, 2, 1, 3).reshape(B, L, H)[:, :S]
    return o, lse


def _dilated_group_sample(q, k, v, kv_buf, d):
    T = q.shape[1]
    Wb = kv_buf.shape[2]
    kk = jnp.concatenate([kv_buf[:, 0], k.astype(kv_buf.dtype)], axis=1)
    vv = jnp.concatenate([kv_buf[:, 1], v.astype(kv_buf.dtype)], axis=1)
    idx = Wb + jnp.arange(T)[:, None] - jnp.arange(N_KEYS)[None, :] * d
    valid = idx >= 0
    idx = jnp.maximum(idx, 0)
    kg = kk[:, idx].astype(jnp.float32)
    vg = vv[:, idx].astype(jnp.float32)
    s = jnp.einsum('bthd,btjhd->bhtj', q.astype(jnp.float32), kg) / math.sqrt(q.shape[-1])
    s = jnp.where(valid[None, None], s, NEG)
    m = jnp.max(s, axis=-1, keepdims=True)
    e = jnp.exp(s - m)
    den = jnp.sum(e, axis=-1, keepdims=True)
    o = jnp.einsum('bhtj,btjhd->bhtd', e, vg) / den
    lse = (m + jnp.log(den))[..., 0]
    new_buf = jnp.stack([kk[:, -Wb:], vv[:, -Wb:]], axis=1)
    return o.transpose(0, 2, 1, 3), lse.transpose(0, 2, 1), new_buf


def _combine(outs, lses, dtype):
    w = jax.nn.softmax(jnp.stack(lses, axis=0), axis=0)[..., None]
    o = jnp.sum(w * jnp.stack(outs, axis=0), axis=0)
    return o.reshape(*o.shape[:-2], D_B_OUT).astype(dtype)


def _dilated_prompt(q, k, v):
    S = q.shape[1]
    outs, lses, states = [], [], []
    for gi, (win, dil) in enumerate(GROUPS):
        sl = slice(gi * H_G, (gi + 1) * H_G)
        o, lse = _dilated_group_prompt(q[:, :, sl], k[:, :, sl], v[:, :, sl], dil)
        outs.append(o)
        lses.append(lse)
        wb = min(win, S)
        states.append(jnp.stack([k[:, S - wb:, sl], v[:, S - wb:, sl]], axis=1))
    return _combine(outs, lses, q.dtype), states


def _dilated_sample(q, k, v, bufs):
    outs, lses, states = [], [], []
    for gi, (win, dil) in enumerate(GROUPS):
        sl = slice(gi * H_G, (gi + 1) * H_G)
        o, lse, nb = _dilated_group_sample(q[:, :, sl], k[:, :, sl], v[:, :, sl], bufs[gi], dil)
        outs.append(o)
        lses.append(lse)
        states.append(nb)
    return _combine(outs, lses, q.dtype), states


def _layer(x, pe, pos, g_mix, w_in, g_v, w_s, b_s, g_q, g_k, w_oa, w_ob, w_o,
           g_ffn, w_ff1, w_ff2, g_ple, w_ple_gate, w_ple_proj, mix_a, mix_b):
    lead = x.shape[:-1]
    h = _rmsnorm(x, g_mix)
    z = h @ w_in
    cuts = [D_A, 2 * D_A, 2 * D_A + D_QKV, 2 * D_A + 2 * D_QKV, 2 * D_A + 3 * D_QKV,
            2 * D_A + 3 * D_QKV + D_MODEL]
    u, va, q, k, v, gate_a, gate_b = jnp.split(z, cuts, axis=-1)
    u = jax.nn.gelu(u)
    va = _rmsnorm(jax.nn.gelu(va), g_v)
    a = mix_a(u, va, w_s, b_s)
    q = _rotary(_rmsnorm(q.reshape(*lead, N_HEADS_B, HEAD_DIM), g_q), pos)
    k = _rotary(_rmsnorm(k.reshape(*lead, N_HEADS_B, HEAD_DIM), g_k), pos)
    v = v.reshape(*lead, N_HEADS_B, HEAD_DIM)
    b, kv_states = mix_b(q, k, v)
    merged = jax.nn.sigmoid(gate_a) * (a @ w_oa) + jax.nn.sigmoid(gate_b) * (b @ w_ob)
    x = x + merged @ w_o
    h2 = _rmsnorm(x, g_ffn)
    x = x + jnp.square(jax.nn.relu(h2 @ w_ff1)) @ w_ff2
    gate = jax.nn.sigmoid(_rmsnorm(x, g_ple) @ w_ple_gate)
    x = x + gate * (pe @ w_ple_proj)
    return x, kv_states, va


def setup_inputs(seed: int = 0) -> dict:
    key = jax.random.key(seed)
    ks = jax.random.split(key, 32)
    f32 = jnp.float32

    def nrm(k, shape, scale=1.0):
        return jax.random.normal(k, shape, f32) * scale

    def gain(k, shape):
        return 1.0 + 0.05 * jax.random.normal(k, shape, f32)

    out = {
        "x_prompt": nrm(ks[0], (BATCH, SEQ, D_MODEL)),
        "x_sample": nrm(ks[1], (DEC_BATCH, DEC_SEQ, D_MODEL)),
        "p_prompt": nrm(ks[2], (DEPTH, BATCH, SEQ, D_PLE)),
        "p_sample": nrm(ks[3], (DEPTH, DEC_BATCH, DEC_SEQ, D_PLE)),
        "cache_kv_w128": nrm(ks[4], (DEPTH, DEC_BATCH, 2, min(GROUPS[0][0], PAST_LEN), H_G, HEAD_DIM)),
        "cache_kv_w512": nrm(ks[5], (DEPTH, DEC_BATCH, 2, min(GROUPS[1][0], PAST_LEN), H_G, HEAD_DIM)),
        "cache_kv_w2048": nrm(ks[6], (DEPTH, DEC_BATCH, 2, min(GROUPS[2][0], PAST_LEN), H_G, HEAD_DIM)),
        "g_mix": gain(ks[7], (DEPTH, D_MODEL)),
        "w_in": nrm(ks[8], (DEPTH, D_MODEL, D_IN), D_MODEL ** -0.5),
        "g_v": gain(ks[9], (DEPTH, D_A)),
        "w_s": nrm(ks[10], (DEPTH, G_A, CHUNK, CHUNK), CHUNK ** -0.5),
        "b_s": 1.0 + 0.1 * nrm(ks[11], (DEPTH, G_A, CHUNK)),
        "g_q": gain(ks[12], (DEPTH, HEAD_DIM)),
        "g_k": gain(ks[13], (DEPTH, HEAD_DIM)),
        "w_oa": nrm(ks[14], (DEPTH, D_A, D_MODEL), D_A ** -0.5),
        "w_ob": nrm(ks[15], (DEPTH, D_B_OUT, D_MODEL), D_B_OUT ** -0.5),
        "w_o": nrm(ks[16], (DEPTH, D_MODEL, D_MODEL), D_MODEL ** -0.5),
        "g_ffn": gain(ks[17], (DEPTH, D_MODEL)),
        "w_ff1": nrm(ks[18], (DEPTH, D_MODEL, D_FF), D_MODEL ** -0.5),
        "w_ff2": nrm(ks[19], (DEPTH, D_FF, D_MODEL), D_FF ** -0.5),
        "g_ple": gain(ks[20], (DEPTH, D_MODEL)),
        "w_ple_gate": nrm(ks[21], (DEPTH, D_MODEL, D_MODEL), D_MODEL ** -0.5),
        "w_ple_proj": nrm(ks[22], (DEPTH, D_PLE, D_MODEL), D_PLE ** -0.5),
    }
    return out


def reference(x_prompt, x_sample, p_prompt, p_sample, cache_kv_w128, cache_kv_w512, cache_kv_w2048,
              g_mix, w_in, g_v, w_s, b_s, g_q, g_k, w_oa, w_ob, w_o,
              g_ffn, w_ff1, w_ff2, g_ple, w_ple_gate, w_ple_proj):
    S = x_prompt.shape[1]
    T = x_sample.shape[1]
    pos_p = jnp.arange(S, dtype=jnp.int32)
    pos_s = PAST_LEN + jnp.arange(T, dtype=jnp.int32)
    xp, xs = x_prompt, x_sample
    kvp = ([], [], [])
    kvs = ([], [], [])
    cvp, cvs = [], []
    for l in range(DEPTH):
        lw = (g_mix[l], w_in[l], g_v[l], w_s[l], b_s[l], g_q[l], g_k[l], w_oa[l], w_ob[l], w_o[l],
              g_ffn[l], w_ff1[l], w_ff2[l], g_ple[l], w_ple_gate[l], w_ple_proj[l])
        xp, st_p, va_p = _layer(xp, p_prompt[l], pos_p, *lw,
                                mix_a=_chunk_mix_prompt, mix_b=_dilated_prompt)
        bufs = (cache_kv_w128[l], cache_kv_w512[l], cache_kv_w2048[l])
        xs, st_s, va_s = _layer(xs, p_sample[l], pos_s, *lw,
                                mix_a=_chunk_mix_sample,
                                mix_b=functools.partial(_dilated_sample, bufs=bufs))
        for gi in range(N_GROUPS):
            kvp[gi].append(st_p[gi])
            kvs[gi].append(st_s[gi])
        cvp.append(va_p[:, S - CHUNK:])
        cvs.append(va_s)
    kv128_prompt = jnp.stack(kvp[0], axis=0)
    kv512_prompt = jnp.stack(kvp[1], axis=0)
    kv2048_prompt = jnp.stack(kvp[2], axis=0)
    kv128_sample = jnp.stack(kvs[0], axis=0)
    kv512_sample = jnp.stack(kvs[1], axis=0)
    kv2048_sample = jnp.stack(kvs[2], axis=0)
    chunk_v_prompt = jnp.stack(cvp, axis=0)
    chunk_v_sample = jnp.stack(cvs, axis=0)
    return (xp, xs, kv128_prompt, kv512_prompt, kv2048_prompt,
            kv128_sample, kv512_sample, kv2048_sample, chunk_v_prompt, chunk_v_sample)
```

```python
import functools

import jax
import jax.numpy as jnp
from jax import lax
from jax.experimental import pallas as pl
from jax.experimental.pallas import tpu as pltpu

F32 = jnp.float32
BF16 = jnp.bfloat16

LANES = 128
BF16_ROWS = 16
HEAD_DIM = 64
H_G = 8
HEADS_PER_VREG = LANES // HEAD_DIM
GROUPS = ((128, 1), (512, 4), (2048, 16))
N_GROUPS = len(GROUPS)
D_GROUP = H_G * HEAD_DIM
GROUP_SLABS = D_GROUP // LANES
D_QKV = N_GROUPS * D_GROUP
N_KEYS = 129
SUB_BLOCK = 128
ROT_DIM = HEAD_DIM // 4
ROT_HALF = ROT_DIM // 2
ROPE_THETA = 500000.0
CHUNK = 128
G_A = 4
PAST_LEN = 8192
EPS = 1e-6
NEG = -1e30
VMEM_LIMIT = 56 * 1024 * 1024


def _rms(x, g):
    return x * lax.rsqrt(jnp.mean(x * x, axis=-1, keepdims=True) + EPS) * g


def _resident(shape):
    zeros = (0,) * len(shape)
    return pl.BlockSpec(shape, lambda *_: zeros, pipeline_mode=pl.Buffered(1))


def _dot(a, b):
    return jnp.dot(a, b, preferred_element_type=F32)


def _dot_nt(a, b):
    return lax.dot_general(a, b, (((1,), (1,)), ((), ())), preferred_element_type=F32)


def _span_tiles(dil, tm):
    if dil == 1:
        return 1
    span = SUB_BLOCK * dil
    assert span % tm == 0 and (tm // dil) % BF16_ROWS == 0
    return span // tm


def _project_qkv(x_ref, gmix_ref, w_ref, gq_ref, gk_ref, rc_ref, ru_ref, rd_ref, bd_ref, store):
    h = _rms(x_ref[...], gmix_ref[...]).astype(BF16)
    rc, ru, rd = rc_ref[...], ru_ref[...], rd_ref[...]
    bd = bd_ref[...]

    def norm_rot(z, g):
        ss = _dot((z * z).astype(BF16), bd)
        zn = z * lax.rsqrt(ss * (1.0 / HEAD_DIM) + EPS) * g
        return zn * rc + pltpu.roll(zn, LANES - ROT_HALF, 1) * ru + pltpu.roll(zn, ROT_HALF, 1) * rd

    for ti, g_ref in enumerate((gq_ref, gk_ref, None)):
        z = _dot(h, w_ref[:, ti * D_QKV:(ti + 1) * D_QKV])
        for c in range(D_QKV // LANES):
            zc = z[:, c * LANES:(c + 1) * LANES]
            store(ti, c, zc if g_ref is None else norm_rot(zc, g_ref[...]))


def _qkv_prompt_kernel(*refs, tm, n_tiles):
    ins, outs, kv_refs, slabs = refs[:9], refs[9:18], refs[18:21], refs[21:24]

    def store(ti, c, val):
        slabs[ti][c] = val

    _project_qkv(*ins, store)

    i = pl.program_id(1)
    for ti, slab in enumerate(slabs):
        for gi, (_, dil) in enumerate(GROUPS):
            out = outs[ti * N_GROUPS + gi]
            piece = tm // dil
            isub = i % _span_tiles(dil, tm)
            for cc in range(GROUP_SLABS):
                c = gi * GROUP_SLABS + cc
                cols = slice(cc * LANES, (cc + 1) * LANES)
                if dil == 1:
                    out[:, cols] = slab[c].astype(BF16)
                    continue
                for r in range(dil):
                    start = pl.multiple_of(r * SUB_BLOCK + isub * piece, BF16_ROWS)
                    out[pl.ds(start, piece), cols] = slab[c, pl.ds(r, piece, stride=dil), :].astype(BF16)

    seq = n_tiles * tm
    for gi, (out_ref, (win, _)) in enumerate(zip(kv_refs, GROUPS)):
        rows = min(win, tm)
        first = (seq - win) // tm if win >= tm else n_tiles - 1

        @pl.when(i >= first)
        def _(out_ref=out_ref, gi=gi, rows=rows):
            for cc in range(GROUP_SLABS):
                c = gi * GROUP_SLABS + cc
                for kv in range(2):
                    t = slabs[1 + kv][c, tm - rows:tm, :].T
                    for hh in range(HEADS_PER_VREG):
                        out_ref[0, kv, cc * HEADS_PER_VREG + hh] = t[hh * HEAD_DIM:(hh + 1) * HEAD_DIM, :]


def _qkv_sample_kernel(*refs):
    ins, outs = refs[:9], refs[9:12]

    def store(ti, c, val):
        if ti == 0:
            outs[ti][:, c * LANES:(c + 1) * LANES] = val
        else:
            outs[ti][c * LANES:(c + 1) * LANES, :] = val.T

    _project_qkv(*ins, store)


def _rotary_tables(pos):
    inv = ROPE_THETA ** (-jnp.arange(ROT_HALF, dtype=F32) * 2.0 / ROT_DIM)
    ang = pos.astype(F32)[:, None] * inv[None, :]
    cos, sin = jnp.cos(ang), jnp.sin(ang)
    n = pos.shape[0]
    pad = jnp.zeros((n, HEAD_DIM - ROT_DIM), F32)
    zero = jnp.zeros((n, ROT_HALF), F32)
    rc = jnp.concatenate([cos, cos, pad + 1.0], axis=1)
    ru = jnp.concatenate([-sin, zero, pad], axis=1)
    rd = jnp.concatenate([zero, sin, pad], axis=1)
    return tuple(jnp.tile(t, (1, HEADS_PER_VREG)) for t in (rc, ru, rd))


def _qkv_call(x2, pos, g_mix, w_qkv, g_q, g_k, *, prompt, batch, tm):
    n_tok, d_model = x2.shape
    seq = n_tok // batch
    n_tiles = seq // tm
    rc, ru, rd = _rotary_tables(pos)
    gq = jnp.tile(g_q.astype(F32), HEADS_PER_VREG)[None, :] * (HEAD_DIM ** -0.5)
    gk = jnp.tile(g_k.astype(F32), HEADS_PER_VREG)[None, :]
    lane = jnp.arange(LANES)
    bd = (lane[:, None] // HEAD_DIM == lane[None, :] // HEAD_DIM).astype(BF16)

    tile_index = lambda b, i: b * n_tiles + i
    tok = lambda b, i: (tile_index(b, i), 0)
    tab = lambda b, i: (i, 0)
    in_specs = [
        pl.BlockSpec((tm, d_model), tok),
        _resident((1, d_model)),
        _resident((d_model, 3 * D_QKV)),
        _resident((1, LANES)),
        _resident((1, LANES)),
        pl.BlockSpec((tm, LANES), tab),
        pl.BlockSpec((tm, LANES), tab),
        pl.BlockSpec((tm, LANES), tab),
        _resident((LANES, LANES)),
    ]
    if prompt:
        out_shape, out_specs = [], []
        for _ in range(3):
            for _, dil in GROUPS:
                st = _span_tiles(dil, tm)
                out_shape.append(jax.ShapeDtypeStruct((n_tok, D_GROUP), BF16))
                out_specs.append(pl.BlockSpec((st * tm, D_GROUP),
                                              lambda b, i, st=st: (tile_index(b, i) // st, 0)))
        for win, _ in GROUPS:
            rows = min(win, tm)
            first = (seq - win) // tm if win >= tm else n_tiles - 1
            out_shape.append(jax.ShapeDtypeStruct((batch, 2, H_G, HEAD_DIM, win), F32))
            out_specs.append(pl.BlockSpec(
                (1, 2, H_G, HEAD_DIM, rows),
                lambda b, i, first=first: (b, 0, 0, 0, jnp.maximum(i - first, 0))))
        scratch = [pltpu.VMEM((D_QKV // LANES, tm, LANES), F32)] * 3
        body = functools.partial(_qkv_prompt_kernel, tm=tm, n_tiles=n_tiles)
    else:
        assert n_tiles == 1 and batch == 1 and tm == LANES
        out_shape = [jax.ShapeDtypeStruct((n_tok, D_QKV), F32)] + [jax.ShapeDtypeStruct((D_QKV, n_tok), F32)] * 2
        out_specs = [pl.BlockSpec((tm, D_QKV), tok)] + [pl.BlockSpec((D_QKV, tm), lambda b, i: (0, 0))] * 2
        scratch = []
        body = _qkv_sample_kernel

    return pl.pallas_call(
        body,
        grid=(batch, n_tiles),
        in_specs=in_specs,
        out_specs=out_specs,
        out_shape=out_shape,
        scratch_shapes=scratch,
        compiler_params=pltpu.CompilerParams(
            dimension_semantics=("arbitrary", "arbitrary"), vmem_limit_bytes=VMEM_LIMIT),
        name="qkv_prompt" if prompt else "qkv_sample",
    )(x2, g_mix[None, :], w_qkv, gq, gk, rc, ru, rd, bd)


def _attn_prompt_kernel(q_ref, kc_ref, kp_ref, vc_ref, vp_ref, o_ref, lse_ref):
    n = pl.program_id(1)
    row = lax.broadcasted_iota(jnp.int32, (SUB_BLOCK, SUB_BLOCK), 0)
    col = lax.broadcasted_iota(jnp.int32, (SUB_BLOCK, SUB_BLOCK), 1)
    mask_cur = col <= row
    mask_prev = jnp.logical_and(col >= row, n > 0)
    lane = lax.broadcasted_iota(jnp.int32, (SUB_BLOCK, LANES), 1)
    lse_tile = jnp.zeros((SUB_BLOCK, LANES), F32)
    for p in range(GROUP_SLABS):
        sl = slice(p * LANES, (p + 1) * LANES)
        q = q_ref[:, sl]
        kc, kp, vc, vp = kc_ref[:, sl], kp_ref[:, sl], vc_ref[:, sl], vp_ref[:, sl]
        o_pair = jnp.zeros((SUB_BLOCK, LANES), F32)
        for half in range(HEADS_PER_VREG):
            in_head = (lane // HEAD_DIM) == half
            qh = jnp.where(in_head, q, jnp.zeros_like(q))
            s_c = jnp.where(mask_cur, _dot_nt(qh, kc), NEG)
            s_p = jnp.where(mask_prev, _dot_nt(qh, kp), NEG)
            m = jnp.maximum(jnp.max(s_c, axis=1, keepdims=True), jnp.max(s_p, axis=1, keepdims=True))
            e_c = jnp.exp(s_c - m)
            e_p = jnp.exp(s_p - m)
            den = jnp.sum(e_c, axis=1, keepdims=True) + jnp.sum(e_p, axis=1, keepdims=True)
            o = (_dot(e_c.astype(BF16), vc) + _dot(e_p.astype(BF16), vp)) / den
            o_pair = jnp.where(in_head, o, o_pair)
            lse = m + jnp.log(den)
            lse_tile = jnp.where(lane == p * HEADS_PER_VREG + half, lse, lse_tile)
        o_ref[:, sl] = o_pair.astype(BF16)
    lse_ref[...] = lse_tile


def _attn_prompt_call(q, k, v, gi, batch, seq):
    dil = GROUPS[gi][1]
    span = SUB_BLOCK * dil
    assert seq % span == 0
    nb = seq // span
    cur_map = lambda b, n, r: ((b * nb + n) * dil + r, 0)
    prev_map = lambda b, n, r: ((b * nb + jnp.maximum(n - 1, 0)) * dil + r, 0)
    cur = pl.BlockSpec((SUB_BLOCK, D_GROUP), cur_map)
    prev = pl.BlockSpec((SUB_BLOCK, D_GROUP), prev_map)
    return pl.pallas_call(
        _attn_prompt_kernel,
        grid=(batch, nb, dil),
        in_specs=[cur, cur, prev, cur, prev],
        out_specs=[cur, pl.BlockSpec((SUB_BLOCK, LANES), cur_map)],
        out_shape=[jax.ShapeDtypeStruct((batch * seq, D_GROUP), BF16),
                   jax.ShapeDtypeStruct((batch * seq, LANES), F32)],
        compiler_params=pltpu.CompilerParams(
            dimension_semantics=("arbitrary", "arbitrary", "arbitrary")),
        name=f"attn_prompt_g{gi}",
    )(q, k, k, v, v)


def _attn_sample_kernel(q_ref, knt_ref, vnt_ref, buf_ref, o_ref, lse_ref, nbuf_ref, *, dil, wb, t_new):
    assert dil & (dil - 1) == 0
    q_rows = q_ref.shape[1]
    width = wb + LANES
    shift = lax.rem(LANES - t_new * pl.program_id(0), LANES)
    knt = pltpu.roll(knt_ref[...], shift, 1)
    vnt = pltpu.roll(vnt_ref[...], shift, 1)
    q = q_ref[0]

    t_row = lax.broadcasted_iota(jnp.int32, (q_rows, width), 0)
    p_col = lax.broadcasted_iota(jnp.int32, (q_rows, width), 1)
    back = wb + t_row - p_col
    valid = jnp.logical_and(jnp.logical_and(back >= 0, (back & (dil - 1)) == 0),
                            jnp.logical_and(back <= (N_KEYS - 1) * dil, p_col < wb + t_new))
    lane = lax.broadcasted_iota(jnp.int32, (q_rows, LANES), 1)
    lse_tile = jnp.zeros((q_rows, LANES), F32)
    o_heads = []
    for h in range(H_G):
        rows = slice(h * HEAD_DIM, (h + 1) * HEAD_DIM)
        kx = jnp.concatenate([buf_ref[0, 0, h], knt[rows, :]], axis=1)
        vx = jnp.concatenate([buf_ref[0, 1, h], vnt[rows, :]], axis=1)
        nbuf_ref[0, 0, h] = pltpu.roll(kx, width - t_new, 1)[:, :wb]
        nbuf_ref[0, 1, h] = pltpu.roll(vx, width - t_new, 1)[:, :wb]
        s = jnp.where(valid, _dot(q[:, rows].astype(BF16), kx.astype(BF16)), NEG)
        m = jnp.max(s, axis=1, keepdims=True)
        e = jnp.exp(s - m)
        den = jnp.sum(e, axis=1, keepdims=True)
        o_heads.append(_dot_nt(e.astype(BF16), vx.astype(BF16)) / den)
        lse_tile = jnp.where(lane == h, m + jnp.log(den), lse_tile)
    o_ref[0] = jnp.concatenate(o_heads, axis=1)
    lse_ref[0] = lse_tile


def _attn_sample_call(q, knt, vnt, buf, gi, batch, t_new):
    dil = GROUPS[gi][1]
    wb = buf.shape[2]
    assert wb % LANES == 0 and LANES % t_new == 0 and batch * t_new <= LANES
    q_rows = 8
    q_pad = jnp.pad(q.reshape(batch, t_new, D_QKV), ((0, 0), (0, q_rows - t_new), (0, 0)))
    buf_t = jnp.transpose(buf, (0, 1, 3, 4, 2))
    new_spec = pl.BlockSpec((D_GROUP, LANES), lambda b: (gi, 0))
    buf_spec = pl.BlockSpec((1, 2, H_G, HEAD_DIM, wb), lambda b: (b, 0, 0, 0, 0))
    o, lse, nbuf = pl.pallas_call(
        functools.partial(_attn_sample_kernel, dil=dil, wb=wb, t_new=t_new),
        grid=(batch,),
        in_specs=[pl.BlockSpec((1, q_rows, D_GROUP), lambda b: (b, 0, gi)), new_spec, new_spec, buf_spec],
        out_specs=[pl.BlockSpec((1, q_rows, D_GROUP), lambda b: (b, 0, 0)),
                   pl.BlockSpec((1, q_rows, LANES), lambda b: (b, 0, 0)),
                   buf_spec],
        out_shape=[jax.ShapeDtypeStruct((batch, q_rows, D_GROUP), F32),
                   jax.ShapeDtypeStruct((batch, q_rows, LANES), F32),
                   jax.ShapeDtypeStruct((batch, 2, H_G, HEAD_DIM, wb), F32)],
        compiler_params=pltpu.CompilerParams(
            dimension_semantics=("arbitrary",), vmem_limit_bytes=VMEM_LIMIT),
        name=f"attn_sample_g{gi}",
    )(q_pad, knt, vnt, buf_t)
    return (o[:, :t_new].reshape(batch * t_new, D_GROUP), lse[:, :t_new].reshape(batch * t_new, LANES),
            jnp.transpose(nbuf, (0, 1, 4, 2, 3)))


def _merge_kernel(x_ref, o0_ref, o1_ref, o2_ref, l0_ref, l1_ref, l2_ref, pe_ref,
                  gmix_ref, wuv_ref, wg_ref, gv_ref, ws_ref, bias_ref, woa_ref, wob_ref, wo_ref,
                  gffn_ref, wff1_ref, wff2_ref, gple_ref, wpg_ref, wpp_ref,
                  y_ref, va_ref, *scratch, prompt, tm, n_tiles):
    d_model = x_ref.shape[1]
    d_a = gv_ref.shape[1]
    c_g = d_a // G_A
    x = x_ref[...]
    h = _rms(x, gmix_ref[...]).astype(BF16)

    u = jax.nn.gelu(_dot(h, wuv_ref[:, 0:d_a]))
    va = _rms(jax.nn.gelu(_dot(h, wuv_ref[:, d_a:2 * d_a])), gv_ref[...])
    if prompt:
        @pl.when(pl.program_id(1) == n_tiles - 1)
        def _():
            va_ref[0] = va[tm - CHUNK:tm, :]
    else:
        va_ref[...] = va
    va16 = va.astype(BF16)
    r_i = lax.broadcasted_iota(jnp.int32, (CHUNK, CHUNK), 0)
    c_i = lax.broadcasted_iota(jnp.int32, (CHUNK, CHUNK), 1)
    w_tril = [jnp.where(r_i >= c_i, ws_ref[g], jnp.zeros((CHUNK, CHUNK), BF16)) for g in range(G_A)]
    mix_rows = []
    for ci in range(tm // CHUNK):
        rs = slice(ci * CHUNK, (ci + 1) * CHUNK)
        cols = [_dot(w_tril[g], va16[rs, g * c_g:(g + 1) * c_g]) for g in range(G_A)]
        mix_rows.append(jnp.concatenate(cols, axis=1) + bias_ref[...])
    a = u * jnp.concatenate(mix_rows, axis=0)

    o_refs, l_refs = (o0_ref, o1_ref, o2_ref), (l0_ref, l1_ref, l2_ref)
    o_slabs, lses = [], []
    for gi, (_, dil) in enumerate(GROUPS):
        if not prompt or dil == 1:
            o_slabs.append([o_refs[gi][:, cc * LANES:(cc + 1) * LANES].astype(F32)
                            for cc in range(GROUP_SLABS)])
            lses.append(l_refs[gi][...])
            continue
        o_sc, l_sc = scratch[2 * (gi - 1)], scratch[2 * (gi - 1) + 1]
        piece = tm // dil
        isub = (pl.program_id(0) * n_tiles + pl.program_id(1)) % _span_tiles(dil, tm)
        for r in range(dil):
            start = pl.multiple_of(r * SUB_BLOCK + isub * piece, BF16_ROWS)
            for cc in range(GROUP_SLABS):
                o_sc[cc, pl.ds(r, piece, stride=dil), :] = (
                    o_refs[gi][pl.ds(start, piece), cc * LANES:(cc + 1) * LANES].astype(F32))
            l_sc[pl.ds(r, piece, stride=dil), :] = l_refs[gi][pl.ds(start, piece), :]
        o_slabs.append([o_sc[cc] for cc in range(GROUP_SLABS)])
        lses.append(l_sc[...])

    m = jnp.maximum(jnp.maximum(lses[0], lses[1]), lses[2])
    ws = [jnp.exp(l - m) for l in lses]
    inv = 1.0 / (ws[0] + ws[1] + ws[2])
    ws = [w * inv for w in ws]
    lane = lax.broadcasted_iota(jnp.int32, (tm, LANES), 1)
    first_head = lane < HEAD_DIM
    b_cols = []
    for p in range(GROUP_SLABS):
        acc = jnp.zeros((tm, LANES), F32)
        for w, slabs in zip(ws, o_slabs):
            w_pair = jnp.where(first_head, w[:, 2 * p:2 * p + 1], w[:, 2 * p + 1:2 * p + 2])
            acc = acc + w_pair * slabs[p]
        b_cols.append(acc)
    b = jnp.concatenate(b_cols, axis=1)

    gate_a = jax.nn.sigmoid(_dot(h, wg_ref[:, 0:d_model]))
    gate_b = jax.nn.sigmoid(_dot(h, wg_ref[:, d_model:2 * d_model]))
    merged = gate_a * _dot(a.astype(BF16), woa_ref[...]) + gate_b * _dot(b.astype(BF16), wob_ref[...])
    x = x + _dot(merged.astype(BF16), wo_ref[...])

    h2 = _rms(x, gffn_ref[...]).astype(BF16)
    d_ff = wff1_ref.shape[1]
    ff_step = 1024
    ffn = jnp.zeros((tm, d_model), F32)
    for c0 in range(0, d_ff, ff_step):
        hid = jnp.square(jnp.maximum(_dot(h2, wff1_ref[:, c0:c0 + ff_step]), 0.0))
        ffn = ffn + _dot(hid.astype(BF16), wff2_ref[c0:c0 + ff_step, :])
    x = x + ffn

    gate = jax.nn.sigmoid(_dot(_rms(x, gple_ref[...]).astype(BF16), wpg_ref[...]))
    y_ref[...] = x + gate * _dot(pe_ref[...].astype(BF16), wpp_ref[...])


def _merge_call(x2, outs, lses, pe2, wts, ws_mat, bias, *, prompt, batch, tm):
    n_tok, d_model = x2.shape
    seq = n_tok // batch
    n_tiles = seq // tm
    d_a = wts["g_v"].shape[1]
    tile_index = lambda b, i: b * n_tiles + i
    tok = lambda b, i: (tile_index(b, i), 0)
    row_spec = lambda width: pl.BlockSpec((tm, width), tok)

    def group_spec(dil, width):
        st = _span_tiles(dil, tm) if prompt else 1
        return pl.BlockSpec((st * tm, width), lambda b, i: (tile_index(b, i) // st, 0))

    names = ("g_mix", "w_uv", "w_gates", "g_v", "ws", "bias", "w_oa", "w_ob", "w_o",
             "g_ffn", "w_ff1", "w_ff2", "g_ple", "w_ple_gate", "w_ple_proj")
    consts = dict(wts, ws=ws_mat, bias=bias)
    in_specs = ([row_spec(d_model)]
                + [group_spec(dil, D_GROUP) for _, dil in GROUPS]
                + [group_spec(dil, LANES) for _, dil in GROUPS]
                + [row_spec(pe2.shape[1])] + [_resident(consts[k].shape) for k in names])
    scratch = []
    if prompt:
        va_shape = jax.ShapeDtypeStruct((batch, CHUNK, d_a), F32)
        va_spec = pl.BlockSpec((1, CHUNK, d_a), lambda b, i: (b, 0, 0))
        for _, dil in GROUPS:
            if dil > 1:
                scratch += [pltpu.VMEM((GROUP_SLABS, tm, LANES), F32), pltpu.VMEM((tm, LANES), F32)]
    else:
        va_shape = jax.ShapeDtypeStruct((n_tok, d_a), F32)
        va_spec = row_spec(d_a)
    return pl.pallas_call(
        functools.partial(_merge_kernel, prompt=prompt, tm=tm, n_tiles=n_tiles),
        grid=(batch, n_tiles),
        in_specs=in_specs,
        out_specs=[row_spec(d_model), va_spec],
        out_shape=[jax.ShapeDtypeStruct((n_tok, d_model), F32), va_shape],
        scratch_shapes=scratch,
        compiler_params=pltpu.CompilerParams(
            dimension_semantics=("arbitrary", "arbitrary"), vmem_limit_bytes=VMEM_LIMIT),
        name="merge_prompt" if prompt else "merge_sample",
    )(x2, *outs, *lses, pe2, *[consts[k] for k in names])


def kernel(x_prompt, x_sample, p_prompt, p_sample, cache_kv_w128, cache_kv_w512, cache_kv_w2048,
           g_mix, w_in, g_v, w_s, b_s, g_q, g_k, w_oa, w_ob, w_o,
           g_ffn, w_ff1, w_ff2, g_ple, w_ple_gate, w_ple_proj):
    depth = w_in.shape[0]
    batch, seq, d_model = x_prompt.shape
    dec_batch, t_new, _ = x_sample.shape
    d_a = g_v.shape[1]
    n_sample = dec_batch * t_new
    assert n_sample == CHUNK and w_in.shape[2] == 2 * d_a + 3 * D_QKV + 2 * d_model
    caches = (cache_kv_w128, cache_kv_w512, cache_kv_w2048)

    xp = x_prompt.reshape(batch * seq, d_model)
    xs = x_sample.reshape(n_sample, d_model)
    pos_p = jnp.arange(seq, dtype=jnp.int32)
    pos_s = PAST_LEN + jnp.tile(jnp.arange(t_new, dtype=jnp.int32), dec_batch)

    kv_p = [[] for _ in GROUPS]
    kv_s = [[] for _ in GROUPS]
    cv_p, cv_s = [], []
    for l in range(depth):
        w_l = w_in[l].astype(BF16)
        wts = {
            "g_mix": g_mix[l][None, :], "w_uv": w_l[:, 0:2 * d_a],
            "w_gates": w_l[:, 2 * d_a + 3 * D_QKV:], "g_v": g_v[l][None, :],
            "w_oa": w_oa[l].astype(BF16), "w_ob": w_ob[l].astype(BF16), "w_o": w_o[l].astype(BF16),
            "g_ffn": g_ffn[l][None, :], "w_ff1": w_ff1[l].astype(BF16), "w_ff2": w_ff2[l].astype(BF16),
            "g_ple": g_ple[l][None, :], "w_ple_gate": w_ple_gate[l].astype(BF16),
            "w_ple_proj": w_ple_proj[l].astype(BF16),
        }
        w_qkv = w_l[:, 2 * d_a:2 * d_a + 3 * D_QKV]
        c_g = d_a // G_A
        ws_p = w_s[l].astype(BF16)
        bias_p = jnp.repeat(b_s[l].T, c_g, axis=1)
        eye = jnp.eye(CHUNK // t_new, dtype=F32)
        ws_s = jnp.einsum("ab,gts->gatbs", eye, w_s[l][:, :t_new, :t_new]).reshape(G_A, CHUNK, CHUNK)
        ws_s = ws_s.astype(BF16)
        bias_s = jnp.tile(jnp.repeat(b_s[l][:, :t_new].T, c_g, axis=1), (CHUNK // t_new, 1))

        tm = 256
        res = _qkv_call(xp, pos_p, g_mix[l], w_qkv, g_q[l], g_k[l], prompt=True, batch=batch, tm=tm)
        q_g, k_g, v_g, kv_tails = res[0:3], res[3:6], res[6:9], res[9:12]
        outs, lses = zip(*[_attn_prompt_call(q_g[gi], k_g[gi], v_g[gi], gi, batch, seq)
                           for gi in range(N_GROUPS)])
        xp, va_p = _merge_call(xp, outs, lses, p_prompt[l].reshape(batch * seq, -1), wts, ws_p, bias_p,
                               prompt=True, batch=batch, tm=tm)
        for gi, st in enumerate(kv_tails):
            kv_p[gi].append(jnp.transpose(st, (0, 1, 4, 2, 3)))
        cv_p.append(va_p)

        qs, knt, vnt = _qkv_call(
            xs, pos_s, g_mix[l], w_qkv, g_q[l], g_k[l], prompt=False, batch=1, tm=n_sample)
        outs, lses = [], []
        for gi in range(N_GROUPS):
            o, lse, nbuf = _attn_sample_call(qs, knt, vnt, caches[gi][l], gi, dec_batch, t_new)
            outs.append(o)
            lses.append(lse)
            kv_s[gi].append(nbuf)
        xs, va_s = _merge_call(xs, outs, lses, p_sample[l].reshape(n_sample, -1), wts, ws_s, bias_s,
                               prompt=False, batch=1, tm=n_sample)
        cv_s.append(va_s.reshape(dec_batch, t_new, d_a))

    return (xp.reshape(batch, seq, d_model), xs.reshape(dec_batch, t_new, d_model),
            jnp.stack(kv_p[0]), jnp.stack(kv_p[1]), jnp.stack(kv_p[2]),
            jnp.stack(kv_s[0]), jnp.stack(kv_s[1]), jnp.stack(kv_s[2]),
            jnp.stack(cv_p), jnp.stack(cv_s))
```

```python
import functools
import math

import jax
import jax.numpy as jnp
from jax import lax
from jax.experimental import pallas as pl
from jax.experimental.pallas import tpu as pltpu

F32 = jnp.float32
BF16 = jnp.bfloat16

LANES = 128
BF16_ROWS = 16
HEAD_DIM = 64
H_G = 8
HEADS_PER_VREG = LANES // HEAD_DIM
GROUPS = ((128, 1), (512, 4), (2048, 16))
N_GROUPS = len(GROUPS)
D_GROUP = H_G * HEAD_DIM
GROUP_SLABS = D_GROUP // LANES
D_QKV = N_GROUPS * D_GROUP
N_KEYS = 129
SUB_BLOCK = 128
ROT_DIM = HEAD_DIM // 4
ROT_HALF = ROT_DIM // 2
ROPE_THETA = 500000.0
CHUNK = 128
G_A = 4
PAST_LEN = 8192
ATTN_SUB_BLOCKS = 4
EPS = 1e-6
NEG = -1e30
LN2 = math.log(2.0)
LOG2E = 1.0 / LN2
VMEM_LIMIT = 56 * 1024 * 1024


def _rms(x, g):
    return x * lax.rsqrt(jnp.mean(x * x, axis=-1, keepdims=True) + EPS) * g


def _resident(shape):
    zeros = (0,) * len(shape)
    return pl.BlockSpec(shape, lambda *_: zeros, pipeline_mode=pl.Buffered(1))


def _dot(a, b):
    return jnp.dot(a, b, preferred_element_type=F32)


def _dot_nt(a, b):
    return lax.dot_general(a, b, (((1,), (1,)), ((), ())), preferred_element_type=F32)


def _span_tiles(dil, tm):
    if dil == 1:
        return 1
    span = SUB_BLOCK * dil
    assert span % tm == 0 and (tm // dil) % BF16_ROWS == 0
    return span // tm


def _project_qkv(x_ref, gmix_ref, w_ref, gq_ref, gk_ref, rc_ref, ru_ref, rd_ref, bd_ref, store):
    h = _rms(x_ref[...], gmix_ref[...]).astype(BF16)
    rc, ru, rd = rc_ref[...], ru_ref[...], rd_ref[...]
    bd = bd_ref[...]

    def norm_rot(z, g):
        ss = _dot((z * z).astype(BF16), bd)
        zn = z * lax.rsqrt(ss * (1.0 / HEAD_DIM) + EPS) * g
        return zn * rc + pltpu.roll(zn, LANES - ROT_HALF, 1) * ru + pltpu.roll(zn, ROT_HALF, 1) * rd

    for ti, g_ref in enumerate((gq_ref, gk_ref, None)):
        z = _dot(h, w_ref[:, ti * D_QKV:(ti + 1) * D_QKV])
        for c in range(D_QKV // LANES):
            zc = z[:, c * LANES:(c + 1) * LANES]
            store(ti, c, zc if g_ref is None else norm_rot(zc, g_ref[...]))


def _qkv_prompt_kernel(*refs, tm, n_tiles):
    ins, outs, kv_refs, slabs = refs[:9], refs[9:18], refs[18:21], refs[21:24]

    def store(ti, c, val):
        slabs[ti][c] = val

    _project_qkv(*ins, store)

    i = pl.program_id(1)
    for ti, slab in enumerate(slabs):
        for gi, (_, dil) in enumerate(GROUPS):
            out = outs[ti * N_GROUPS + gi]
            piece = tm // dil
            isub = i % _span_tiles(dil, tm)
            for cc in range(GROUP_SLABS):
                c = gi * GROUP_SLABS + cc
                cols = slice(cc * LANES, (cc + 1) * LANES)
                if dil == 1:
                    out[:, cols] = slab[c].astype(BF16)
                    continue
                for r in range(dil):
                    start = pl.multiple_of(r * SUB_BLOCK + isub * piece, BF16_ROWS)
                    out[pl.ds(start, piece), cols] = slab[c, pl.ds(r, piece, stride=dil), :].astype(BF16)

    seq = n_tiles * tm
    for gi, (out_ref, (win, _)) in enumerate(zip(kv_refs, GROUPS)):
        rows = min(win, tm)
        first = (seq - win) // tm if win >= tm else n_tiles - 1

        @pl.when(i >= first)
        def _(out_ref=out_ref, gi=gi, rows=rows):
            for cc in range(GROUP_SLABS):
                c = gi * GROUP_SLABS + cc
                for kv in range(2):
                    t = slabs[1 + kv][c, tm - rows:tm, :].T
                    for hh in range(HEADS_PER_VREG):
                        out_ref[0, kv, cc * HEADS_PER_VREG + hh] = t[hh * HEAD_DIM:(hh + 1) * HEAD_DIM, :]


def _qkv_sample_kernel(*refs):
    ins, outs = refs[:9], refs[9:12]

    def store(ti, c, val):
        if ti == 0:
            outs[ti][:, c * LANES:(c + 1) * LANES] = val
        else:
            outs[ti][c * LANES:(c + 1) * LANES, :] = val.T

    _project_qkv(*ins, store)


def _rotary_tables(pos):
    inv = ROPE_THETA ** (-jnp.arange(ROT_HALF, dtype=F32) * 2.0 / ROT_DIM)
    ang = pos.astype(F32)[:, None] * inv[None, :]
    cos, sin = jnp.cos(ang), jnp.sin(ang)
    n = pos.shape[0]
    pad = jnp.zeros((n, HEAD_DIM - ROT_DIM), F32)
    zero = jnp.zeros((n, ROT_HALF), F32)
    rc = jnp.concatenate([cos, cos, pad + 1.0], axis=1)
    ru = jnp.concatenate([-sin, zero, pad], axis=1)
    rd = jnp.concatenate([zero, sin, pad], axis=1)
    return tuple(jnp.tile(t, (1, HEADS_PER_VREG)) for t in (rc, ru, rd))


def _qkv_call(x2, pos, g_mix, w_qkv, g_q, g_k, *, prompt, batch, tm):
    n_tok, d_model = x2.shape
    seq = n_tok // batch
    n_tiles = seq // tm
    rc, ru, rd = _rotary_tables(pos)
    gq = jnp.tile(g_q.astype(F32), HEADS_PER_VREG)[None, :] * (LOG2E * HEAD_DIM ** -0.5)
    gk = jnp.tile(g_k.astype(F32), HEADS_PER_VREG)[None, :]
    lane = jnp.arange(LANES)
    bd = (lane[:, None] // HEAD_DIM == lane[None, :] // HEAD_DIM).astype(BF16)

    tile_index = lambda b, i: b * n_tiles + i
    tok = lambda b, i: (tile_index(b, i), 0)
    tab = lambda b, i: (i, 0)
    in_specs = [
        pl.BlockSpec((tm, d_model), tok),
        _resident((1, d_model)),
        _resident((d_model, 3 * D_QKV)),
        _resident((1, LANES)),
        _resident((1, LANES)),
        pl.BlockSpec((tm, LANES), tab),
        pl.BlockSpec((tm, LANES), tab),
        pl.BlockSpec((tm, LANES), tab),
        _resident((LANES, LANES)),
    ]
    if prompt:
        out_shape, out_specs = [], []
        for _ in range(3):
            for _, dil in GROUPS:
                st = _span_tiles(dil, tm)
                out_shape.append(jax.ShapeDtypeStruct((n_tok, D_GROUP), BF16))
                out_specs.append(pl.BlockSpec((st * tm, D_GROUP),
                                              lambda b, i, st=st: (tile_index(b, i) // st, 0)))
        for win, _ in GROUPS:
            rows = min(win, tm)
            first = (seq - win) // tm if win >= tm else n_tiles - 1
            out_shape.append(jax.ShapeDtypeStruct((batch, 2, H_G, HEAD_DIM, win), F32))
            out_specs.append(pl.BlockSpec(
                (1, 2, H_G, HEAD_DIM, rows),
                lambda b, i, first=first: (b, 0, 0, 0, jnp.maximum(i - first, 0))))
        scratch = [pltpu.VMEM((D_QKV // LANES, tm, LANES), F32)] * 3
        body = functools.partial(_qkv_prompt_kernel, tm=tm, n_tiles=n_tiles)
    else:
        assert n_tiles == 1 and batch == 1 and tm == LANES
        out_shape = [jax.ShapeDtypeStruct((n_tok, D_QKV), F32)] + [jax.ShapeDtypeStruct((D_QKV, n_tok), F32)] * 2
        out_specs = [pl.BlockSpec((tm, D_QKV), tok)] + [pl.BlockSpec((D_QKV, tm), lambda b, i: (0, 0))] * 2
        scratch = []
        body = _qkv_sample_kernel

    return pl.pallas_call(
        body,
        grid=(batch, n_tiles),
        in_specs=in_specs,
        out_specs=out_specs,
        out_shape=out_shape,
        scratch_shapes=scratch,
        compiler_params=pltpu.CompilerParams(
            dimension_semantics=("arbitrary", "arbitrary"), vmem_limit_bytes=VMEM_LIMIT),
        name="qkv_prompt" if prompt else "qkv_sample",
    )(x2, g_mix[None, :], w_qkv, gq, gk, rc, ru, rd, bd)


def _attn_prompt_kernel(q_ref, kc_ref, kp_ref, vc_ref, vp_ref, o_ref, lse_ref, *, natural, lookback):
    step = pl.program_id(1)
    key = lax.broadcasted_iota(jnp.int32, (2 * SUB_BLOCK, HEADS_PER_VREG * SUB_BLOCK), 0)
    qry = lax.broadcasted_iota(jnp.int32, (2 * SUB_BLOCK, HEADS_PER_VREG * SUB_BLOCK), 1) % SUB_BLOCK
    vis_cur = key <= qry
    vis_prev = key - SUB_BLOCK >= qry
    lane16 = lax.broadcasted_iota(jnp.int32, (SUB_BLOCK, LANES), 1).astype(F32).astype(BF16)
    head_lanes = (lane16 < HEAD_DIM, lane16 >= HEAD_DIM)
    eye_r = lax.broadcasted_iota(jnp.int32, (LANES, LANES), 0)
    eye_c = lax.broadcasted_iota(jnp.int32, (LANES, LANES), 1)
    eye = (eye_r == eye_c).astype(F32).astype(BF16)
    ones_rows = jnp.ones((BF16_ROWS, 2 * SUB_BLOCK), BF16)
    head_row = lax.broadcasted_iota(jnp.int32, (H_G, SUB_BLOCK), 0)

    pad = jnp.zeros((LANES - H_G, SUB_BLOCK), F32)
    slabs = [slice(p * LANES, (p + 1) * LANES) for p in range(GROUP_SLABS)]

    if natural:
        blocks = [(0, kp_ref, vp_ref, 0, step > 0)]
        blocks += [(i * SUB_BLOCK, kc_ref, vc_ref, (i - 1) * SUB_BLOCK, True)
                   for i in range(1, ATTN_SUB_BLOCKS)]
    else:
        blocks = [(i * SUB_BLOCK, kp_ref, vp_ref, i * SUB_BLOCK, step >= lookback)
                  for i in range(ATTN_SUB_BLOCKS)]

    scores, vts = [], []
    for r0, kp_src, vp_src, rp, _ in blocks:
        rows, prev_rows = slice(r0, r0 + SUB_BLOCK), slice(rp, rp + SUB_BLOCK)
        for sl in slabs:
            q = q_ref[rows, sl]
            qq = jnp.concatenate([jnp.where(hl, q, jnp.zeros_like(q)) for hl in head_lanes], axis=0)
            kk = jnp.concatenate([kc_ref[rows, sl], kp_src[prev_rows, sl]], axis=0)
            vv = jnp.concatenate([vc_ref[rows, sl], vp_src[prev_rows, sl]], axis=0)
            scores.append(_dot_nt(kk, qq))
            vts.append(jnp.concatenate([_dot_nt(eye, vv).astype(BF16), ones_rows], axis=0))
    maxes, probs = [], []
    for bi, (_, _, _, _, has_prev) in enumerate(blocks):
        visible = jnp.logical_or(vis_cur, jnp.logical_and(vis_prev, has_prev))
        for s in scores[bi * GROUP_SLABS:(bi + 1) * GROUP_SLABS]:
            s = jnp.where(visible, s, NEG)
            m = jnp.max(s, axis=0, keepdims=True)
            maxes.append(m)
            probs.append(jnp.exp2(s - m).astype(BF16))
    outs = [_dot(vt, e) for vt, e in zip(vts, probs)]
    for bi, (r0, _, _, _, _) in enumerate(blocks):
        rows = slice(r0, r0 + SUB_BLOCK)
        lse_rows = jnp.zeros((H_G, SUB_BLOCK), F32)
        for p, sl in enumerate(slabs):
            ot, m = outs[bi * GROUP_SLABS + p], maxes[bi * GROUP_SLABS + p]
            den = ot[LANES:LANES + 1, :]
            lse = m * LN2 + jnp.log(den)
            inv = 1.0 / den
            o_halves = []
            for half in range(HEADS_PER_VREG):
                qs = slice(half * SUB_BLOCK, (half + 1) * SUB_BLOCK)
                o_halves.append(ot[half * HEAD_DIM:(half + 1) * HEAD_DIM, qs] * inv[:, qs])
                lse_rows = jnp.where(head_row == p * HEADS_PER_VREG + half, lse[:, qs], lse_rows)
            o_ref[rows, sl] = jnp.concatenate(o_halves, axis=0).T.astype(BF16)
        lse_ref[rows, :] = jnp.concatenate([lse_rows, pad], axis=0).T


def _attn_prompt_call(q, k, v, gi, batch, seq):
    dil = GROUPS[gi][1]
    natural = dil == 1
    rows = ATTN_SUB_BLOCKS * SUB_BLOCK
    assert seq % (SUB_BLOCK * dil) == 0 and seq % rows == 0
    assert natural or dil % ATTN_SUB_BLOCKS == 0
    steps = seq // rows
    lookback = 1 if natural else dil // ATTN_SUB_BLOCKS
    cur_map = lambda b, s: (b * steps + s, 0)
    cur = pl.BlockSpec((rows, D_GROUP), cur_map)
    if natural:
        prev = pl.BlockSpec((SUB_BLOCK, D_GROUP),
                            lambda b, s: (jnp.maximum((b * steps + s) * ATTN_SUB_BLOCKS - 1, 0), 0))
    else:
        prev = pl.BlockSpec((rows, D_GROUP), lambda b, s: (b * steps + jnp.maximum(s - lookback, 0), 0))
    return pl.pallas_call(
        functools.partial(_attn_prompt_kernel, natural=natural, lookback=lookback),
        grid=(batch, steps),
        in_specs=[cur, cur, prev, cur, prev],
        out_specs=[cur, pl.BlockSpec((rows, LANES), cur_map)],
        out_shape=[jax.ShapeDtypeStruct((batch * seq, D_GROUP), BF16),
                   jax.ShapeDtypeStruct((batch * seq, LANES), F32)],
        compiler_params=pltpu.CompilerParams(dimension_semantics=("arbitrary", "arbitrary")),
        name=f"attn_prompt_g{gi}",
    )(q, k, k, v, v)


def _attn_sample_kernel(q_ref, knt_ref, vnt_ref, buf_ref, o_ref, lse_ref, nbuf_ref, *, dil, wb, t_new):
    assert dil & (dil - 1) == 0
    q_rows = q_ref.shape[1]
    width = wb + LANES
    shift = lax.rem(LANES - t_new * pl.program_id(0), LANES)
    knt = pltpu.roll(knt_ref[...], shift, 1)
    vnt = pltpu.roll(vnt_ref[...], shift, 1)
    q = q_ref[0]

    t_row = lax.broadcasted_iota(jnp.int32, (q_rows, width), 0)
    p_col = lax.broadcasted_iota(jnp.int32, (q_rows, width), 1)
    back = wb + t_row - p_col
    valid = jnp.logical_and(jnp.logical_and(back >= 0, (back & (dil - 1)) == 0),
                            jnp.logical_and(back <= (N_KEYS - 1) * dil, p_col < wb + t_new))
    lane = lax.broadcasted_iota(jnp.int32, (q_rows, LANES), 1)
    lse_tile = jnp.zeros((q_rows, LANES), F32)
    o_heads = []
    for h in range(H_G):
        rows = slice(h * HEAD_DIM, (h + 1) * HEAD_DIM)
        kx = jnp.concatenate([buf_ref[0, 0, h], knt[rows, :]], axis=1)
        vx = jnp.concatenate([buf_ref[0, 1, h], vnt[rows, :]], axis=1)
        nbuf_ref[0, 0, h] = pltpu.roll(kx, width - t_new, 1)[:, :wb]
        nbuf_ref[0, 1, h] = pltpu.roll(vx, width - t_new, 1)[:, :wb]
        s = jnp.where(valid, _dot(q[:, rows].astype(BF16), kx.astype(BF16)), NEG)
        m = jnp.max(s, axis=1, keepdims=True)
        e = jnp.exp2(s - m)
        den = jnp.sum(e, axis=1, keepdims=True)
        o_heads.append(_dot_nt(e.astype(BF16), vx.astype(BF16)) / den)
        lse_tile = jnp.where(lane == h, m * LN2 + jnp.log(den), lse_tile)
    o_ref[0] = jnp.concatenate(o_heads, axis=1)
    lse_ref[0] = lse_tile


def _attn_sample_call(q, knt, vnt, buf, gi, batch, t_new):
    dil = GROUPS[gi][1]
    wb = buf.shape[2]
    assert wb % LANES == 0 and LANES % t_new == 0 and batch * t_new <= LANES
    q_rows = 8
    q_pad = jnp.pad(q.reshape(batch, t_new, D_QKV), ((0, 0), (0, q_rows - t_new), (0, 0)))
    buf_t = jnp.transpose(buf, (0, 1, 3, 4, 2))
    new_spec = pl.BlockSpec((D_GROUP, LANES), lambda b: (gi, 0))
    buf_spec = pl.BlockSpec((1, 2, H_G, HEAD_DIM, wb), lambda b: (b, 0, 0, 0, 0))
    o, lse, nbuf = pl.pallas_call(
        functools.partial(_attn_sample_kernel, dil=dil, wb=wb, t_new=t_new),
        grid=(batch,),
        in_specs=[pl.BlockSpec((1, q_rows, D_GROUP), lambda b: (b, 0, gi)), new_spec, new_spec, buf_spec],
        out_specs=[pl.BlockSpec((1, q_rows, D_GROUP), lambda b: (b, 0, 0)),
                   pl.BlockSpec((1, q_rows, LANES), lambda b: (b, 0, 0)),
                   buf_spec],
        out_shape=[jax.ShapeDtypeStruct((batch, q_rows, D_GROUP), F32),
                   jax.ShapeDtypeStruct((batch, q_rows, LANES), F32),
                   jax.ShapeDtypeStruct((batch, 2, H_G, HEAD_DIM, wb), F32)],
        compiler_params=pltpu.CompilerParams(
            dimension_semantics=("arbitrary",), vmem_limit_bytes=VMEM_LIMIT),
        name=f"attn_sample_g{gi}",
    )(q_pad, knt, vnt, buf_t)
    return (o[:, :t_new].reshape(batch * t_new, D_GROUP), lse[:, :t_new].reshape(batch * t_new, LANES),
            jnp.transpose(nbuf, (0, 1, 4, 2, 3)))


def _merge_kernel(x_ref, o0_ref, o1_ref, o2_ref, l0_ref, l1_ref, l2_ref, pe_ref,
                  gmix_ref, wuv_ref, wg_ref, gv_ref, ws_ref, bias_ref, woa_ref, wob_ref, wo_ref,
                  gffn_ref, wff1_ref, wff2_ref, gple_ref, wpg_ref, wpp_ref,
                  y_ref, va_ref, *scratch, prompt, tm, n_tiles):
    d_model = x_ref.shape[1]
    d_a = gv_ref.shape[1]
    c_g = d_a // G_A
    x = x_ref[...]
    h = _rms(x, gmix_ref[...]).astype(BF16)

    u = jax.nn.gelu(_dot(h, wuv_ref[:, 0:d_a]))
    va = _rms(jax.nn.gelu(_dot(h, wuv_ref[:, d_a:2 * d_a])), gv_ref[...])
    if prompt:
        @pl.when(pl.program_id(1) == n_tiles - 1)
        def _():
            va_ref[0] = va[tm - CHUNK:tm, :]
    else:
        va_ref[...] = va
    va16 = va.astype(BF16)
    r_i = lax.broadcasted_iota(jnp.int32, (CHUNK, CHUNK), 0)
    c_i = lax.broadcasted_iota(jnp.int32, (CHUNK, CHUNK), 1)
    w_tril = [jnp.where(r_i >= c_i, ws_ref[g], jnp.zeros((CHUNK, CHUNK), BF16)) for g in range(G_A)]
    mix_rows = []
    for ci in range(tm // CHUNK):
        rs = slice(ci * CHUNK, (ci + 1) * CHUNK)
        cols = [_dot(w_tril[g], va16[rs, g * c_g:(g + 1) * c_g]) for g in range(G_A)]
        mix_rows.append(jnp.concatenate(cols, axis=1) + bias_ref[...])
    a = u * jnp.concatenate(mix_rows, axis=0)

    o_refs, l_refs = (o0_ref, o1_ref, o2_ref), (l0_ref, l1_ref, l2_ref)
    o_slabs, lses = [], []
    for gi, (_, dil) in enumerate(GROUPS):
        if not prompt or dil == 1:
            o_slabs.append([o_refs[gi][:, cc * LANES:(cc + 1) * LANES].astype(F32)
                            for cc in range(GROUP_SLABS)])
            lses.append(l_refs[gi][...])
            continue
        o_sc, l_sc = scratch[2 * (gi - 1)], scratch[2 * (gi - 1) + 1]
        piece = tm // dil
        isub = (pl.program_id(0) * n_tiles + pl.program_id(1)) % _span_tiles(dil, tm)
        for r in range(dil):
            start = pl.multiple_of(r * SUB_BLOCK + isub * piece, BF16_ROWS)
            for cc in range(GROUP_SLABS):
                o_sc[cc, pl.ds(r, piece, stride=dil), :] = (
                    o_refs[gi][pl.ds(start, piece), cc * LANES:(cc + 1) * LANES].astype(F32))
            l_sc[pl.ds(r, piece, stride=dil), :] = l_refs[gi][pl.ds(start, piece), :]
        o_slabs.append([o_sc[cc] for cc in range(GROUP_SLABS)])
        lses.append(l_sc[...])

    m = jnp.maximum(jnp.maximum(lses[0], lses[1]), lses[2])
    ws = [jnp.exp(l - m) for l in lses]
    inv = 1.0 / (ws[0] + ws[1] + ws[2])
    ws = [w * inv for w in ws]
    lane = lax.broadcasted_iota(jnp.int32, (tm, LANES), 1)
    first_head = lane < HEAD_DIM
    b_cols = []
    for p in range(GROUP_SLABS):
        acc = jnp.zeros((tm, LANES), F32)
        for w, slabs in zip(ws, o_slabs):
            w_pair = jnp.where(first_head, w[:, 2 * p:2 * p + 1], w[:, 2 * p + 1:2 * p + 2])
            acc = acc + w_pair * slabs[p]
        b_cols.append(acc)
    b = jnp.concatenate(b_cols, axis=1)

    gate_a = jax.nn.sigmoid(_dot(h, wg_ref[:, 0:d_model]))
    gate_b = jax.nn.sigmoid(_dot(h, wg_ref[:, d_model:2 * d_model]))
    merged = gate_a * _dot(a.astype(BF16), woa_ref[...]) + gate_b * _dot(b.astype(BF16), wob_ref[...])
    x = x + _dot(merged.astype(BF16), wo_ref[...])

    h2 = _rms(x, gffn_ref[...]).astype(BF16)
    d_ff = wff1_ref.shape[1]
    ff_step = 1024
    ffn = jnp.zeros((tm, d_model), F32)
    for c0 in range(0, d_ff, ff_step):
        hid = jnp.square(jnp.maximum(_dot(h2, wff1_ref[:, c0:c0 + ff_step]), 0.0))
        ffn = ffn + _dot(hid.astype(BF16), wff2_ref[c0:c0 + ff_step, :])
    x = x + ffn

    gate = jax.nn.sigmoid(_dot(_rms(x, gple_ref[...]).astype(BF16), wpg_ref[...]))
    y_ref[...] = x + gate * _dot(pe_ref[...].astype(BF16), wpp_ref[...])


def _merge_call(x2, outs, lses, pe2, wts, ws_mat, bias, *, prompt, batch, tm):
    n_tok, d_model = x2.shape
    seq = n_tok // batch
    n_tiles = seq // tm
    d_a = wts["g_v"].shape[1]
    tile_index = lambda b, i: b * n_tiles + i
    tok = lambda b, i: (tile_index(b, i), 0)
    row_spec = lambda width: pl.BlockSpec((tm, width), tok)

    def group_spec(dil, width):
        st = _span_tiles(dil, tm) if prompt else 1
        return pl.BlockSpec((st * tm, width), lambda b, i: (tile_index(b, i) // st, 0))

    names = ("g_mix", "w_uv", "w_gates", "g_v", "ws", "bias", "w_oa", "w_ob", "w_o",
             "g_ffn", "w_ff1", "w_ff2", "g_ple", "w_ple_gate", "w_ple_proj")
    consts = dict(wts, ws=ws_mat, bias=bias)
    in_specs = ([row_spec(d_model)]
                + [group_spec(dil, D_GROUP) for _, dil in GROUPS]
                + [group_spec(dil, LANES) for _, dil in GROUPS]
                + [row_spec(pe2.shape[1])] + [_resident(consts[k].shape) for k in names])
    scratch = []
    if prompt:
        va_shape = jax.ShapeDtypeStruct((batch, CHUNK, d_a), F32)
        va_spec = pl.BlockSpec((1, CHUNK, d_a), lambda b, i: (b, 0, 0))
        for _, dil in GROUPS:
            if dil > 1:
                scratch += [pltpu.VMEM((GROUP_SLABS, tm, LANES), F32), pltpu.VMEM((tm, LANES), F32)]
    else:
        va_shape = jax.ShapeDtypeStruct((n_tok, d_a), F32)
        va_spec = row_spec(d_a)
    return pl.pallas_call(
        functools.partial(_merge_kernel, prompt=prompt, tm=tm, n_tiles=n_tiles),
        grid=(batch, n_tiles),
        in_specs=in_specs,
        out_specs=[row_spec(d_model), va_spec],
        out_shape=[jax.ShapeDtypeStruct((n_tok, d_model), F32), va_shape],
        scratch_shapes=scratch,
        compiler_params=pltpu.CompilerParams(
            dimension_semantics=("arbitrary", "arbitrary"), vmem_limit_bytes=VMEM_LIMIT),
        name="merge_prompt" if prompt else "merge_sample",
    )(x2, *outs, *lses, pe2, *[consts[k] for k in names])


def kernel(x_prompt, x_sample, p_prompt, p_sample, cache_kv_w128, cache_kv_w512, cache_kv_w2048,
           g_mix, w_in, g_v, w_s, b_s, g_q, g_k, w_oa, w_ob, w_o,
           g_ffn, w_ff1, w_ff2, g_ple, w_ple_gate, w_ple_proj):
    depth = w_in.shape[0]
    batch, seq, d_model = x_prompt.shape
    dec_batch, t_new, _ = x_sample.shape
    d_a = g_v.shape[1]
    n_sample = dec_batch * t_new
    assert n_sample == CHUNK and w_in.shape[2] == 2 * d_a + 3 * D_QKV + 2 * d_model
    caches = (cache_kv_w128, cache_kv_w512, cache_kv_w2048)

    xp = x_prompt.reshape(batch * seq, d_model)
    xs = x_sample.reshape(n_sample, d_model)
    pos_p = jnp.arange(seq, dtype=jnp.int32)
    pos_s = PAST_LEN + jnp.tile(jnp.arange(t_new, dtype=jnp.int32), dec_batch)

    kv_p = [[] for _ in GROUPS]
    kv_s = [[] for _ in GROUPS]
    cv_p, cv_s = [], []
    for l in range(depth):
        w_l = w_in[l].astype(BF16)
        wts = {
            "g_mix": g_mix[l][None, :], "w_uv": w_l[:, 0:2 * d_a],
            "w_gates": w_l[:, 2 * d_a + 3 * D_QKV:], "g_v": g_v[l][None, :],
            "w_oa": w_oa[l].astype(BF16), "w_ob": w_ob[l].astype(BF16), "w_o": w_o[l].astype(BF16),
            "g_ffn": g_ffn[l][None, :], "w_ff1": w_ff1[l].astype(BF16), "w_ff2": w_ff2[l].astype(BF16),
            "g_ple": g_ple[l][None, :], "w_ple_gate": w_ple_gate[l].astype(BF16),
            "w_ple_proj": w_ple_proj[l].astype(BF16),
        }
        w_qkv = w_l[:, 2 * d_a:2 * d_a + 3 * D_QKV]
        c_g = d_a // G_A
        ws_p = w_s[l].astype(BF16)
        bias_p = jnp.repeat(b_s[l].T, c_g, axis=1)
        eye = jnp.eye(CHUNK // t_new, dtype=F32)
        ws_s = jnp.einsum("ab,gts->gatbs", eye, w_s[l][:, :t_new, :t_new]).reshape(G_A, CHUNK, CHUNK)
        ws_s = ws_s.astype(BF16)
        bias_s = jnp.tile(jnp.repeat(b_s[l][:, :t_new].T, c_g, axis=1), (CHUNK // t_new, 1))

        tm = 256
        res = _qkv_call(xp, pos_p, g_mix[l], w_qkv, g_q[l], g_k[l], prompt=True, batch=batch, tm=tm)
        q_g, k_g, v_g, kv_tails = res[0:3], res[3:6], res[6:9], res[9:12]
        outs, lses = zip(*[_attn_prompt_call(q_g[gi], k_g[gi], v_g[gi], gi, batch, seq)
                           for gi in range(N_GROUPS)])
        xp, va_p = _merge_call(xp, outs, lses, p_prompt[l].reshape(batch * seq, -1), wts, ws_p, bias_p,
                               prompt=True, batch=batch, tm=tm)
        for gi, st in enumerate(kv_tails):
            kv_p[gi].append(jnp.transpose(st, (0, 1, 4, 2, 3)))
        cv_p.append(va_p)

        qs, knt, vnt = _qkv_call(
            xs, pos_s, g_mix[l], w_qkv, g_q[l], g_k[l], prompt=False, batch=1, tm=n_sample)
        outs, lses = [], []
        for gi in range(N_GROUPS):
            o, lse, nbuf = _attn_sample_call(qs, knt, vnt, caches[gi][l], gi, dec_batch, t_new)
            outs.append(o)
            lses.append(lse)
            kv_s[gi].append(nbuf)
        xs, va_s = _merge_call(xs, outs, lses, p_sample[l].reshape(n_sample, -1), wts, ws_s, bias_s,
                               prompt=False, batch=1, tm=n_sample)
        cv_s.append(va_s.reshape(dec_batch, t_new, d_a))

    return (xp.reshape(batch, seq, d_model), xs.reshape(dec_batch, t_new, d_model),
            jnp.stack(kv_p[0]), jnp.stack(kv_p[1]), jnp.stack(kv_p[2]),
            jnp.stack(kv_s[0]), jnp.stack(kv_s[1]), jnp.stack(kv_s[2]),
            jnp.stack(cv_p), jnp.stack(cv_s))
```

```python
import functools
import math

import jax
import jax.numpy as jnp
from jax import lax
from jax.experimental import pallas as pl
from jax.experimental.pallas import tpu as pltpu

F32 = jnp.float32
BF16 = jnp.bfloat16

LANES = 128
BF16_ROWS = 16
MXU_DIM = 256
TOKEN_TILE = 256
HEAD_DIM = 64
H_G = 8
HEADS_PER_VREG = LANES // HEAD_DIM
GROUPS = ((128, 1), (512, 4), (2048, 16))
N_GROUPS = len(GROUPS)
D_GROUP = H_G * HEAD_DIM
GROUP_SLABS = D_GROUP // LANES
D_QKV = N_GROUPS * D_GROUP
N_KEYS = 129
SUB_BLOCK = 128
ROT_DIM = HEAD_DIM // 4
ROT_HALF = ROT_DIM // 2
ROPE_THETA = 500000.0
CHUNK = 128
G_A = 4
PAST_LEN = 8192
ATTN_SUB_BLOCKS = 4
EPS = 1e-6
NEG = -1e30
LN2 = math.log(2.0)
LOG2E = 1.0 / LN2
VMEM_LIMIT = 56 * 1024 * 1024


def _rms(x, g):
    return x * lax.rsqrt(jnp.mean(x * x, axis=-1, keepdims=True) + EPS) * g


def _resident(shape):
    zeros = (0,) * len(shape)
    return pl.BlockSpec(shape, lambda *_: zeros, pipeline_mode=pl.Buffered(1))


def _dot(a, b):
    return jnp.dot(a, b, preferred_element_type=F32)


def _dot_nt(a, b):
    return lax.dot_general(a, b, (((1,), (1,)), ((), ())), preferred_element_type=F32)


def _project_qkv(x_ref, gmix_ref, w_ref, gq_ref, gk_ref, rot_ref, bd_ref, perm_refs, emit):
    h = _rms(x_ref[...], gmix_ref[...]).astype(BF16)
    hs = [h if p_ref is None else _dot(p_ref[...], h).astype(BF16) for p_ref in perm_refs]
    bd = bd_ref[...]
    wide = bd.shape[0]

    for ti, g_ref in enumerate((gq_ref, gk_ref, None)):
        for gi in range(N_GROUPS):
            c0 = ti * D_QKV + gi * D_GROUP
            z = _dot(hs[gi], w_ref[:, c0:c0 + D_GROUP])
            if g_ref is None:
                emit(ti, gi, [z[:, c * LANES:(c + 1) * LANES] for c in range(GROUP_SLABS)])
                continue
            rc, ru, rd = (rot_ref[min(gi, rot_ref.shape[0] - 1), k] for k in range(3))

            def rot(zn):
                return zn * rc + pltpu.roll(zn, LANES - ROT_HALF, 1) * ru + pltpu.roll(zn, ROT_HALF, 1) * rd

            vals = []
            for c in range(0, D_GROUP, wide):
                zc = z[:, c:c + wide]
                ss = _dot((zc * zc).astype(BF16), bd)
                zn = zc * lax.rsqrt(ss * (1.0 / HEAD_DIM) + EPS)
                vals += [rot(zn[:, j:j + LANES] * g_ref[...]) for j in range(0, wide, LANES)]
            emit(ti, gi, vals)


def _qkv_prompt_kernel(*refs, tm, n_tiles):
    n_perm = sum(dil > 1 for _, dil in GROUPS)
    ins, perms, rest = refs[:7], refs[7:7 + n_perm], refs[7 + n_perm:]
    outs, kv_refs = rest[:3 * N_GROUPS], rest[3 * N_GROUPS:4 * N_GROUPS]
    kv_slabs, order_slab = rest[4 * N_GROUPS:4 * N_GROUPS + 2], rest[4 * N_GROUPS + 2]
    perms = list(perms)
    perm_refs = [perms.pop(0) if dil > 1 else None for _, dil in GROUPS]

    def emit(ti, gi, vals):
        out = outs[ti * N_GROUPS + gi]
        for cc, val in enumerate(vals):
            out[:, cc * LANES:(cc + 1) * LANES] = val.astype(BF16)
            if ti > 0:
                kv_slabs[ti - 1][gi * GROUP_SLABS + cc] = val

    _project_qkv(*ins, perm_refs, emit)

    i = pl.program_id(1)
    seq = n_tiles * tm
    for gi, (win, dil) in enumerate(GROUPS):
        rows = min(win, tm)
        first = (seq - win) // tm if win >= tm else n_tiles - 1
        piece = tm // dil

        @pl.when(i >= first)
        def _(gi=gi, dil=dil, rows=rows, piece=piece):
            for kv in range(2):
                for cc in range(GROUP_SLABS):
                    c = gi * GROUP_SLABS + cc
                    if dil > 1:
                        for r in range(dil):
                            order_slab[pl.ds(r, piece, stride=dil), :] = kv_slabs[kv][c, r * piece:(r + 1) * piece, :]
                        val = order_slab[...]
                    else:
                        val = kv_slabs[kv][c]
                    t = val[tm - rows:tm].T
                    for hh in range(HEADS_PER_VREG):
                        kv_refs[gi][0, kv, cc * HEADS_PER_VREG + hh] = t[hh * HEAD_DIM:(hh + 1) * HEAD_DIM, :]


def _qkv_sample_kernel(*refs):
    ins, outs = refs[:7], refs[7:10]

    def emit(ti, gi, vals):
        for cc, val in enumerate(vals):
            c = gi * GROUP_SLABS + cc
            if ti == 0:
                outs[ti][:, c * LANES:(c + 1) * LANES] = val
            else:
                outs[ti][c * LANES:(c + 1) * LANES, :] = val.T

    _project_qkv(*ins, [None] * N_GROUPS, emit)


def _residue_major_source(dil, tm):
    row = jnp.arange(tm, dtype=jnp.int32)
    piece = tm // dil
    return (row % piece) * dil + row // piece


def _rotary_tables(pos):
    inv = ROPE_THETA ** (-jnp.arange(ROT_HALF, dtype=F32) * 2.0 / ROT_DIM)
    ang = pos.astype(F32)[:, None] * inv[None, :]
    cos, sin = jnp.cos(ang), jnp.sin(ang)
    n = pos.shape[0]
    pad = jnp.zeros((n, HEAD_DIM - ROT_DIM), F32)
    zero = jnp.zeros((n, ROT_HALF), F32)
    rc = jnp.concatenate([cos, cos, pad + 1.0], axis=1)
    ru = jnp.concatenate([-sin, zero, pad], axis=1)
    rd = jnp.concatenate([zero, sin, pad], axis=1)
    return tuple(jnp.tile(t, (1, HEADS_PER_VREG)) for t in (rc, ru, rd))


def _qkv_call(x2, pos, g_mix, w_qkv, g_q, g_k, *, prompt, batch, tm):
    n_tok, d_model = x2.shape
    seq = n_tok // batch
    n_tiles = seq // tm
    perms = []
    if prompt:
        assert all(tm % (dil * BF16_ROWS) == 0 for _, dil in GROUPS)
        tile_start = (jnp.arange(seq, dtype=jnp.int32) // tm) * tm
        tables = []
        for _, dil in GROUPS:
            src = _residue_major_source(dil, tm)
            tables.append(jnp.stack(_rotary_tables(pos[tile_start + jnp.tile(src, n_tiles)])))
            if dil > 1:
                perms.append((src[:, None] == jnp.arange(tm, dtype=jnp.int32)[None, :]).astype(BF16))
        rot = jnp.stack(tables)
    else:
        rot = jnp.stack(_rotary_tables(pos))[None]
    gq = jnp.tile(g_q.astype(F32), HEADS_PER_VREG)[None, :] * (LOG2E * HEAD_DIM ** -0.5)
    gk = jnp.tile(g_k.astype(F32), HEADS_PER_VREG)[None, :]
    lane = jnp.arange(MXU_DIM)
    bd = (lane[:, None] // HEAD_DIM == lane[None, :] // HEAD_DIM).astype(BF16)

    tile_index = lambda b, i: b * n_tiles + i
    tok = lambda b, i: (tile_index(b, i), 0)
    in_specs = [
        pl.BlockSpec((tm, d_model), tok),
        _resident((1, d_model)),
        _resident((d_model, 3 * D_QKV)),
        _resident((1, LANES)),
        _resident((1, LANES)),
        pl.BlockSpec(rot.shape[:2] + (tm, LANES), lambda b, i: (0, 0, i, 0)),
        _resident((MXU_DIM, MXU_DIM)),
    ] + [_resident(p.shape) for p in perms]
    if prompt:
        out_shape = [jax.ShapeDtypeStruct((n_tok, D_GROUP), BF16)] * (3 * N_GROUPS)
        out_specs = [pl.BlockSpec((tm, D_GROUP), tok)] * (3 * N_GROUPS)
        for win, _ in GROUPS:
            rows = min(win, tm)
            first = (seq - win) // tm if win >= tm else n_tiles - 1
            out_shape.append(jax.ShapeDtypeStruct((batch, 2, H_G, HEAD_DIM, win), F32))
            out_specs.append(pl.BlockSpec(
                (1, 2, H_G, HEAD_DIM, rows),
                lambda b, i, first=first: (b, 0, 0, 0, jnp.maximum(i - first, 0))))
        scratch = [pltpu.VMEM((D_QKV // LANES, tm, LANES), F32)] * 2 + [pltpu.VMEM((tm, LANES), F32)]
        body = functools.partial(_qkv_prompt_kernel, tm=tm, n_tiles=n_tiles)
    else:
        assert n_tiles == 1 and batch == 1 and tm == LANES
        out_shape = [jax.ShapeDtypeStruct((n_tok, D_QKV), F32)] + [jax.ShapeDtypeStruct((D_QKV, n_tok), F32)] * 2
        out_specs = [pl.BlockSpec((tm, D_QKV), tok)] + [pl.BlockSpec((D_QKV, tm), lambda b, i: (0, 0))] * 2
        scratch = []
        body = _qkv_sample_kernel

    return pl.pallas_call(
        body,
        grid=(batch, n_tiles),
        in_specs=in_specs,
        out_specs=out_specs,
        out_shape=out_shape,
        scratch_shapes=scratch,
        compiler_params=pltpu.CompilerParams(
            dimension_semantics=("arbitrary", "arbitrary"), vmem_limit_bytes=VMEM_LIMIT),
        name="qkv_prompt" if prompt else "qkv_sample",
    )(x2, g_mix[None, :], w_qkv, gq, gk, rot, bd, *perms)


def _attn_prompt_kernel(q_ref, kc_ref, kp_ref, vc_ref, vp_ref, o_ref, lse_ref, *, dil):
    has_prev = pl.program_id(1) > 0
    piece = TOKEN_TILE // dil

    def residue_rows(r):
        starts = [t * TOKEN_TILE + r * piece for t in range(SUB_BLOCK // piece)]
        if isinstance(r, int):
            return [slice(s0, s0 + piece) for s0 in starts]
        return [pl.ds(pl.multiple_of(s0, piece), piece) for s0 in starts]

    def load(ref, rows, sl):
        parts = [ref[rs, sl] for rs in rows]
        return parts[0] if len(parts) == 1 else jnp.concatenate(parts, axis=0)

    def store(ref, rows, sl, val):
        n = val.shape[0] // len(rows)
        for t, rs in enumerate(rows):
            ref[rs, sl] = val[t * n:(t + 1) * n]
    key = lax.broadcasted_iota(jnp.int32, (2 * SUB_BLOCK, HEADS_PER_VREG * SUB_BLOCK), 0)
    qry = lax.broadcasted_iota(jnp.int32, (2 * SUB_BLOCK, HEADS_PER_VREG * SUB_BLOCK), 1) % SUB_BLOCK
    vis_cur = key <= qry
    vis_prev = key - SUB_BLOCK >= qry
    lane16 = lax.broadcasted_iota(jnp.int32, (SUB_BLOCK, LANES), 1).astype(F32).astype(BF16)
    head_lanes = (lane16 < HEAD_DIM, lane16 >= HEAD_DIM)
    eye_r = lax.broadcasted_iota(jnp.int32, (LANES, LANES), 0)
    eye_c = lax.broadcasted_iota(jnp.int32, (LANES, LANES), 1)
    eye = (eye_r == eye_c).astype(F32).astype(BF16)
    ones_rows = jnp.ones((BF16_ROWS, 2 * SUB_BLOCK), BF16)
    head_row = lax.broadcasted_iota(jnp.int32, (H_G, SUB_BLOCK), 0)

    pad = jnp.zeros((LANES - H_G, SUB_BLOCK), F32)
    slabs = [slice(p * LANES, (p + 1) * LANES) for p in range(GROUP_SLABS)]

    if dil == 1:
        blocks = [([slice(0, SUB_BLOCK)], kp_ref, vp_ref, [slice(0, SUB_BLOCK)], has_prev)]
        blocks += [([slice(i * SUB_BLOCK, (i + 1) * SUB_BLOCK)], kc_ref, vc_ref,
                    [slice((i - 1) * SUB_BLOCK, i * SUB_BLOCK)], True) for i in range(1, ATTN_SUB_BLOCKS)]
    else:
        parts = dil // ATTN_SUB_BLOCKS
        first = 0 if parts == 1 else pl.program_id(2) * ATTN_SUB_BLOCKS
        blocks = [(residue_rows(first + i), kp_ref, vp_ref, residue_rows(first + i), has_prev)
                  for i in range(ATTN_SUB_BLOCKS)]

    scores, vts = [], []
    for rows, kp_src, vp_src, prev_rows, _ in blocks:
        for sl in slabs:
            q = load(q_ref, rows, sl)
            qq = jnp.concatenate([jnp.where(hl, q, jnp.zeros_like(q)) for hl in head_lanes], axis=0)
            kk = jnp.concatenate([load(kc_ref, rows, sl), load(kp_src, prev_rows, sl)], axis=0)
            vv = jnp.concatenate([load(vc_ref, rows, sl), load(vp_src, prev_rows, sl)], axis=0)
            scores.append(_dot_nt(kk, qq))
            vts.append(jnp.concatenate([_dot_nt(eye, vv).astype(BF16), ones_rows], axis=0))
    maxes, probs = [], []
    for bi, (_, _, _, _, has_prev) in enumerate(blocks):
        visible = jnp.logical_or(vis_cur, jnp.logical_and(vis_prev, has_prev))
        for s in scores[bi * GROUP_SLABS:(bi + 1) * GROUP_SLABS]:
            s = jnp.where(visible, s, NEG)
            m = jnp.max(s, axis=0, keepdims=True)
            maxes.append(m)
            probs.append(jnp.exp2(s - m).astype(BF16))
    outs = [_dot(vt, e) for vt, e in zip(vts, probs)]
    for bi, (rows, _, _, _, _) in enumerate(blocks):
        lse_rows = jnp.zeros((H_G, SUB_BLOCK), F32)
        for p, sl in enumerate(slabs):
            ot, m = outs[bi * GROUP_SLABS + p], maxes[bi * GROUP_SLABS + p]
            den = ot[LANES:LANES + 1, :]
            lse = m * LN2 + jnp.log(den)
            inv = 1.0 / den
            o_halves = []
            for half in range(HEADS_PER_VREG):
                qs = slice(half * SUB_BLOCK, (half + 1) * SUB_BLOCK)
                o_halves.append(ot[half * HEAD_DIM:(half + 1) * HEAD_DIM, qs] * inv[:, qs])
                lse_rows = jnp.where(head_row == p * HEADS_PER_VREG + half, lse[:, qs], lse_rows)
            store(o_ref, rows, sl, jnp.concatenate(o_halves, axis=0).T.astype(BF16))
        store(lse_ref, rows, slice(None), jnp.concatenate([lse_rows, pad], axis=0).T)


def _attn_prompt_call(q, k, v, gi, batch, seq):
    dil = GROUPS[gi][1]
    if dil == 1:
        rows, parts = ATTN_SUB_BLOCKS * SUB_BLOCK, 1
        prev_rows = SUB_BLOCK
        prev_map = lambda b, s, p: (jnp.maximum((b * steps + s) * ATTN_SUB_BLOCKS - 1, 0), 0)
    else:
        rows, parts = SUB_BLOCK * dil, dil // ATTN_SUB_BLOCKS
        assert dil % ATTN_SUB_BLOCKS == 0 and rows % TOKEN_TILE == 0
        assert TOKEN_TILE % (dil * BF16_ROWS) == 0
        prev_rows = rows
        prev_map = lambda b, s, p: (b * steps + jnp.maximum(s - 1, 0), 0)
    assert seq % rows == 0
    steps = seq // rows
    cur_map = lambda b, s, p: (b * steps + s, 0)
    cur = pl.BlockSpec((rows, D_GROUP), cur_map)
    prev = pl.BlockSpec((prev_rows, D_GROUP), prev_map)
    return pl.pallas_call(
        functools.partial(_attn_prompt_kernel, dil=dil),
        grid=(batch, steps, parts),
        in_specs=[cur, cur, prev, cur, prev],
        out_specs=[cur, pl.BlockSpec((rows, LANES), cur_map)],
        out_shape=[jax.ShapeDtypeStruct((batch * seq, D_GROUP), BF16),
                   jax.ShapeDtypeStruct((batch * seq, LANES), F32)],
        compiler_params=pltpu.CompilerParams(
            dimension_semantics=("arbitrary", "arbitrary", "arbitrary"), vmem_limit_bytes=VMEM_LIMIT),
        name=f"attn_prompt_g{gi}",
    )(q, k, k, v, v)


def _attn_sample_kernel(q_ref, knt_ref, vnt_ref, buf_ref, o_ref, lse_ref, nbuf_ref, *, dil, wb, t_new):
    assert dil & (dil - 1) == 0
    q_rows = q_ref.shape[1]
    width = wb + LANES
    shift = lax.rem(LANES - t_new * pl.program_id(0), LANES)
    knt = pltpu.roll(knt_ref[...], shift, 1)
    vnt = pltpu.roll(vnt_ref[...], shift, 1)
    q = q_ref[0]

    t_row = lax.broadcasted_iota(jnp.int32, (q_rows, width), 0)
    p_col = lax.broadcasted_iota(jnp.int32, (q_rows, width), 1)
    back = wb + t_row - p_col
    valid = jnp.logical_and(jnp.logical_and(back >= 0, (back & (dil - 1)) == 0),
                            jnp.logical_and(back <= (N_KEYS - 1) * dil, p_col < wb + t_new))
    lane = lax.broadcasted_iota(jnp.int32, (q_rows, LANES), 1)
    lse_tile = jnp.zeros((q_rows, LANES), F32)
    o_heads = []
    for h in range(H_G):
        rows = slice(h * HEAD_DIM, (h + 1) * HEAD_DIM)
        kx = jnp.concatenate([buf_ref[0, 0, h], knt[rows, :]], axis=1)
        vx = jnp.concatenate([buf_ref[0, 1, h], vnt[rows, :]], axis=1)
        nbuf_ref[0, 0, h] = pltpu.roll(kx, width - t_new, 1)[:, :wb]
        nbuf_ref[0, 1, h] = pltpu.roll(vx, width - t_new, 1)[:, :wb]
        s = jnp.where(valid, _dot(q[:, rows].astype(BF16), kx.astype(BF16)), NEG)
        m = jnp.max(s, axis=1, keepdims=True)
        e = jnp.exp2(s - m)
        den = jnp.sum(e, axis=1, keepdims=True)
        o_heads.append(_dot_nt(e.astype(BF16), vx.astype(BF16)) / den)
        lse_tile = jnp.where(lane == h, m * LN2 + jnp.log(den), lse_tile)
    o_ref[0] = jnp.concatenate(o_heads, axis=1)
    lse_ref[0] = lse_tile


def _attn_sample_call(q, knt, vnt, buf, gi, batch, t_new):
    dil = GROUPS[gi][1]
    wb = buf.shape[2]
    assert wb % LANES == 0 and LANES % t_new == 0 and batch * t_new <= LANES
    q_rows = 8
    q_pad = jnp.pad(q.reshape(batch, t_new, D_QKV), ((0, 0), (0, q_rows - t_new), (0, 0)))
    buf_t = jnp.transpose(buf, (0, 1, 3, 4, 2))
    new_spec = pl.BlockSpec((D_GROUP, LANES), lambda b: (gi, 0))
    buf_spec = pl.BlockSpec((1, 2, H_G, HEAD_DIM, wb), lambda b: (b, 0, 0, 0, 0))
    o, lse, nbuf = pl.pallas_call(
        functools.partial(_attn_sample_kernel, dil=dil, wb=wb, t_new=t_new),
        grid=(batch,),
        in_specs=[pl.BlockSpec((1, q_rows, D_GROUP), lambda b: (b, 0, gi)), new_spec, new_spec, buf_spec],
        out_specs=[pl.BlockSpec((1, q_rows, D_GROUP), lambda b: (b, 0, 0)),
                   pl.BlockSpec((1, q_rows, LANES), lambda b: (b, 0, 0)),
                   buf_spec],
        out_shape=[jax.ShapeDtypeStruct((batch, q_rows, D_GROUP), F32),
                   jax.ShapeDtypeStruct((batch, q_rows, LANES), F32),
                   jax.ShapeDtypeStruct((batch, 2, H_G, HEAD_DIM, wb), F32)],
        compiler_params=pltpu.CompilerParams(
            dimension_semantics=("arbitrary",), vmem_limit_bytes=VMEM_LIMIT),
        name=f"attn_sample_g{gi}",
    )(q_pad, knt, vnt, buf_t)
    return (o[:, :t_new].reshape(batch * t_new, D_GROUP), lse[:, :t_new].reshape(batch * t_new, LANES),
            jnp.transpose(nbuf, (0, 1, 4, 2, 3)))


def _merge_kernel(x_ref, o0_ref, o1_ref, o2_ref, l0_ref, l1_ref, l2_ref, pe_ref,
                  gmix_ref, wuv_ref, wg_ref, gv_ref, ws_ref, bias_ref, woa_ref, wob_ref, wo_ref,
                  gffn_ref, wff1_ref, wff2_ref, gple_ref, wpg_ref, wpp_ref,
                  y_ref, va_ref, *scratch, prompt, tm, n_tiles):
    d_model = x_ref.shape[1]
    d_a = gv_ref.shape[1]
    c_g = d_a // G_A
    x = x_ref[...]
    h = _rms(x, gmix_ref[...]).astype(BF16)

    u = jax.nn.gelu(_dot(h, wuv_ref[:, 0:d_a]))
    va = _rms(jax.nn.gelu(_dot(h, wuv_ref[:, d_a:2 * d_a])), gv_ref[...])
    if prompt:
        @pl.when(pl.program_id(1) == n_tiles - 1)
        def _():
            va_ref[0] = va[tm - CHUNK:tm, :]
    else:
        va_ref[...] = va
    va16 = va.astype(BF16)
    r_i = lax.broadcasted_iota(jnp.int32, (CHUNK, CHUNK), 0)
    c_i = lax.broadcasted_iota(jnp.int32, (CHUNK, CHUNK), 1)
    w_tril = [jnp.where(r_i >= c_i, ws_ref[g], jnp.zeros((CHUNK, CHUNK), BF16)) for g in range(G_A)]
    mix_rows = []
    for ci in range(tm // CHUNK):
        rs = slice(ci * CHUNK, (ci + 1) * CHUNK)
        cols = [_dot(w_tril[g], va16[rs, g * c_g:(g + 1) * c_g]) for g in range(G_A)]
        mix_rows.append(jnp.concatenate(cols, axis=1) + bias_ref[...])
    a = u * jnp.concatenate(mix_rows, axis=0)

    o_refs, l_refs = (o0_ref, o1_ref, o2_ref), (l0_ref, l1_ref, l2_ref)
    o_slabs, lses = [], []
    for gi, (_, dil) in enumerate(GROUPS):
        if not prompt or dil == 1:
            o_slabs.append([o_refs[gi][:, cc * LANES:(cc + 1) * LANES].astype(F32)
                            for cc in range(GROUP_SLABS)])
            lses.append(l_refs[gi][...])
            continue
        o_sc, l_sc = scratch[2 * (gi - 1)], scratch[2 * (gi - 1) + 1]
        piece = tm // dil
        for r in range(dil):
            rows = slice(r * piece, (r + 1) * piece)
            for cc in range(GROUP_SLABS):
                o_sc[cc, pl.ds(r, piece, stride=dil), :] = (
                    o_refs[gi][rows, cc * LANES:(cc + 1) * LANES].astype(F32))
            l_sc[pl.ds(r, piece, stride=dil), :] = l_refs[gi][rows, :]
        o_slabs.append([o_sc[cc] for cc in range(GROUP_SLABS)])
        lses.append(l_sc[...])

    m = jnp.maximum(jnp.maximum(lses[0], lses[1]), lses[2])
    ws = [jnp.exp(l - m) for l in lses]
    inv = 1.0 / (ws[0] + ws[1] + ws[2])
    ws = [w * inv for w in ws]
    lane = lax.broadcasted_iota(jnp.int32, (tm, LANES), 1)
    first_head = lane < HEAD_DIM
    b_cols = []
    for p in range(GROUP_SLABS):
        acc = jnp.zeros((tm, LANES), F32)
        for w, slabs in zip(ws, o_slabs):
            w_pair = jnp.where(first_head, w[:, 2 * p:2 * p + 1], w[:, 2 * p + 1:2 * p + 2])
            acc = acc + w_pair * slabs[p]
        b_cols.append(acc)
    b = jnp.concatenate(b_cols, axis=1)

    gate_a = jax.nn.sigmoid(_dot(h, wg_ref[:, 0:d_model]))
    gate_b = jax.nn.sigmoid(_dot(h, wg_ref[:, d_model:2 * d_model]))
    merged = gate_a * _dot(a.astype(BF16), woa_ref[...]) + gate_b * _dot(b.astype(BF16), wob_ref[...])
    x = x + _dot(merged.astype(BF16), wo_ref[...])

    h2 = _rms(x, gffn_ref[...]).astype(BF16)
    d_ff = wff1_ref.shape[1]
    ff_step = 1024
    ffn = jnp.zeros((tm, d_model), F32)
    for c0 in range(0, d_ff, ff_step):
        hid = jnp.square(jnp.maximum(_dot(h2, wff1_ref[:, c0:c0 + ff_step]), 0.0))
        ffn = ffn + _dot(hid.astype(BF16), wff2_ref[c0:c0 + ff_step, :])
    x = x + ffn

    gate = jax.nn.sigmoid(_dot(_rms(x, gple_ref[...]).astype(BF16), wpg_ref[...]))
    y_ref[...] = x + gate * _dot(pe_ref[...].astype(BF16), wpp_ref[...])


def _merge_call(x2, outs, lses, pe2, wts, ws_mat, bias, *, prompt, batch, tm):
    n_tok, d_model = x2.shape
    seq = n_tok // batch
    n_tiles = seq // tm
    d_a = wts["g_v"].shape[1]
    tile_index = lambda b, i: b * n_tiles + i
    tok = lambda b, i: (tile_index(b, i), 0)
    row_spec = lambda width: pl.BlockSpec((tm, width), tok)

    names = ("g_mix", "w_uv", "w_gates", "g_v", "ws", "bias", "w_oa", "w_ob", "w_o",
             "g_ffn", "w_ff1", "w_ff2", "g_ple", "w_ple_gate", "w_ple_proj")
    consts = dict(wts, ws=ws_mat, bias=bias)
    in_specs = ([row_spec(d_model)]
                + [row_spec(D_GROUP)] * N_GROUPS + [row_spec(LANES)] * N_GROUPS
                + [row_spec(pe2.shape[1])] + [_resident(consts[k].shape) for k in names])
    scratch = []
    if prompt:
        va_shape = jax.ShapeDtypeStruct((batch, CHUNK, d_a), F32)
        va_spec = pl.BlockSpec((1, CHUNK, d_a), lambda b, i: (b, 0, 0))
        for _, dil in GROUPS:
            if dil > 1:
                scratch += [pltpu.VMEM((GROUP_SLABS, tm, LANES), F32), pltpu.VMEM((tm, LANES), F32)]
    else:
        va_shape = jax.ShapeDtypeStruct((n_tok, d_a), F32)
        va_spec = row_spec(d_a)
    return pl.pallas_call(
        functools.partial(_merge_kernel, prompt=prompt, tm=tm, n_tiles=n_tiles),
        grid=(batch, n_tiles),
        in_specs=in_specs,
        out_specs=[row_spec(d_model), va_spec],
        out_shape=[jax.ShapeDtypeStruct((n_tok, d_model), F32), va_shape],
        scratch_shapes=scratch,
        compiler_params=pltpu.CompilerParams(
            dimension_semantics=("arbitrary", "arbitrary"), vmem_limit_bytes=VMEM_LIMIT),
        name="merge_prompt" if prompt else "merge_sample",
    )(x2, *outs, *lses, pe2, *[consts[k] for k in names])


def kernel(x_prompt, x_sample, p_prompt, p_sample, cache_kv_w128, cache_kv_w512, cache_kv_w2048,
           g_mix, w_in, g_v, w_s, b_s, g_q, g_k, w_oa, w_ob, w_o,
           g_ffn, w_ff1, w_ff2, g_ple, w_ple_gate, w_ple_proj):
    depth = w_in.shape[0]
    batch, seq, d_model = x_prompt.shape
    dec_batch, t_new, _ = x_sample.shape
    d_a = g_v.shape[1]
    n_sample = dec_batch * t_new
    assert n_sample == CHUNK and w_in.shape[2] == 2 * d_a + 3 * D_QKV + 2 * d_model
    caches = (cache_kv_w128, cache_kv_w512, cache_kv_w2048)

    xp = x_prompt.reshape(batch * seq, d_model)
    xs = x_sample.reshape(n_sample, d_model)
    pos_p = jnp.arange(seq, dtype=jnp.int32)
    pos_s = PAST_LEN + jnp.tile(jnp.arange(t_new, dtype=jnp.int32), dec_batch)

    kv_p = [[] for _ in GROUPS]
    kv_s = [[] for _ in GROUPS]
    cv_p, cv_s = [], []
    for l in range(depth):
        w_l = w_in[l].astype(BF16)
        wts = {
            "g_mix": g_mix[l][None, :], "w_uv": w_l[:, 0:2 * d_a],
            "w_gates": w_l[:, 2 * d_a + 3 * D_QKV:], "g_v": g_v[l][None, :],
            "w_oa": w_oa[l].astype(BF16), "w_ob": w_ob[l].astype(BF16), "w_o": w_o[l].astype(BF16),
            "g_ffn": g_ffn[l][None, :], "w_ff1": w_ff1[l].astype(BF16), "w_ff2": w_ff2[l].astype(BF16),
            "g_ple": g_ple[l][None, :], "w_ple_gate": w_ple_gate[l].astype(BF16),
            "w_ple_proj": w_ple_proj[l].astype(BF16),
        }
        w_qkv = w_l[:, 2 * d_a:2 * d_a + 3 * D_QKV]
        c_g = d_a // G_A
        ws_p = w_s[l].astype(BF16)
        bias_p = jnp.repeat(b_s[l].T, c_g, axis=1)
        eye = jnp.eye(CHUNK // t_new, dtype=F32)
        ws_s = jnp.einsum("ab,gts->gatbs", eye, w_s[l][:, :t_new, :t_new]).reshape(G_A, CHUNK, CHUNK)
        ws_s = ws_s.astype(BF16)
        bias_s = jnp.tile(jnp.repeat(b_s[l][:, :t_new].T, c_g, axis=1), (CHUNK // t_new, 1))

        tm = TOKEN_TILE
        res = _qkv_call(xp, pos_p, g_mix[l], w_qkv, g_q[l], g_k[l], prompt=True, batch=batch, tm=tm)
        q_g, k_g, v_g, kv_tails = res[0:3], res[3:6], res[6:9], res[9:12]
        outs, lses = zip(*[_attn_prompt_call(q_g[gi], k_g[gi], v_g[gi], gi, batch, seq)
                           for gi in range(N_GROUPS)])
        xp, va_p = _merge_call(xp, outs, lses, p_prompt[l].reshape(batch * seq, -1), wts, ws_p, bias_p,
                               prompt=True, batch=batch, tm=tm)
        for gi, st in enumerate(kv_tails):
            kv_p[gi].append(jnp.transpose(st, (0, 1, 4, 2, 3)))
        cv_p.append(va_p)

        qs, knt, vnt = _qkv_call(
            xs, pos_s, g_mix[l], w_qkv, g_q[l], g_k[l], prompt=False, batch=1, tm=n_sample)
        outs, lses = [], []
        for gi in range(N_GROUPS):
            o, lse, nbuf = _attn_sample_call(qs, knt, vnt, caches[gi][l], gi, dec_batch, t_new)
            outs.append(o)
            lses.append(lse)
            kv_s[gi].append(nbuf)
        xs, va_s = _merge_call(xs, outs, lses, p_sample[l].reshape(n_sample, -1), wts, ws_s, bias_s,
                               prompt=False, batch=1, tm=n_sample)
        cv_s.append(va_s.reshape(dec_batch, t_new, d_a))

    return (xp.reshape(batch, seq, d_model), xs.reshape(dec_batch, t_new, d_model),
            jnp.stack(kv_p[0]), jnp.stack(kv_p[1]), jnp.stack(kv_p[2]),
            jnp.stack(kv_s[0]), jnp.stack(kv_s[1]), jnp.stack(kv_s[2]),
            jnp.stack(cv_p), jnp.stack(cv_s))
```

```python
import functools
import math

import jax
import jax.numpy as jnp
import numpy as np
from jax import lax
from jax.experimental import pallas as pl
from jax.experimental.pallas import tpu as pltpu

F32 = jnp.float32
BF16 = jnp.bfloat16

LANES = 128
BF16_ROWS = 16
MXU_DIM = 256
TOKEN_TILE = 256
HEAD_DIM = 64
H_G = 8
HEADS_PER_VREG = LANES // HEAD_DIM
GROUPS = ((128, 1), (512, 4), (2048, 16))
N_GROUPS = len(GROUPS)
D_GROUP = H_G * HEAD_DIM
GROUP_SLABS = D_GROUP // LANES
D_QKV = N_GROUPS * D_GROUP
N_KEYS = 129
SUB_BLOCK = 128
ROT_DIM = HEAD_DIM // 4
ROT_HALF = ROT_DIM // 2
ROPE_THETA = 500000.0
CHUNK = 128
G_A = 4
PAST_LEN = 8192
ATTN_SUB_BLOCKS = 4
EPS = 1e-6
NEG = -1e30
LN2 = math.log(2.0)
LOG2E = 1.0 / LN2
VMEM_LIMIT = 56 * 1024 * 1024


def _rms(x, g):
    return x * lax.rsqrt(jnp.mean(x * x, axis=-1, keepdims=True) + EPS) * g


def _resident(shape):
    zeros = (0,) * len(shape)
    return pl.BlockSpec(shape, lambda *_: zeros, pipeline_mode=pl.Buffered(1))


def _dot(a, b):
    return jnp.dot(a, b, preferred_element_type=F32)


def _dot_nt(a, b):
    return lax.dot_general(a, b, (((1,), (1,)), ((), ())), preferred_element_type=F32)


def _project_qkv(x_ref, gmix_ref, w_ref, gq_ref, gk_ref, rot_ref, bd_ref, perm_refs, emit):
    h = _rms(x_ref[...], gmix_ref[...]).astype(BF16)
    hs = [h if p_ref is None else _dot(p_ref[...], h).astype(BF16) for p_ref in perm_refs]
    bd = bd_ref[...]
    wide = bd.shape[0]

    for ti, g_ref in enumerate((gq_ref, gk_ref, None)):
        for gi in range(N_GROUPS):
            c0 = ti * D_QKV + gi * D_GROUP
            z = _dot(hs[gi], w_ref[:, c0:c0 + D_GROUP])
            if g_ref is None:
                emit(ti, gi, [z[:, c * LANES:(c + 1) * LANES] for c in range(GROUP_SLABS)])
                continue
            t0 = 3 * LANES * min(gi, rot_ref.shape[1] // (3 * LANES) - 1)
            rc, ru, rd = (rot_ref[:, t0 + k * LANES:t0 + (k + 1) * LANES] for k in range(3))

            def rot(zn):
                return zn * rc + pltpu.roll(zn, LANES - ROT_HALF, 1) * ru + pltpu.roll(zn, ROT_HALF, 1) * rd

            vals = []
            for c in range(0, D_GROUP, wide):
                zc = z[:, c:c + wide]
                ss = _dot((zc * zc).astype(BF16), bd)
                zn = zc * lax.rsqrt(ss * (1.0 / HEAD_DIM) + EPS)
                vals += [rot(zn[:, j:j + LANES] * g_ref[...]) for j in range(0, wide, LANES)]
            emit(ti, gi, vals)


def _qkv_prompt_kernel(*refs, tm, n_tiles):
    n_perm = sum(dil > 1 for _, dil in GROUPS)
    ins, perms, rest = refs[:7], refs[7:7 + n_perm], refs[7 + n_perm:]
    outs, kv_refs = rest[:3 * N_GROUPS], rest[3 * N_GROUPS:4 * N_GROUPS]
    kv_slabs, order_slab = rest[4 * N_GROUPS:4 * N_GROUPS + 2], rest[4 * N_GROUPS + 2]
    perms = list(perms)
    perm_refs = [perms.pop(0) if dil > 1 else None for _, dil in GROUPS]

    def emit(ti, gi, vals):
        out = outs[ti * N_GROUPS + gi]
        for cc, val in enumerate(vals):
            out[:, cc * LANES:(cc + 1) * LANES] = val.astype(BF16)
            if ti > 0:
                kv_slabs[ti - 1][gi * GROUP_SLABS + cc] = val

    _project_qkv(*ins, perm_refs, emit)

    i = pl.program_id(1)
    seq = n_tiles * tm
    for gi, (win, dil) in enumerate(GROUPS):
        rows = min(win, tm)
        first = (seq - win) // tm if win >= tm else n_tiles - 1
        piece = tm // dil

        @pl.when(i >= first)
        def _(gi=gi, dil=dil, rows=rows, piece=piece):
            for kv in range(2):
                for cc in range(GROUP_SLABS):
                    c = gi * GROUP_SLABS + cc
                    if dil > 1:
                        for r in range(dil):
                            order_slab[pl.ds(r, piece, stride=dil), :] = kv_slabs[kv][c, r * piece:(r + 1) * piece, :]
                        val = order_slab[...]
                    else:
                        val = kv_slabs[kv][c]
                    t = val[tm - rows:tm].T
                    for hh in range(HEADS_PER_VREG):
                        kv_refs[gi][0, kv, cc * HEADS_PER_VREG + hh] = t[hh * HEAD_DIM:(hh + 1) * HEAD_DIM, :]


def _qkv_sample_kernel(*refs):
    ins, outs = refs[:7], refs[7:10]

    def emit(ti, gi, vals):
        for cc, val in enumerate(vals):
            c = gi * GROUP_SLABS + cc
            if ti == 0:
                outs[ti][:, c * LANES:(c + 1) * LANES] = val
            else:
                outs[ti][c * LANES:(c + 1) * LANES, :] = val.T

    _project_qkv(*ins, [None] * N_GROUPS, emit)


def _residue_major_source(dil, tm):
    row = np.arange(tm, dtype=np.int32)
    piece = tm // dil
    return (row % piece) * dil + row // piece


def _rotary_tables(pos):
    inv = np.float32(ROPE_THETA) ** (-np.arange(ROT_HALF, dtype=np.float32) * np.float32(2.0 / ROT_DIM))
    ang = pos.astype(np.float32)[:, None] * inv[None, :]
    cos, sin = np.cos(ang), np.sin(ang)
    n = pos.shape[0]
    pad = np.zeros((n, HEAD_DIM - ROT_DIM), np.float32)
    zero = np.zeros((n, ROT_HALF), np.float32)
    rc = np.concatenate([cos, cos, pad + 1.0], axis=1)
    ru = np.concatenate([-sin, zero, pad], axis=1)
    rd = np.concatenate([zero, sin, pad], axis=1)
    return np.concatenate([np.tile(t, (1, HEADS_PER_VREG)) for t in (rc, ru, rd)], axis=1).astype(np.float32)


def _qkv_call(x2, pos, g_mix, w_qkv, g_q, g_k, *, prompt, batch, tm):
    n_tok, d_model = x2.shape
    seq = n_tok // batch
    n_tiles = seq // tm
    perms = []
    if prompt:
        assert all(tm % (dil * BF16_ROWS) == 0 for _, dil in GROUPS)
        tile_start = (np.arange(seq, dtype=np.int32) // tm) * tm
        tables = []
        for _, dil in GROUPS:
            src = _residue_major_source(dil, tm)
            tables.append(_rotary_tables(pos[tile_start + np.tile(src, n_tiles)]))
            if dil > 1:
                perms.append(jnp.asarray(src[:, None] == np.arange(tm)[None, :], dtype=BF16))
        rot = jnp.asarray(np.concatenate(tables, axis=1))
    else:
        rot = jnp.asarray(_rotary_tables(pos))
    gq = jnp.tile(g_q.astype(F32), HEADS_PER_VREG)[None, :] * (LOG2E * HEAD_DIM ** -0.5)
    gk = jnp.tile(g_k.astype(F32), HEADS_PER_VREG)[None, :]
    lane = np.arange(MXU_DIM)
    bd = jnp.asarray(lane[:, None] // HEAD_DIM == lane[None, :] // HEAD_DIM, dtype=BF16)

    tile_index = lambda b, i: b * n_tiles + i
    tok = lambda b, i: (tile_index(b, i), 0)
    in_specs = [
        pl.BlockSpec((tm, d_model), tok),
        _resident((1, d_model)),
        _resident((d_model, 3 * D_QKV)),
        _resident((1, LANES)),
        _resident((1, LANES)),
        pl.BlockSpec((tm, rot.shape[1]), lambda b, i: (i, 0)),
        _resident((MXU_DIM, MXU_DIM)),
    ] + [_resident(p.shape) for p in perms]
    if prompt:
        out_shape = [jax.ShapeDtypeStruct((n_tok, D_GROUP), BF16)] * (3 * N_GROUPS)
        out_specs = [pl.BlockSpec((tm, D_GROUP), tok)] * (3 * N_GROUPS)
        for win, _ in GROUPS:
            rows = min(win, tm)
            first = (seq - win) // tm if win >= tm else n_tiles - 1
            out_shape.append(jax.ShapeDtypeStruct((batch, 2, H_G, HEAD_DIM, win), F32))
            out_specs.append(pl.BlockSpec(
                (1, 2, H_G, HEAD_DIM, rows),
                lambda b, i, first=first: (b, 0, 0, 0, jnp.maximum(i - first, 0))))
        scratch = [pltpu.VMEM((D_QKV // LANES, tm, LANES), F32)] * 2 + [pltpu.VMEM((tm, LANES), F32)]
        body = functools.partial(_qkv_prompt_kernel, tm=tm, n_tiles=n_tiles)
    else:
        assert n_tiles == 1 and batch == 1 and tm == LANES
        out_shape = [jax.ShapeDtypeStruct((n_tok, D_QKV), F32)] + [jax.ShapeDtypeStruct((D_QKV, n_tok), F32)] * 2
        out_specs = [pl.BlockSpec((tm, D_QKV), tok)] + [pl.BlockSpec((D_QKV, tm), lambda b, i: (0, 0))] * 2
        scratch = []
        body = _qkv_sample_kernel

    return pl.pallas_call(
        body,
        grid=(batch, n_tiles),
        in_specs=in_specs,
        out_specs=out_specs,
        out_shape=out_shape,
        scratch_shapes=scratch,
        compiler_params=pltpu.CompilerParams(
            dimension_semantics=("arbitrary", "arbitrary"), vmem_limit_bytes=VMEM_LIMIT),
        name="qkv_prompt" if prompt else "qkv_sample",
    )(x2, g_mix[None, :], w_qkv, gq, gk, rot, bd, *perms)


def _attn_prompt_kernel(q_ref, kc_ref, kp_ref, vc_ref, vp_ref, o_ref, lse_ref, *, dil):
    has_prev = pl.program_id(1) > 0
    piece = TOKEN_TILE // dil

    def residue_rows(i):
        return [(t, 0, slice(i * piece, (i + 1) * piece)) for t in range(SUB_BLOCK // piece)]

    def load(ref, rows, sl):
        parts = [ref[rs + (sl,)] for rs in rows]
        return parts[0] if len(parts) == 1 else jnp.concatenate(parts, axis=0)

    def store(ref, rows, sl, val):
        n = val.shape[0] // len(rows)
        for t, rs in enumerate(rows):
            ref[rs + (sl,)] = val[t * n:(t + 1) * n]
    key = lax.broadcasted_iota(jnp.int32, (2 * SUB_BLOCK, HEADS_PER_VREG * SUB_BLOCK), 0)
    qry = lax.broadcasted_iota(jnp.int32, (2 * SUB_BLOCK, HEADS_PER_VREG * SUB_BLOCK), 1) % SUB_BLOCK
    vis_cur = key <= qry
    vis_prev = key - SUB_BLOCK >= qry
    lane16 = lax.broadcasted_iota(jnp.int32, (SUB_BLOCK, LANES), 1).astype(F32).astype(BF16)
    head_lanes = (lane16 < HEAD_DIM, lane16 >= HEAD_DIM)
    eye_r = lax.broadcasted_iota(jnp.int32, (LANES, LANES), 0)
    eye_c = lax.broadcasted_iota(jnp.int32, (LANES, LANES), 1)
    eye = (eye_r == eye_c).astype(F32).astype(BF16)
    ones_rows = jnp.ones((BF16_ROWS, 2 * SUB_BLOCK), BF16)
    head_row = lax.broadcasted_iota(jnp.int32, (H_G, SUB_BLOCK), 0)

    pad = jnp.zeros((LANES - H_G, SUB_BLOCK), F32)
    slabs = [slice(p * LANES, (p + 1) * LANES) for p in range(GROUP_SLABS)]

    if dil == 1:
        blocks = [([(slice(0, SUB_BLOCK),)], kp_ref, vp_ref, [(slice(0, SUB_BLOCK),)], has_prev)]
        blocks += [([(slice(i * SUB_BLOCK, (i + 1) * SUB_BLOCK),)], kc_ref, vc_ref,
                    [(slice((i - 1) * SUB_BLOCK, i * SUB_BLOCK),)], True) for i in range(1, ATTN_SUB_BLOCKS)]
    else:
        blocks = [(residue_rows(i), kp_ref, vp_ref, residue_rows(i), has_prev)
                  for i in range(ATTN_SUB_BLOCKS)]

    scores, vts = [], []
    for rows, kp_src, vp_src, prev_rows, _ in blocks:
        for sl in slabs:
            q = load(q_ref, rows, sl)
            qq = jnp.concatenate([jnp.where(hl, q, jnp.zeros_like(q)) for hl in head_lanes], axis=0)
            kk = jnp.concatenate([load(kc_ref, rows, sl), load(kp_src, prev_rows, sl)], axis=0)
            vv = jnp.concatenate([load(vc_ref, rows, sl), load(vp_src, prev_rows, sl)], axis=0)
            scores.append(_dot_nt(kk, qq))
            vts.append(jnp.concatenate([_dot_nt(eye, vv).astype(BF16), ones_rows], axis=0))
    maxes, probs = [], []
    for bi, (_, _, _, _, has_prev) in enumerate(blocks):
        visible = jnp.logical_or(vis_cur, jnp.logical_and(vis_prev, has_prev))
        for s in scores[bi * GROUP_SLABS:(bi + 1) * GROUP_SLABS]:
            s = jnp.where(visible, s, NEG)
            m = jnp.max(s, axis=0, keepdims=True)
            maxes.append(m)
            probs.append(jnp.exp2(s - m).astype(BF16))
    outs = [_dot(vt, e) for vt, e in zip(vts, probs)]
    for bi, (rows, _, _, _, _) in enumerate(blocks):
        lse_rows = jnp.zeros((H_G, SUB_BLOCK), F32)
        for p, sl in enumerate(slabs):
            ot, m = outs[bi * GROUP_SLABS + p], maxes[bi * GROUP_SLABS + p]
            den = ot[LANES:LANES + 1, :]
            lse = m * LN2 + jnp.log(den)
            inv = 1.0 / den
            o_halves = []
            for half in range(HEADS_PER_VREG):
                qs = slice(half * SUB_BLOCK, (half + 1) * SUB_BLOCK)
                o_halves.append(ot[half * HEAD_DIM:(half + 1) * HEAD_DIM, qs] * inv[:, qs])
                lse_rows = jnp.where(head_row == p * HEADS_PER_VREG + half, lse[:, qs], lse_rows)
            store(o_ref, rows, sl, jnp.concatenate(o_halves, axis=0).T.astype(BF16))
        store(lse_ref, rows, slice(None), jnp.concatenate([lse_rows, pad], axis=0).T)


def _attn_prompt_call(q, k, v, gi, batch, seq):
    dil = GROUPS[gi][1]
    n_tok = batch * seq
    if dil == 1:
        rows, parts = ATTN_SUB_BLOCKS * SUB_BLOCK, 1
        assert seq % rows == 0
        steps = seq // rows
        view = lambda width: (n_tok, width)
        cur_map = lambda b, s, p: (b * steps + s, 0)
        cur = lambda width: pl.BlockSpec((rows, width), cur_map)
        prev = pl.BlockSpec((SUB_BLOCK, D_GROUP),
                            lambda b, s, p: (jnp.maximum((b * steps + s) * ATTN_SUB_BLOCKS - 1, 0), 0))
    else:
        span, parts = SUB_BLOCK * dil, dil // ATTN_SUB_BLOCKS
        assert dil % ATTN_SUB_BLOCKS == 0 and span % TOKEN_TILE == 0 and seq % span == 0
        assert TOKEN_TILE % (dil * BF16_ROWS) == 0
        steps, span_tiles, part_rows = seq // span, span // TOKEN_TILE, TOKEN_TILE // parts
        view = lambda width: (n_tok // TOKEN_TILE, parts, part_rows, width)
        cur_map = lambda b, s, p: (b * steps + s, p, 0, 0)
        cur = lambda width: pl.BlockSpec((span_tiles, 1, part_rows, width), cur_map)
        prev = pl.BlockSpec((span_tiles, 1, part_rows, D_GROUP),
                            lambda b, s, p: (b * steps + jnp.maximum(s - 1, 0), p, 0, 0))
    q, k, v = (t.reshape(view(D_GROUP)) for t in (q, k, v))
    o, lse = pl.pallas_call(
        functools.partial(_attn_prompt_kernel, dil=dil),
        grid=(batch, steps, parts),
        in_specs=[cur(D_GROUP), cur(D_GROUP), prev, cur(D_GROUP), prev],
        out_specs=[cur(D_GROUP), cur(LANES)],
        out_shape=[jax.ShapeDtypeStruct(view(D_GROUP), BF16), jax.ShapeDtypeStruct(view(LANES), F32)],
        compiler_params=pltpu.CompilerParams(
            dimension_semantics=("arbitrary", "arbitrary", "arbitrary"), vmem_limit_bytes=VMEM_LIMIT),
        name=f"attn_prompt_g{gi}",
    )(q, k, k, v, v)
    return o.reshape(n_tok, D_GROUP), lse.reshape(n_tok, LANES)


def _attn_sample_kernel(q_ref, knt_ref, vnt_ref, buf_ref, o_ref, lse_ref, nbuf_ref, *, dil, wb, t_new):
    assert dil & (dil - 1) == 0
    q_rows = q_ref.shape[1]
    width = wb + LANES
    shift = lax.rem(LANES - t_new * pl.program_id(0), LANES)
    knt = pltpu.roll(knt_ref[...], shift, 1)
    vnt = pltpu.roll(vnt_ref[...], shift, 1)
    q = q_ref[0]

    t_row = lax.broadcasted_iota(jnp.int32, (q_rows, width), 0)
    p_col = lax.broadcasted_iota(jnp.int32, (q_rows, width), 1)
    back = wb + t_row - p_col
    valid = jnp.logical_and(jnp.logical_and(back >= 0, (back & (dil - 1)) == 0),
                            jnp.logical_and(back <= (N_KEYS - 1) * dil, p_col < wb + t_new))
    lane = lax.broadcasted_iota(jnp.int32, (q_rows, LANES), 1)
    lse_tile = jnp.zeros((q_rows, LANES), F32)
    o_heads = []
    for h in range(H_G):
        rows = slice(h * HEAD_DIM, (h + 1) * HEAD_DIM)
        kx = jnp.concatenate([buf_ref[0, 0, h], knt[rows, :]], axis=1)
        vx = jnp.concatenate([buf_ref[0, 1, h], vnt[rows, :]], axis=1)
        nbuf_ref[0, 0, h] = pltpu.roll(kx, width - t_new, 1)[:, :wb]
        nbuf_ref[0, 1, h] = pltpu.roll(vx, width - t_new, 1)[:, :wb]
        s = jnp.where(valid, _dot(q[:, rows].astype(BF16), kx.astype(BF16)), NEG)
        m = jnp.max(s, axis=1, keepdims=True)
        e = jnp.exp2(s - m)
        den = jnp.sum(e, axis=1, keepdims=True)
        o_heads.append(_dot_nt(e.astype(BF16), vx.astype(BF16)) / den)
        lse_tile = jnp.where(lane == h, m * LN2 + jnp.log(den), lse_tile)
    o_ref[0] = jnp.concatenate(o_heads, axis=1)
    lse_ref[0] = lse_tile


def _attn_sample_call(q, knt, vnt, buf, gi, batch, t_new):
    dil = GROUPS[gi][1]
    wb = buf.shape[2]
    assert wb % LANES == 0 and LANES % t_new == 0 and batch * t_new <= LANES
    q_rows = 8
    q_pad = jnp.pad(q.reshape(batch, t_new, D_QKV), ((0, 0), (0, q_rows - t_new), (0, 0)))
    buf_t = jnp.transpose(buf, (0, 1, 3, 4, 2))
    new_spec = pl.BlockSpec((D_GROUP, LANES), lambda b: (gi, 0))
    buf_spec = pl.BlockSpec((1, 2, H_G, HEAD_DIM, wb), lambda b: (b, 0, 0, 0, 0))
    o, lse, nbuf = pl.pallas_call(
        functools.partial(_attn_sample_kernel, dil=dil, wb=wb, t_new=t_new),
        grid=(batch,),
        in_specs=[pl.BlockSpec((1, q_rows, D_GROUP), lambda b: (b, 0, gi)), new_spec, new_spec, buf_spec],
        out_specs=[pl.BlockSpec((1, q_rows, D_GROUP), lambda b: (b, 0, 0)),
                   pl.BlockSpec((1, q_rows, LANES), lambda b: (b, 0, 0)),
                   buf_spec],
        out_shape=[jax.ShapeDtypeStruct((batch, q_rows, D_GROUP), F32),
                   jax.ShapeDtypeStruct((batch, q_rows, LANES), F32),
                   jax.ShapeDtypeStruct((batch, 2, H_G, HEAD_DIM, wb), F32)],
        compiler_params=pltpu.CompilerParams(
            dimension_semantics=("arbitrary",), vmem_limit_bytes=VMEM_LIMIT),
        name=f"attn_sample_g{gi}",
    )(q_pad, knt, vnt, buf_t)
    return (o[:, :t_new].reshape(batch * t_new, D_GROUP), lse[:, :t_new].reshape(batch * t_new, LANES),
            jnp.transpose(nbuf, (0, 1, 4, 2, 3)))


def _merge_kernel(x_ref, o0_ref, o1_ref, o2_ref, l0_ref, l1_ref, l2_ref, pe_ref,
                  gmix_ref, wuv_ref, wg_ref, gv_ref, ws_ref, bias_ref, woa_ref, wob_ref, wo_ref,
                  gffn_ref, wff1_ref, wff2_ref, gple_ref, wpg_ref, wpp_ref,
                  y_ref, va_ref, *scratch, prompt, tm, n_tiles):
    d_model = x_ref.shape[1]
    d_a = gv_ref.shape[1]
    c_g = d_a // G_A
    x = x_ref[...]
    h = _rms(x, gmix_ref[...]).astype(BF16)

    u = jax.nn.gelu(_dot(h, wuv_ref[:, 0:d_a]))
    va = _rms(jax.nn.gelu(_dot(h, wuv_ref[:, d_a:2 * d_a])), gv_ref[...])
    if prompt:
        @pl.when(pl.program_id(1) == n_tiles - 1)
        def _():
            va_ref[0] = va[tm - CHUNK:tm, :]
    else:
        va_ref[...] = va
    va16 = va.astype(BF16)
    r_i = lax.broadcasted_iota(jnp.int32, (CHUNK, CHUNK), 0)
    c_i = lax.broadcasted_iota(jnp.int32, (CHUNK, CHUNK), 1)
    w_tril = [jnp.where(r_i >= c_i, ws_ref[g], jnp.zeros((CHUNK, CHUNK), BF16)) for g in range(G_A)]
    mix_rows = []
    for ci in range(tm // CHUNK):
        rs = slice(ci * CHUNK, (ci + 1) * CHUNK)
        cols = [_dot(w_tril[g], va16[rs, g * c_g:(g + 1) * c_g]) for g in range(G_A)]
        mix_rows.append(jnp.concatenate(cols, axis=1) + bias_ref[...])
    a = u * jnp.concatenate(mix_rows, axis=0)

    o_refs, l_refs = (o0_ref, o1_ref, o2_ref), (l0_ref, l1_ref, l2_ref)
    o_slabs, lses = [], []
    for gi, (_, dil) in enumerate(GROUPS):
        if not prompt or dil == 1:
            o_slabs.append([o_refs[gi][:, cc * LANES:(cc + 1) * LANES].astype(F32)
                            for cc in range(GROUP_SLABS)])
            lses.append(l_refs[gi][...])
            continue
        o_sc, l_sc = scratch[2 * (gi - 1)], scratch[2 * (gi - 1) + 1]
        piece = tm // dil
        for r in range(dil):
            rows = slice(r * piece, (r + 1) * piece)
            for cc in range(GROUP_SLABS):
                o_sc[cc, pl.ds(r, piece, stride=dil), :] = (
                    o_refs[gi][rows, cc * LANES:(cc + 1) * LANES].astype(F32))
            l_sc[pl.ds(r, piece, stride=dil), :] = l_refs[gi][rows, :]
        o_slabs.append([o_sc[cc] for cc in range(GROUP_SLABS)])
        lses.append(l_sc[...])

    m = jnp.maximum(jnp.maximum(lses[0], lses[1]), lses[2])
    ws = [jnp.exp(l - m) for l in lses]
    inv = 1.0 / (ws[0] + ws[1] + ws[2])
    ws = [w * inv for w in ws]
    lane = lax.broadcasted_iota(jnp.int32, (tm, LANES), 1)
    first_head = lane < HEAD_DIM
    b_cols = []
    for p in range(GROUP_SLABS):
        acc = jnp.zeros((tm, LANES), F32)
        for w, slabs in zip(ws, o_slabs):
            w_pair = jnp.where(first_head, w[:, 2 * p:2 * p + 1], w[:, 2 * p + 1:2 * p + 2])
            acc = acc + w_pair * slabs[p]
        b_cols.append(acc)
    b = jnp.concatenate(b_cols, axis=1)

    gate_a = jax.nn.sigmoid(_dot(h, wg_ref[:, 0:d_model]))
    gate_b = jax.nn.sigmoid(_dot(h, wg_ref[:, d_model:2 * d_model]))
    merged = gate_a * _dot(a.astype(BF16), woa_ref[...]) + gate_b * _dot(b.astype(BF16), wob_ref[...])
    x = x + _dot(merged.astype(BF16), wo_ref[...])

    h2 = _rms(x, gffn_ref[...]).astype(BF16)
    d_ff = wff1_ref.shape[1]
    ff_step = 1024
    ffn = jnp.zeros((tm, d_model), F32)
    for c0 in range(0, d_ff, ff_step):
        hid = jnp.square(jnp.maximum(_dot(h2, wff1_ref[:, c0:c0 + ff_step]), 0.0))
        ffn = ffn + _dot(hid.astype(BF16), wff2_ref[c0:c0 + ff_step, :])
    x = x + ffn

    gate = jax.nn.sigmoid(_dot(_rms(x, gple_ref[...]).astype(BF16), wpg_ref[...]))
    y_ref[...] = x + gate * _dot(pe_ref[...].astype(BF16), wpp_ref[...])


def _merge_call(x2, outs, lses, pe2, wts, ws_mat, bias, *, prompt, batch, tm):
    n_tok, d_model = x2.shape
    seq = n_tok // batch
    n_tiles = seq // tm
    d_a = wts["g_v"].shape[1]
    tile_index = lambda b, i: b * n_tiles + i
    tok = lambda b, i: (tile_index(b, i), 0)
    row_spec = lambda width: pl.BlockSpec((tm, width), tok)

    names = ("g_mix", "w_uv", "w_gates", "g_v", "ws", "bias", "w_oa", "w_ob", "w_o",
             "g_ffn", "w_ff1", "w_ff2", "g_ple", "w_ple_gate", "w_ple_proj")
    consts = dict(wts, ws=ws_mat, bias=bias)
    in_specs = ([row_spec(d_model)]
                + [row_spec(D_GROUP)] * N_GROUPS + [row_spec(LANES)] * N_GROUPS
                + [row_spec(pe2.shape[1])] + [_resident(consts[k].shape) for k in names])
    scratch = []
    if prompt:
        va_shape = jax.ShapeDtypeStruct((batch, CHUNK, d_a), F32)
        va_spec = pl.BlockSpec((1, CHUNK, d_a), lambda b, i: (b, 0, 0))
        for _, dil in GROUPS:
            if dil > 1:
                scratch += [pltpu.VMEM((GROUP_SLABS, tm, LANES), F32), pltpu.VMEM((tm, LANES), F32)]
    else:
        va_shape = jax.ShapeDtypeStruct((n_tok, d_a), F32)
        va_spec = row_spec(d_a)
    return pl.pallas_call(
        functools.partial(_merge_kernel, prompt=prompt, tm=tm, n_tiles=n_tiles),
        grid=(batch, n_tiles),
        in_specs=in_specs,
        out_specs=[row_spec(d_model), va_spec],
        out_shape=[jax.ShapeDtypeStruct((n_tok, d_model), F32), va_shape],
        scratch_shapes=scratch,
        compiler_params=pltpu.CompilerParams(
            dimension_semantics=("arbitrary", "arbitrary"), vmem_limit_bytes=VMEM_LIMIT),
        name="merge_prompt" if prompt else "merge_sample",
    )(x2, *outs, *lses, pe2, *[consts[k] for k in names])


def kernel(x_prompt, x_sample, p_prompt, p_sample, cache_kv_w128, cache_kv_w512, cache_kv_w2048,
           g_mix, w_in, g_v, w_s, b_s, g_q, g_k, w_oa, w_ob, w_o,
           g_ffn, w_ff1, w_ff2, g_ple, w_ple_gate, w_ple_proj):
    depth = w_in.shape[0]
    batch, seq, d_model = x_prompt.shape
    dec_batch, t_new, _ = x_sample.shape
    d_a = g_v.shape[1]
    n_sample = dec_batch * t_new
    assert n_sample == CHUNK and w_in.shape[2] == 2 * d_a + 3 * D_QKV + 2 * d_model
    caches = (cache_kv_w128, cache_kv_w512, cache_kv_w2048)

    xp = x_prompt.reshape(batch * seq, d_model)
    xs = x_sample.reshape(n_sample, d_model)
    pos_p = np.arange(seq, dtype=np.int32)
    pos_s = PAST_LEN + np.tile(np.arange(t_new, dtype=np.int32), dec_batch)

    kv_p = [[] for _ in GROUPS]
    kv_s = [[] for _ in GROUPS]
    cv_p, cv_s = [], []
    for l in range(depth):
        wts = {
            "g_mix": g_mix[l][None, :], "w_uv": w_in[l][:, 0:2 * d_a].astype(BF16),
            "w_gates": w_in[l][:, 2 * d_a + 3 * D_QKV:].astype(BF16), "g_v": g_v[l][None, :],
            "w_oa": w_oa[l].astype(BF16), "w_ob": w_ob[l].astype(BF16), "w_o": w_o[l].astype(BF16),
            "g_ffn": g_ffn[l][None, :], "w_ff1": w_ff1[l].astype(BF16), "w_ff2": w_ff2[l].astype(BF16),
            "g_ple": g_ple[l][None, :], "w_ple_gate": w_ple_gate[l].astype(BF16),
            "w_ple_proj": w_ple_proj[l].astype(BF16),
        }
        w_qkv = w_in[l][:, 2 * d_a:2 * d_a + 3 * D_QKV].astype(BF16)
        c_g = d_a // G_A
        ws_p = w_s[l].astype(BF16)
        bias_p = jnp.repeat(b_s[l].T, c_g, axis=1)
        tok = np.arange(CHUNK)
        same_seq = jnp.asarray(tok[:, None] // t_new == tok[None, :] // t_new)
        reps = CHUNK // t_new
        ws_s = jnp.where(same_seq, jnp.tile(w_s[l][:, :t_new, :t_new], (1, reps, reps)), 0.0).astype(BF16)
        bias_s = jnp.tile(jnp.repeat(b_s[l][:, :t_new].T, c_g, axis=1), (reps, 1))

        tm = TOKEN_TILE
        res = _qkv_call(xp, pos_p, g_mix[l], w_qkv, g_q[l], g_k[l], prompt=True, batch=batch, tm=tm)
        q_g, k_g, v_g, kv_tails = res[0:3], res[3:6], res[6:9], res[9:12]
        outs, lses = zip(*[_attn_prompt_call(q_g[gi], k_g[gi], v_g[gi], gi, batch, seq)
                           for gi in range(N_GROUPS)])
        xp, va_p = _merge_call(xp, outs, lses, p_prompt[l].reshape(batch * seq, -1), wts, ws_p, bias_p,
                               prompt=True, batch=batch, tm=tm)
        for gi, st in enumerate(kv_tails):
            kv_p[gi].append(jnp.transpose(st, (0, 1, 4, 2, 3)))
        cv_p.append(va_p)

        qs, knt, vnt = _qkv_call(
            xs, pos_s, g_mix[l], w_qkv, g_q[l], g_k[l], prompt=False, batch=1, tm=n_sample)
        outs, lses = [], []
        for gi in range(N_GROUPS):
            o, lse, nbuf = _attn_sample_call(qs, knt, vnt, caches[gi][l], gi, dec_batch, t_new)
            outs.append(o)
            lses.append(lse)
            kv_s[gi].append(nbuf)
        xs, va_s = _merge_call(xs, outs, lses, p_sample[l].reshape(n_sample, -1), wts, ws_s, bias_s,
                               prompt=False, batch=1, tm=n_sample)
        cv_s.append(va_s.reshape(dec_batch, t_new, d_a))

    return (xp.reshape(batch, seq, d_model), xs.reshape(dec_batch, t_new, d_model),
            jnp.stack(kv_p[0]), jnp.stack(kv_p[1]), jnp.stack(kv_p[2]),
            jnp.stack(kv_s[0]), jnp.stack(kv_s[1]), jnp.stack(kv_s[2]),
            jnp.stack(cv_p), jnp.stack(cv_s))
```

```python
import functools
import math

import jax
import jax.numpy as jnp
import numpy as np
from jax import lax
from jax.experimental import pallas as pl
from jax.experimental.pallas import tpu as pltpu

F32 = jnp.float32
BF16 = jnp.bfloat16

LANES = 128
BF16_ROWS = 16
MXU_DIM = 256
TOKEN_TILE = 256
HEAD_DIM = 64
H_G = 8
HEADS_PER_VREG = LANES // HEAD_DIM
GROUPS = ((128, 1), (512, 4), (2048, 16))
N_GROUPS = len(GROUPS)
D_GROUP = H_G * HEAD_DIM
GROUP_SLABS = D_GROUP // LANES
D_QKV = N_GROUPS * D_GROUP
N_KEYS = 129
SUB_BLOCK = 128
ROT_DIM = HEAD_DIM // 4
ROT_HALF = ROT_DIM // 2
ROPE_THETA = 500000.0
CHUNK = 128
G_A = 4
PAST_LEN = 8192
ATTN_SUB_BLOCKS = 4
EPS = 1e-6
NEG = -1e30
LN2 = math.log(2.0)
LOG2E = 1.0 / LN2
VMEM_LIMIT = 56 * 1024 * 1024


def _rms(x, g):
    return x * lax.rsqrt(jnp.mean(x * x, axis=-1, keepdims=True) + EPS) * g


def _resident(shape):
    zeros = (0,) * len(shape)
    return pl.BlockSpec(shape, lambda *_: zeros, pipeline_mode=pl.Buffered(1))


def _dot(a, b):
    return jnp.dot(a, b, preferred_element_type=F32)


def _dot_nt(a, b):
    return lax.dot_general(a, b, (((1,), (1,)), ((), ())), preferred_element_type=F32)


def _project_qkv(x_ref, gmix_ref, w_ref, gq_ref, gk_ref, rot_ref, bd_ref, perm_refs, emit):
    h = _rms(x_ref[...], gmix_ref[...]).astype(BF16)
    hs = [h if p_ref is None else _dot(p_ref[...], h).astype(BF16) for p_ref in perm_refs]
    bd = bd_ref[...]
    wide = bd.shape[0]

    for ti, g_ref in enumerate((gq_ref, gk_ref, None)):
        for gi in range(N_GROUPS):
            c0 = ti * D_QKV + gi * D_GROUP
            z = _dot(hs[gi], w_ref[:, c0:c0 + D_GROUP])
            if g_ref is None:
                emit(ti, gi, [z[:, c * LANES:(c + 1) * LANES] for c in range(GROUP_SLABS)])
                continue
            t0 = 3 * LANES * min(gi, rot_ref.shape[1] // (3 * LANES) - 1)
            rc, ru, rd = (rot_ref[:, t0 + k * LANES:t0 + (k + 1) * LANES] for k in range(3))

            def rot(zn):
                return zn * rc + pltpu.roll(zn, LANES - ROT_HALF, 1) * ru + pltpu.roll(zn, ROT_HALF, 1) * rd

            vals = []
            for c in range(0, D_GROUP, wide):
                zc = z[:, c:c + wide]
                ss = _dot((zc * zc).astype(BF16), bd)
                zn = zc * lax.rsqrt(ss * (1.0 / HEAD_DIM) + EPS)
                vals += [rot(zn[:, j:j + LANES] * g_ref[...]) for j in range(0, wide, LANES)]
            emit(ti, gi, vals)


def _qkv_prompt_kernel(*refs, tm, n_tiles):
    n_perm = sum(dil > 1 for _, dil in GROUPS)
    ins, perms, rest = refs[:7], refs[7:7 + n_perm], refs[7 + n_perm:]
    outs, kv_refs = rest[:3 * N_GROUPS], rest[3 * N_GROUPS:4 * N_GROUPS]
    kv_slabs, order_slab = rest[4 * N_GROUPS:4 * N_GROUPS + 2], rest[4 * N_GROUPS + 2]
    perms = list(perms)
    perm_refs = [perms.pop(0) if dil > 1 else None for _, dil in GROUPS]

    def emit(ti, gi, vals):
        out = outs[ti * N_GROUPS + gi]
        for cc, val in enumerate(vals):
            out[:, cc * LANES:(cc + 1) * LANES] = val.astype(BF16)
            if ti > 0:
                kv_slabs[ti - 1][gi * GROUP_SLABS + cc] = val

    _project_qkv(*ins, perm_refs, emit)

    i = pl.program_id(1)
    seq = n_tiles * tm
    for gi, (win, dil) in enumerate(GROUPS):
        rows = min(win, tm)
        first = (seq - win) // tm if win >= tm else n_tiles - 1
        piece = tm // dil

        @pl.when(i >= first)
        def _(gi=gi, dil=dil, rows=rows, piece=piece):
            for kv in range(2):
                for cc in range(GROUP_SLABS):
                    c = gi * GROUP_SLABS + cc
                    if dil > 1:
                        for r in range(dil):
                            order_slab[pl.ds(r, piece, stride=dil), :] = kv_slabs[kv][c, r * piece:(r + 1) * piece, :]
                        val = order_slab[...]
                    else:
                        val = kv_slabs[kv][c]
                    t = val[tm - rows:tm].T
                    for hh in range(HEADS_PER_VREG):
                        kv_refs[gi][0, kv, cc * HEADS_PER_VREG + hh] = t[hh * HEAD_DIM:(hh + 1) * HEAD_DIM, :]


def _qkv_sample_kernel(*refs):
    ins, outs = refs[:7], refs[7:10]

    def emit(ti, gi, vals):
        for cc, val in enumerate(vals):
            c = gi * GROUP_SLABS + cc
            if ti == 0:
                outs[ti][:, c * LANES:(c + 1) * LANES] = val
            else:
                outs[ti][c * LANES:(c + 1) * LANES, :] = val.T

    _project_qkv(*ins, [None] * N_GROUPS, emit)


def _residue_major_source(dil, tm):
    row = np.arange(tm, dtype=np.int32)
    piece = tm // dil
    return (row % piece) * dil + row // piece


def _rotary_tables(pos):
    inv = np.float32(ROPE_THETA) ** (-np.arange(ROT_HALF, dtype=np.float32) * np.float32(2.0 / ROT_DIM))
    ang = pos.astype(np.float32)[:, None] * inv[None, :]
    cos, sin = np.cos(ang), np.sin(ang)
    n = pos.shape[0]
    pad = np.zeros((n, HEAD_DIM - ROT_DIM), np.float32)
    zero = np.zeros((n, ROT_HALF), np.float32)
    rc = np.concatenate([cos, cos, pad + 1.0], axis=1)
    ru = np.concatenate([-sin, zero, pad], axis=1)
    rd = np.concatenate([zero, sin, pad], axis=1)
    return np.concatenate([np.tile(t, (1, HEADS_PER_VREG)) for t in (rc, ru, rd)], axis=1).astype(np.float32)


def _qkv_call(x2, pos, g_mix, w_qkv, g_q, g_k, *, prompt, batch, tm):
    n_tok, d_model = x2.shape
    seq = n_tok // batch
    n_tiles = seq // tm
    perms = []
    if prompt:
        assert all(tm % (dil * BF16_ROWS) == 0 for _, dil in GROUPS)
        tile_start = (np.arange(seq, dtype=np.int32) // tm) * tm
        tables = []
        for _, dil in GROUPS:
            src = _residue_major_source(dil, tm)
            tables.append(_rotary_tables(pos[tile_start + np.tile(src, n_tiles)]))
            if dil > 1:
                perms.append(jnp.asarray(src[:, None] == np.arange(tm)[None, :], dtype=BF16))
        rot = jnp.asarray(np.concatenate(tables, axis=1))
    else:
        rot = jnp.asarray(_rotary_tables(pos))
    gq = jnp.tile(g_q.astype(F32), HEADS_PER_VREG)[None, :] * (LOG2E * HEAD_DIM ** -0.5)
    gk = jnp.tile(g_k.astype(F32), HEADS_PER_VREG)[None, :]
    lane = np.arange(MXU_DIM)
    bd = jnp.asarray(lane[:, None] // HEAD_DIM == lane[None, :] // HEAD_DIM, dtype=BF16)

    tile_index = lambda b, i: b * n_tiles + i
    tok = lambda b, i: (tile_index(b, i), 0)
    in_specs = [
        pl.BlockSpec((tm, d_model), tok),
        _resident((1, d_model)),
        _resident((d_model, 3 * D_QKV)),
        _resident((1, LANES)),
        _resident((1, LANES)),
        pl.BlockSpec((tm, rot.shape[1]), lambda b, i: (i, 0)),
        _resident((MXU_DIM, MXU_DIM)),
    ] + [_resident(p.shape) for p in perms]
    if prompt:
        out_shape = [jax.ShapeDtypeStruct((n_tok, D_GROUP), BF16)] * (3 * N_GROUPS)
        out_specs = [pl.BlockSpec((tm, D_GROUP), tok)] * (3 * N_GROUPS)
        for win, _ in GROUPS:
            rows = min(win, tm)
            first = (seq - win) // tm if win >= tm else n_tiles - 1
            out_shape.append(jax.ShapeDtypeStruct((batch, 2, H_G, HEAD_DIM, win), F32))
            out_specs.append(pl.BlockSpec(
                (1, 2, H_G, HEAD_DIM, rows),
                lambda b, i, first=first: (b, 0, 0, 0, jnp.maximum(i - first, 0))))
        scratch = [pltpu.VMEM((D_QKV // LANES, tm, LANES), F32)] * 2 + [pltpu.VMEM((tm, LANES), F32)]
        body = functools.partial(_qkv_prompt_kernel, tm=tm, n_tiles=n_tiles)
    else:
        assert n_tiles == 1 and batch == 1 and tm == LANES
        out_shape = [jax.ShapeDtypeStruct((n_tok, D_QKV), F32)] + [jax.ShapeDtypeStruct((D_QKV, n_tok), F32)] * 2
        out_specs = [pl.BlockSpec((tm, D_QKV), tok)] + [pl.BlockSpec((D_QKV, tm), lambda b, i: (0, 0))] * 2
        scratch = []
        body = _qkv_sample_kernel

    return pl.pallas_call(
        body,
        grid=(batch, n_tiles),
        in_specs=in_specs,
        out_specs=out_specs,
        out_shape=out_shape,
        scratch_shapes=scratch,
        compiler_params=pltpu.CompilerParams(
            dimension_semantics=("arbitrary", "arbitrary"), vmem_limit_bytes=VMEM_LIMIT),
        name="qkv_prompt" if prompt else "qkv_sample",
    )(x2, g_mix[None, :], w_qkv, gq, gk, rot, bd, *perms)


def _attn_kernel(q_ref, kc_ref, kp_ref, vc_ref, vp_ref, *rest, dil, tasks, n_alias, t_new):
    n_in = 4 * len(tasks)
    n_skip = n_in + n_alias
    task_ins, (o_ref, lse_ref), task_outs = rest[:n_in], rest[n_skip:n_skip + 2], rest[n_skip + 2:]
    step = (pl.program_id(0) * pl.num_programs(1) + pl.program_id(1)) * pl.num_programs(2) + pl.program_id(2)

    has_prev = pl.program_id(1) > 0
    piece = TOKEN_TILE // dil

    def residue_rows(i):
        return [(t, 0, slice(i * piece, (i + 1) * piece)) for t in range(SUB_BLOCK // piece)]

    def load(ref, rows, sl):
        parts = [ref[rs + (sl,)] for rs in rows]
        return parts[0] if len(parts) == 1 else jnp.concatenate(parts, axis=0)

    def store(ref, rows, sl, val):
        n = val.shape[0] // len(rows)
        for t, rs in enumerate(rows):
            ref[rs + (sl,)] = val[t * n:(t + 1) * n]
    key = lax.broadcasted_iota(jnp.int32, (2 * SUB_BLOCK, HEADS_PER_VREG * SUB_BLOCK), 0)
    qry = lax.broadcasted_iota(jnp.int32, (2 * SUB_BLOCK, HEADS_PER_VREG * SUB_BLOCK), 1) % SUB_BLOCK
    vis_cur = key <= qry
    vis_prev = key - SUB_BLOCK >= qry
    lane16 = lax.broadcasted_iota(jnp.int32, (SUB_BLOCK, LANES), 1).astype(F32).astype(BF16)
    head_lanes = (lane16 < HEAD_DIM, lane16 >= HEAD_DIM)
    eye_r = lax.broadcasted_iota(jnp.int32, (LANES, LANES), 0)
    eye_c = lax.broadcasted_iota(jnp.int32, (LANES, LANES), 1)
    eye = (eye_r == eye_c).astype(F32).astype(BF16)
    ones_rows = jnp.ones((BF16_ROWS, 2 * SUB_BLOCK), BF16)
    head_row = lax.broadcasted_iota(jnp.int32, (H_G, SUB_BLOCK), 0)

    pad = jnp.zeros((LANES - H_G, SUB_BLOCK), F32)
    slabs = [slice(p * LANES, (p + 1) * LANES) for p in range(GROUP_SLABS)]

    if dil == 1:
        blocks = [([(slice(0, SUB_BLOCK),)], kp_ref, vp_ref, [(slice(0, SUB_BLOCK),)], has_prev)]
        blocks += [([(slice(i * SUB_BLOCK, (i + 1) * SUB_BLOCK),)], kc_ref, vc_ref,
                    [(slice((i - 1) * SUB_BLOCK, i * SUB_BLOCK),)], True) for i in range(1, ATTN_SUB_BLOCKS)]
    else:
        blocks = [(residue_rows(i), kp_ref, vp_ref, residue_rows(i), has_prev)
                  for i in range(ATTN_SUB_BLOCKS)]

    scores, vts = [], []
    for rows, kp_src, vp_src, prev_rows, _ in blocks:
        for sl in slabs:
            q = load(q_ref, rows, sl)
            qq = jnp.concatenate([jnp.where(hl, q, jnp.zeros_like(q)) for hl in head_lanes], axis=0)
            kk = jnp.concatenate([load(kc_ref, rows, sl), load(kp_src, prev_rows, sl)], axis=0)
            vv = jnp.concatenate([load(vc_ref, rows, sl), load(vp_src, prev_rows, sl)], axis=0)
            scores.append(_dot_nt(kk, qq))
            vts.append(jnp.concatenate([_dot_nt(eye, vv).astype(BF16), ones_rows], axis=0))
    for ti, (task_dil, first_head) in enumerate(tasks):
        _sample_attention(*task_ins[4 * ti:4 * ti + 4], *task_outs[3 * ti:3 * ti + 3], step,
                          dil=task_dil, first_head=first_head, t_new=t_new)
    maxes, probs = [], []
    for bi, (_, _, _, _, has_prev) in enumerate(blocks):
        visible = jnp.logical_or(vis_cur, jnp.logical_and(vis_prev, has_prev))
        for s in scores[bi * GROUP_SLABS:(bi + 1) * GROUP_SLABS]:
            s = jnp.where(visible, s, NEG)
            m = jnp.max(s, axis=0, keepdims=True)
            maxes.append(m)
            probs.append(jnp.exp2(s - m).astype(BF16))
    outs = [_dot(vt, e) for vt, e in zip(vts, probs)]
    for bi, (rows, _, _, _, _) in enumerate(blocks):
        lse_rows = jnp.zeros((H_G, SUB_BLOCK), F32)
        for p, sl in enumerate(slabs):
            ot, m = outs[bi * GROUP_SLABS + p], maxes[bi * GROUP_SLABS + p]
            den = ot[LANES:LANES + 1, :]
            lse = m * LN2 + jnp.log(den)
            inv = 1.0 / den
            o_halves = []
            for half in range(HEADS_PER_VREG):
                qs = slice(half * SUB_BLOCK, (half + 1) * SUB_BLOCK)
                o_halves.append(ot[half * HEAD_DIM:(half + 1) * HEAD_DIM, qs] * inv[:, qs])
                lse_rows = jnp.where(head_row == p * HEADS_PER_VREG + half, lse[:, qs], lse_rows)
            store(o_ref, rows, sl, jnp.concatenate(o_halves, axis=0).T.astype(BF16))
        store(lse_ref, rows, slice(None), jnp.concatenate([lse_rows, pad], axis=0).T)


def _attn_call(q, k, v, gi, batch, seq, sample_tasks, t_new):
    dil = GROUPS[gi][1]
    n_tok = batch * seq
    if dil == 1:
        rows, parts = ATTN_SUB_BLOCKS * SUB_BLOCK, 1
        assert seq % rows == 0
        steps = seq // rows
        view = lambda width: (n_tok, width)
        cur_map = lambda b, s, p: (b * steps + s, 0)
        cur = lambda width: pl.BlockSpec((rows, width), cur_map)
        prev = pl.BlockSpec((SUB_BLOCK, D_GROUP),
                            lambda b, s, p: (jnp.maximum((b * steps + s) * ATTN_SUB_BLOCKS - 1, 0), 0))
    else:
        span, parts = SUB_BLOCK * dil, dil // ATTN_SUB_BLOCKS
        assert dil % ATTN_SUB_BLOCKS == 0 and span % TOKEN_TILE == 0 and seq % span == 0
        assert TOKEN_TILE % (dil * BF16_ROWS) == 0
        steps, span_tiles, part_rows = seq // span, span // TOKEN_TILE, TOKEN_TILE // parts
        view = lambda width: (n_tok // TOKEN_TILE, parts, part_rows, width)
        cur_map = lambda b, s, p: (b * steps + s, p, 0, 0)
        cur = lambda width: pl.BlockSpec((span_tiles, 1, part_rows, width), cur_map)
        prev = pl.BlockSpec((span_tiles, 1, part_rows, D_GROUP),
                            lambda b, s, p: (b * steps + jnp.maximum(s - 1, 0), p, 0, 0))
    q, k, v = (t.reshape(view(D_GROUP)) for t in (q, k, v))
    in_specs = [cur(D_GROUP), cur(D_GROUP), prev, cur(D_GROUP), prev]
    out_specs = [cur(D_GROUP), cur(LANES)]
    out_shape = [jax.ShapeDtypeStruct(view(D_GROUP), BF16), jax.ShapeDtypeStruct(view(LANES), F32)]
    operands, alias_operands, aliases = [q, k, k, v, v], [], {}
    seq_of = lambda b, s, p: (b * steps + s) * parts + p
    for task in sample_tasks:
        n_seq, q_rows, _ = task["q"].shape
        assert n_seq == batch * steps * parts
        wb = task["buf"].shape[4]
        n_h, h0 = task["n_heads"], task["first_head"]
        cols = n_h * HEAD_DIM
        col_block = (task["group"] * D_GROUP + h0 * HEAD_DIM) // cols
        buf_spec = pl.BlockSpec((1, 2, n_h, HEAD_DIM, wb), lambda b, s, p, hb=h0 // n_h: (seq_of(b, s, p), 0, hb, 0, 0))
        new_spec = pl.BlockSpec((cols, LANES), lambda b, s, p, cb=col_block: (cb, 0))
        in_specs += [pl.BlockSpec((1, q_rows, cols), lambda b, s, p, cb=col_block: (seq_of(b, s, p), 0, cb)),
                     new_spec, new_spec, buf_spec]
        operands += [task["q"], task["knt"], task["vnt"], task["buf"]]
        if task.get("nbuf") is not None:
            aliases[len(alias_operands)] = len(out_shape) + 2
            alias_operands.append(task["nbuf"])
        out_specs += [pl.BlockSpec((1, q_rows, cols), lambda b, s, p: (seq_of(b, s, p), 0, 0)),
                      pl.BlockSpec((1, q_rows, LANES), lambda b, s, p: (seq_of(b, s, p), 0, 0)), buf_spec]
        out_shape += [jax.ShapeDtypeStruct((n_seq, q_rows, cols), F32),
                      jax.ShapeDtypeStruct((n_seq, q_rows, LANES), F32),
                      jax.ShapeDtypeStruct(task["buf"].shape, F32)]
    n_main = len(operands)
    res = pl.pallas_call(
        functools.partial(_attn_kernel, dil=dil, n_alias=len(alias_operands), t_new=t_new,
                          tasks=tuple((GROUPS[t["group"]][1], t["first_head"]) for t in sample_tasks)),
        grid=(batch, steps, parts),
        in_specs=in_specs + [pl.BlockSpec(memory_space=pl.ANY)] * len(alias_operands),
        out_specs=out_specs,
        out_shape=out_shape,
        input_output_aliases={n_main + i: o for i, o in aliases.items()},
        compiler_params=pltpu.CompilerParams(
            dimension_semantics=("arbitrary", "arbitrary", "arbitrary"), vmem_limit_bytes=VMEM_LIMIT),
        name=f"attn_g{gi}",
    )(*operands, *alias_operands)
    task_res = [tuple(res[2 + 3 * i:5 + 3 * i]) for i in range(len(sample_tasks))]
    return res[0].reshape(n_tok, D_GROUP), res[1].reshape(n_tok, LANES), task_res


def _sample_attention(q_ref, knt_ref, vnt_ref, buf_ref, o_ref, lse_ref, nbuf_ref, seq_index, *,
                      dil, first_head, t_new):
    assert dil & (dil - 1) == 0
    q_rows = q_ref.shape[1]
    n_heads, wb = buf_ref.shape[2], buf_ref.shape[4]
    width = wb + LANES
    shift = lax.rem(LANES - t_new * seq_index, LANES)
    knt = pltpu.roll(knt_ref[...], shift, 1)
    vnt = pltpu.roll(vnt_ref[...], shift, 1)
    q = q_ref[0]

    t_row = lax.broadcasted_iota(jnp.int32, (q_rows, width), 0)
    p_col = lax.broadcasted_iota(jnp.int32, (q_rows, width), 1)
    back = wb + t_row - p_col
    valid = jnp.logical_and(jnp.logical_and(back >= 0, (back & (dil - 1)) == 0),
                            jnp.logical_and(back <= (N_KEYS - 1) * dil, p_col < wb + t_new))
    lane = lax.broadcasted_iota(jnp.int32, (q_rows, LANES), 1)
    lse_tile = jnp.zeros((q_rows, LANES), F32)
    o_heads = []
    for h in range(n_heads):
        rows = slice(h * HEAD_DIM, (h + 1) * HEAD_DIM)
        kx = jnp.concatenate([buf_ref[0, 0, h], knt[rows, :]], axis=1)
        vx = jnp.concatenate([buf_ref[0, 1, h], vnt[rows, :]], axis=1)
        nbuf_ref[0, 0, h] = pltpu.roll(kx, width - t_new, 1)[:, :wb]
        nbuf_ref[0, 1, h] = pltpu.roll(vx, width - t_new, 1)[:, :wb]
        s = jnp.where(valid, _dot(q[:, rows].astype(BF16), kx.astype(BF16)), NEG)
        m = jnp.max(s, axis=1, keepdims=True)
        e = jnp.exp2(s - m)
        den = jnp.sum(e, axis=1, keepdims=True)
        o_heads.append(_dot_nt(e.astype(BF16), vx.astype(BF16)) / den)
        lse_tile = jnp.where(lane == first_head + h, m * LN2 + jnp.log(den), lse_tile)
    o_ref[0] = jnp.concatenate(o_heads, axis=1)
    lse_ref[0] = lse_tile


def _merge_kernel(x_ref, o0_ref, o1_ref, o2_ref, l0_ref, l1_ref, l2_ref, pe_ref,
                  gmix_ref, wuv_ref, wg_ref, gv_ref, ws_ref, bias_ref, woa_ref, wob_ref, wo_ref,
                  gffn_ref, wff1_ref, wff2_ref, gple_ref, wpg_ref, wpp_ref,
                  y_ref, va_ref, *scratch, prompt, tm, n_tiles):
    d_model = x_ref.shape[1]
    d_a = gv_ref.shape[1]
    c_g = d_a // G_A
    x = x_ref[...]
    h = _rms(x, gmix_ref[...]).astype(BF16)

    u = jax.nn.gelu(_dot(h, wuv_ref[:, 0:d_a]))
    va = _rms(jax.nn.gelu(_dot(h, wuv_ref[:, d_a:2 * d_a])), gv_ref[...])
    if prompt:
        @pl.when(pl.program_id(1) == n_tiles - 1)
        def _():
            va_ref[0] = va[tm - CHUNK:tm, :]
    else:
        va_ref[...] = va
    va16 = va.astype(BF16)
    r_i = lax.broadcasted_iota(jnp.int32, (CHUNK, CHUNK), 0)
    c_i = lax.broadcasted_iota(jnp.int32, (CHUNK, CHUNK), 1)
    w_tril = [jnp.where(r_i >= c_i, ws_ref[g], jnp.zeros((CHUNK, CHUNK), BF16)) for g in range(G_A)]
    mix_rows = []
    for ci in range(tm // CHUNK):
        rs = slice(ci * CHUNK, (ci + 1) * CHUNK)
        cols = [_dot(w_tril[g], va16[rs, g * c_g:(g + 1) * c_g]) for g in range(G_A)]
        mix_rows.append(jnp.concatenate(cols, axis=1) + bias_ref[...])
    a = u * jnp.concatenate(mix_rows, axis=0)

    o_refs, l_refs = (o0_ref, o1_ref, o2_ref), (l0_ref, l1_ref, l2_ref)
    o_slabs, lses = [], []
    for gi, (_, dil) in enumerate(GROUPS):
        if not prompt or dil == 1:
            o_slabs.append([o_refs[gi][:, cc * LANES:(cc + 1) * LANES].astype(F32)
                            for cc in range(GROUP_SLABS)])
            lses.append(l_refs[gi][...])
            continue
        o_sc, l_sc = scratch[2 * (gi - 1)], scratch[2 * (gi - 1) + 1]
        piece = tm // dil
        for r in range(dil):
            rows = slice(r * piece, (r + 1) * piece)
            for cc in range(GROUP_SLABS):
                o_sc[cc, pl.ds(r, piece, stride=dil), :] = (
                    o_refs[gi][rows, cc * LANES:(cc + 1) * LANES].astype(F32))
            l_sc[pl.ds(r, piece, stride=dil), :] = l_refs[gi][rows, :]
        o_slabs.append([o_sc[cc] for cc in range(GROUP_SLABS)])
        lses.append(l_sc[...])

    m = jnp.maximum(jnp.maximum(lses[0], lses[1]), lses[2])
    ws = [jnp.exp(l - m) for l in lses]
    inv = 1.0 / (ws[0] + ws[1] + ws[2])
    ws = [w * inv for w in ws]
    lane = lax.broadcasted_iota(jnp.int32, (tm, LANES), 1)
    first_head = lane < HEAD_DIM
    b_cols = []
    for p in range(GROUP_SLABS):
        acc = jnp.zeros((tm, LANES), F32)
        for w, slabs in zip(ws, o_slabs):
            w_pair = jnp.where(first_head, w[:, 2 * p:2 * p + 1], w[:, 2 * p + 1:2 * p + 2])
            acc = acc + w_pair * slabs[p]
        b_cols.append(acc)
    b = jnp.concatenate(b_cols, axis=1)

    gate_a = jax.nn.sigmoid(_dot(h, wg_ref[:, 0:d_model]))
    gate_b = jax.nn.sigmoid(_dot(h, wg_ref[:, d_model:2 * d_model]))
    merged = gate_a * _dot(a.astype(BF16), woa_ref[...]) + gate_b * _dot(b.astype(BF16), wob_ref[...])
    x = x + _dot(merged.astype(BF16), wo_ref[...])

    h2 = _rms(x, gffn_ref[...]).astype(BF16)
    d_ff = wff1_ref.shape[1]
    ff_step = 1024
    ffn = jnp.zeros((tm, d_model), F32)
    for c0 in range(0, d_ff, ff_step):
        hid = jnp.square(jnp.maximum(_dot(h2, wff1_ref[:, c0:c0 + ff_step]), 0.0))
        ffn = ffn + _dot(hid.astype(BF16), wff2_ref[c0:c0 + ff_step, :])
    x = x + ffn

    gate = jax.nn.sigmoid(_dot(_rms(x, gple_ref[...]).astype(BF16), wpg_ref[...]))
    y_ref[...] = x + gate * _dot(pe_ref[...].astype(BF16), wpp_ref[...])


def _merge_call(x2, outs, lses, pe2, wts, ws_mat, bias, *, prompt, batch, tm):
    n_tok, d_model = x2.shape
    seq = n_tok // batch
    n_tiles = seq // tm
    d_a = wts["g_v"].shape[1]
    tile_index = lambda b, i: b * n_tiles + i
    tok = lambda b, i: (tile_index(b, i), 0)
    row_spec = lambda width: pl.BlockSpec((tm, width), tok)

    names = ("g_mix", "w_uv", "w_gates", "g_v", "ws", "bias", "w_oa", "w_ob", "w_o",
             "g_ffn", "w_ff1", "w_ff2", "g_ple", "w_ple_gate", "w_ple_proj")
    consts = dict(wts, ws=ws_mat, bias=bias)
    in_specs = ([row_spec(d_model)]
                + [row_spec(D_GROUP)] * N_GROUPS + [row_spec(LANES)] * N_GROUPS
                + [row_spec(pe2.shape[1])] + [_resident(consts[k].shape) for k in names])
    scratch = []
    if prompt:
        va_shape = jax.ShapeDtypeStruct((batch, CHUNK, d_a), F32)
        va_spec = pl.BlockSpec((1, CHUNK, d_a), lambda b, i: (b, 0, 0))
        for _, dil in GROUPS:
            if dil > 1:
                scratch += [pltpu.VMEM((GROUP_SLABS, tm, LANES), F32), pltpu.VMEM((tm, LANES), F32)]
    else:
        va_shape = jax.ShapeDtypeStruct((n_tok, d_a), F32)
        va_spec = row_spec(d_a)
    return pl.pallas_call(
        functools.partial(_merge_kernel, prompt=prompt, tm=tm, n_tiles=n_tiles),
        grid=(batch, n_tiles),
        in_specs=in_specs,
        out_specs=[row_spec(d_model), va_spec],
        out_shape=[jax.ShapeDtypeStruct((n_tok, d_model), F32), va_shape],
        scratch_shapes=scratch,
        compiler_params=pltpu.CompilerParams(
            dimension_semantics=("arbitrary", "arbitrary"), vmem_limit_bytes=VMEM_LIMIT),
        name="merge_prompt" if prompt else "merge_sample",
    )(x2, *outs, *lses, pe2, *[consts[k] for k in names])


def kernel(x_prompt, x_sample, p_prompt, p_sample, cache_kv_w128, cache_kv_w512, cache_kv_w2048,
           g_mix, w_in, g_v, w_s, b_s, g_q, g_k, w_oa, w_ob, w_o,
           g_ffn, w_ff1, w_ff2, g_ple, w_ple_gate, w_ple_proj):
    depth = w_in.shape[0]
    batch, seq, d_model = x_prompt.shape
    dec_batch, t_new, _ = x_sample.shape
    d_a = g_v.shape[1]
    n_sample = dec_batch * t_new
    assert n_sample == CHUNK and w_in.shape[2] == 2 * d_a + 3 * D_QKV + 2 * d_model
    caches = (cache_kv_w128, cache_kv_w512, cache_kv_w2048)

    xp = x_prompt.reshape(batch * seq, d_model)
    xs = x_sample.reshape(n_sample, d_model)
    pos_p = np.arange(seq, dtype=np.int32)
    pos_s = PAST_LEN + np.tile(np.arange(t_new, dtype=np.int32), dec_batch)

    kv_p = [[] for _ in GROUPS]
    kv_s = [[] for _ in GROUPS]
    cv_p, cv_s = [], []
    for l in range(depth):
        wts = {
            "g_mix": g_mix[l][None, :], "w_uv": w_in[l][:, 0:2 * d_a].astype(BF16),
            "w_gates": w_in[l][:, 2 * d_a + 3 * D_QKV:].astype(BF16), "g_v": g_v[l][None, :],
            "w_oa": w_oa[l].astype(BF16), "w_ob": w_ob[l].astype(BF16), "w_o": w_o[l].astype(BF16),
            "g_ffn": g_ffn[l][None, :], "w_ff1": w_ff1[l].astype(BF16), "w_ff2": w_ff2[l].astype(BF16),
            "g_ple": g_ple[l][None, :], "w_ple_gate": w_ple_gate[l].astype(BF16),
            "w_ple_proj": w_ple_proj[l].astype(BF16),
        }
        w_qkv = w_in[l][:, 2 * d_a:2 * d_a + 3 * D_QKV].astype(BF16)
        c_g = d_a // G_A
        ws_p = w_s[l].astype(BF16)
        bias_p = jnp.repeat(b_s[l].T, c_g, axis=1)
        tok = np.arange(CHUNK)
        same_seq = jnp.asarray(tok[:, None] // t_new == tok[None, :] // t_new)
        reps = CHUNK // t_new
        ws_s = jnp.where(same_seq, jnp.tile(w_s[l][:, :t_new, :t_new], (1, reps, reps)), 0.0).astype(BF16)
        bias_s = jnp.tile(jnp.repeat(b_s[l][:, :t_new].T, c_g, axis=1), (reps, 1))

        tm = TOKEN_TILE
        res = _qkv_call(xp, pos_p, g_mix[l], w_qkv, g_q[l], g_k[l], prompt=True, batch=batch, tm=tm)
        q_g, k_g, v_g, kv_tails = res[0:3], res[3:6], res[6:9], res[9:12]
        qs, knt, vnt = _qkv_call(
            xs, pos_s, g_mix[l], w_qkv, g_q[l], g_k[l], prompt=False, batch=1, tm=n_sample)

        q_rows = 8
        assert LANES % t_new == 0 and t_new <= q_rows and N_GROUPS == 3
        q_pad = jnp.pad(qs.reshape(dec_batch, t_new, D_QKV), ((0, 0), (0, q_rows - t_new), (0, 0)))
        bufs = [jnp.transpose(c[l], (0, 1, 3, 4, 2)) for c in caches]
        task = lambda g, h0, n_h, **kw: dict(group=g, first_head=h0, n_heads=n_h, q=q_pad, knt=knt, vnt=vnt,
                                             buf=bufs[g], **kw)
        half = H_G // 2
        o2, l2, ((so2a, sl2a, nbuf2),) = _attn_call(q_g[2], k_g[2], v_g[2], 2, batch, seq,
                                                  [task(2, 0, half)], t_new)
        o1, l1, ((so2b, sl2b, nbuf2),) = _attn_call(q_g[1], k_g[1], v_g[1], 1, batch, seq,
                                                  [task(2, half, half, nbuf=nbuf2)], t_new)
        o0, l0, ((so0, sl0, nbuf0), (so1, sl1, nbuf1)) = _attn_call(
            q_g[0], k_g[0], v_g[0], 0, batch, seq, [task(0, 0, H_G), task(1, 0, H_G)], t_new)
        s_outs = (so0, so1, jnp.concatenate([so2a, so2b], axis=2))
        s_lses = (sl0, sl1, sl2a + sl2b)
        for gi, nbuf in enumerate((nbuf0, nbuf1, nbuf2)):
            kv_s[gi].append(jnp.transpose(nbuf, (0, 1, 4, 2, 3)))

        xp, va_p = _merge_call(xp, (o0, o1, o2), (l0, l1, l2), p_prompt[l].reshape(batch * seq, -1),
                               wts, ws_p, bias_p, prompt=True, batch=batch, tm=tm)
        for gi, st in enumerate(kv_tails):
            kv_p[gi].append(jnp.transpose(st, (0, 1, 4, 2, 3)))
        cv_p.append(va_p)

        s_outs = [o[:, :t_new].reshape(n_sample, D_GROUP) for o in s_outs]
        s_lses = [s[:, :t_new].reshape(n_sample, LANES) for s in s_lses]
        xs, va_s = _merge_call(xs, s_outs, s_lses, p_sample[l].reshape(n_sample, -1), wts, ws_s, bias_s,
                               prompt=False, batch=1, tm=n_sample)
        cv_s.append(va_s.reshape(dec_batch, t_new, d_a))

    return (xp.reshape(batch, seq, d_model), xs.reshape(dec_batch, t_new, d_model),
            jnp.stack(kv_p[0]), jnp.stack(kv_p[1]), jnp.stack(kv_p[2]),
            jnp.stack(kv_s[0]), jnp.stack(kv_s[1]), jnp.stack(kv_s[2]),
            jnp.stack(cv_p), jnp.stack(cv_s))
```

```python
import functools
import math

import jax
import jax.numpy as jnp
import numpy as np
from jax import lax
from jax.experimental import pallas as pl
from jax.experimental.pallas import tpu as pltpu

F32 = jnp.float32
BF16 = jnp.bfloat16

LANES = 128
BF16_ROWS = 16
MXU_DIM = 256
TOKEN_TILE = 256
HEAD_DIM = 64
H_G = 8
HEADS_PER_VREG = LANES // HEAD_DIM
GROUPS = ((128, 1), (512, 4), (2048, 16))
N_GROUPS = len(GROUPS)
D_GROUP = H_G * HEAD_DIM
GROUP_SLABS = D_GROUP // LANES
D_QKV = N_GROUPS * D_GROUP
N_KEYS = 129
SUB_BLOCK = 128
ROT_DIM = HEAD_DIM // 4
ROT_HALF = ROT_DIM // 2
ROPE_THETA = 500000.0
CHUNK = 128
G_A = 4
PAST_LEN = 8192
ATTN_SUB_BLOCKS = 4
EPS = 1e-6
NEG = -1e30
LN2 = math.log(2.0)
LOG2E = 1.0 / LN2
VMEM_LIMIT = 62 * 1024 * 1024


def _rms(x, g):
    return x * lax.rsqrt(jnp.mean(x * x, axis=-1, keepdims=True) + EPS) * g


def _resident(shape):
    zeros = (0,) * len(shape)
    return pl.BlockSpec(shape, lambda *_: zeros, pipeline_mode=pl.Buffered(1))


def _dot(a, b):
    return jnp.dot(a, b, preferred_element_type=F32)


def _dot_nt(a, b):
    return lax.dot_general(a, b, (((1,), (1,)), ((), ())), preferred_element_type=F32)


def _project_qkv(x_ref, gmix_ref, w_ref, gq_ref, gk_ref, rot_ref, bd_ref, perm_refs, emit):
    h = _rms(x_ref[...], gmix_ref[...]).astype(BF16)
    hs = [h if p_ref is None else _dot(p_ref[...], h).astype(BF16) for p_ref in perm_refs]
    bd = bd_ref[...]
    wide = bd.shape[0]

    for ti, g_ref in enumerate((gq_ref, gk_ref, None)):
        for gi in range(N_GROUPS):
            c0 = ti * D_QKV + gi * D_GROUP
            z = _dot(hs[gi], w_ref[:, c0:c0 + D_GROUP])
            if g_ref is None:
                emit(ti, gi, [z[:, c * LANES:(c + 1) * LANES] for c in range(GROUP_SLABS)])
                continue
            t0 = 3 * LANES * min(gi, rot_ref.shape[1] // (3 * LANES) - 1)
            rc, ru, rd = (rot_ref[:, t0 + k * LANES:t0 + (k + 1) * LANES] for k in range(3))

            def rot(zn):
                return zn * rc + pltpu.roll(zn, LANES - ROT_HALF, 1) * ru + pltpu.roll(zn, ROT_HALF, 1) * rd

            vals = []
            for c in range(0, D_GROUP, wide):
                zc = z[:, c:c + wide]
                ss = _dot((zc * zc).astype(BF16), bd)
                zn = zc * lax.rsqrt(ss * (1.0 / HEAD_DIM) + EPS)
                vals += [rot(zn[:, j:j + LANES] * g_ref[...]) for j in range(0, wide, LANES)]
            emit(ti, gi, vals)


def _qkv_prompt_kernel(*refs, tm, n_tiles):
    n_perm = sum(dil > 1 for _, dil in GROUPS)
    ins, perms, rest = refs[:7], refs[7:7 + n_perm], refs[7 + n_perm:]
    outs, kv_refs = rest[:3 * N_GROUPS], rest[3 * N_GROUPS:4 * N_GROUPS]
    kv_slabs, order_slab = rest[4 * N_GROUPS:4 * N_GROUPS + 2], rest[4 * N_GROUPS + 2]
    perms = list(perms)
    perm_refs = [perms.pop(0) if dil > 1 else None for _, dil in GROUPS]

    def emit(ti, gi, vals):
        out = outs[ti * N_GROUPS + gi]
        for cc, val in enumerate(vals):
            out[:, cc * LANES:(cc + 1) * LANES] = val.astype(BF16)
            if ti > 0:
                kv_slabs[ti - 1][gi * GROUP_SLABS + cc] = val

    _project_qkv(*ins, perm_refs, emit)

    i = pl.program_id(1)
    seq = n_tiles * tm
    for gi, (win, dil) in enumerate(GROUPS):
        rows = min(win, tm)
        first = (seq - win) // tm if win >= tm else n_tiles - 1
        piece = tm // dil

        @pl.when(i >= first)
        def _(gi=gi, dil=dil, rows=rows, piece=piece):
            for kv in range(2):
                for cc in range(GROUP_SLABS):
                    c = gi * GROUP_SLABS + cc
                    if dil > 1:
                        for r in range(dil):
                            order_slab[pl.ds(r, piece, stride=dil), :] = kv_slabs[kv][c, r * piece:(r + 1) * piece, :]
                        val = order_slab[...]
                    else:
                        val = kv_slabs[kv][c]
                    t = val[tm - rows:tm].T
                    for hh in range(HEADS_PER_VREG):
                        kv_refs[gi][0, kv, cc * HEADS_PER_VREG + hh] = t[hh * HEAD_DIM:(hh + 1) * HEAD_DIM, :]


def _qkv_sample_kernel(*refs):
    ins, outs = refs[:7], refs[7:10]

    def emit(ti, gi, vals):
        for cc, val in enumerate(vals):
            c = gi * GROUP_SLABS + cc
            if ti == 0:
                outs[ti][:, c * LANES:(c + 1) * LANES] = val
            else:
                outs[ti][c * LANES:(c + 1) * LANES, :] = val.T

    _project_qkv(*ins, [None] * N_GROUPS, emit)


def _residue_major_source(dil, tm):
    row = np.arange(tm, dtype=np.int32)
    piece = tm // dil
    return (row % piece) * dil + row // piece


def _rotary_tables(pos):
    inv = np.float32(ROPE_THETA) ** (-np.arange(ROT_HALF, dtype=np.float32) * np.float32(2.0 / ROT_DIM))
    ang = pos.astype(np.float32)[:, None] * inv[None, :]
    cos, sin = np.cos(ang), np.sin(ang)
    n = pos.shape[0]
    pad = np.zeros((n, HEAD_DIM - ROT_DIM), np.float32)
    zero = np.zeros((n, ROT_HALF), np.float32)
    rc = np.concatenate([cos, cos, pad + 1.0], axis=1)
    ru = np.concatenate([-sin, zero, pad], axis=1)
    rd = np.concatenate([zero, sin, pad], axis=1)
    return np.concatenate([np.tile(t, (1, HEADS_PER_VREG)) for t in (rc, ru, rd)], axis=1).astype(np.float32)


def _qkv_call(x2, pos, g_mix, w_qkv, g_q, g_k, *, prompt, batch, tm):
    n_tok, d_model = x2.shape
    seq = n_tok // batch
    n_tiles = seq // tm
    perms = []
    if prompt:
        assert all(tm % (dil * BF16_ROWS) == 0 for _, dil in GROUPS)
        tile_start = (np.arange(seq, dtype=np.int32) // tm) * tm
        tables = []
        for _, dil in GROUPS:
            src = _residue_major_source(dil, tm)
            tables.append(_rotary_tables(pos[tile_start + np.tile(src, n_tiles)]))
            if dil > 1:
                perms.append(jnp.asarray(src[:, None] == np.arange(tm)[None, :], dtype=BF16))
        rot = jnp.asarray(np.concatenate(tables, axis=1))
    else:
        rot = jnp.asarray(_rotary_tables(pos))
    gq = jnp.tile(g_q.astype(F32), HEADS_PER_VREG)[None, :] * (LOG2E * HEAD_DIM ** -0.5)
    gk = jnp.tile(g_k.astype(F32), HEADS_PER_VREG)[None, :]
    lane = np.arange(MXU_DIM)
    bd = jnp.asarray(lane[:, None] // HEAD_DIM == lane[None, :] // HEAD_DIM, dtype=BF16)

    tile_index = lambda b, i: b * n_tiles + i
    tok = lambda b, i: (tile_index(b, i), 0)
    in_specs = [
        pl.BlockSpec((tm, d_model), tok),
        _resident((1, d_model)),
        _resident((d_model, 3 * D_QKV)),
        _resident((1, LANES)),
        _resident((1, LANES)),
        pl.BlockSpec((tm, rot.shape[1]), lambda b, i: (i, 0)),
        _resident((MXU_DIM, MXU_DIM)),
    ] + [_resident(p.shape) for p in perms]
    if prompt:
        out_shape = [jax.ShapeDtypeStruct((n_tok, D_GROUP), BF16)] * (3 * N_GROUPS)
        out_specs = [pl.BlockSpec((tm, D_GROUP), tok)] * (3 * N_GROUPS)
        for win, _ in GROUPS:
            rows = min(win, tm)
            first = (seq - win) // tm if win >= tm else n_tiles - 1
            out_shape.append(jax.ShapeDtypeStruct((batch, 2, H_G, HEAD_DIM, win), F32))
            out_specs.append(pl.BlockSpec(
                (1, 2, H_G, HEAD_DIM, rows),
                lambda b, i, first=first: (b, 0, 0, 0, jnp.maximum(i - first, 0))))
        scratch = [pltpu.VMEM((D_QKV // LANES, tm, LANES), F32)] * 2 + [pltpu.VMEM((tm, LANES), F32)]
        body = functools.partial(_qkv_prompt_kernel, tm=tm, n_tiles=n_tiles)
    else:
        assert n_tiles == 1 and batch == 1 and tm == LANES
        out_shape = [jax.ShapeDtypeStruct((n_tok, D_QKV), F32)] + [jax.ShapeDtypeStruct((D_QKV, n_tok), F32)] * 2
        out_specs = [pl.BlockSpec((tm, D_QKV), tok)] + [pl.BlockSpec((D_QKV, tm), lambda b, i: (0, 0))] * 2
        scratch = []
        body = _qkv_sample_kernel

    return pl.pallas_call(
        body,
        grid=(batch, n_tiles),
        in_specs=in_specs,
        out_specs=out_specs,
        out_shape=out_shape,
        scratch_shapes=scratch,
        compiler_params=pltpu.CompilerParams(
            dimension_semantics=("arbitrary", "arbitrary"), vmem_limit_bytes=VMEM_LIMIT),
        name="qkv_prompt" if prompt else "qkv_sample",
    )(x2, g_mix[None, :], w_qkv, gq, gk, rot, bd, *perms)


def _attn_kernel(q_ref, kc_ref, kp_ref, vc_ref, vp_ref, *rest, dil, tasks, n_alias, t_new):
    n_in = 4 * len(tasks)
    n_skip = n_in + n_alias
    task_ins, (o_ref, lse_ref), task_outs = rest[:n_in], rest[n_skip:n_skip + 2], rest[n_skip + 2:]
    step = (pl.program_id(0) * pl.num_programs(1) + pl.program_id(1)) * pl.num_programs(2) + pl.program_id(2)

    has_prev = pl.program_id(1) > 0
    piece = TOKEN_TILE // dil

    def residue_rows(i):
        return [(t, 0, slice(i * piece, (i + 1) * piece)) for t in range(SUB_BLOCK // piece)]

    def load(ref, rows, sl):
        parts = [ref[rs + (sl,)] for rs in rows]
        return parts[0] if len(parts) == 1 else jnp.concatenate(parts, axis=0)

    def store(ref, rows, sl, val):
        n = val.shape[0] // len(rows)
        for t, rs in enumerate(rows):
            ref[rs + (sl,)] = val[t * n:(t + 1) * n]
    key = lax.broadcasted_iota(jnp.int32, (2 * SUB_BLOCK, HEADS_PER_VREG * SUB_BLOCK), 0)
    qry = lax.broadcasted_iota(jnp.int32, (2 * SUB_BLOCK, HEADS_PER_VREG * SUB_BLOCK), 1) % SUB_BLOCK
    vis_cur = key <= qry
    vis_prev = key - SUB_BLOCK >= qry
    lane16 = lax.broadcasted_iota(jnp.int32, (SUB_BLOCK, LANES), 1).astype(F32).astype(BF16)
    head_lanes = (lane16 < HEAD_DIM, lane16 >= HEAD_DIM)
    eye_r = lax.broadcasted_iota(jnp.int32, (LANES, LANES), 0)
    eye_c = lax.broadcasted_iota(jnp.int32, (LANES, LANES), 1)
    eye = (eye_r == eye_c).astype(F32).astype(BF16)
    ones_rows = jnp.ones((BF16_ROWS, 2 * SUB_BLOCK), BF16)
    head_row = lax.broadcasted_iota(jnp.int32, (H_G, SUB_BLOCK), 0)

    pad = jnp.zeros((LANES - H_G, SUB_BLOCK), F32)
    slabs = [slice(p * LANES, (p + 1) * LANES) for p in range(GROUP_SLABS)]

    if dil == 1:
        blocks = [([(slice(0, SUB_BLOCK),)], kp_ref, vp_ref, [(slice(0, SUB_BLOCK),)], has_prev)]
        blocks += [([(slice(i * SUB_BLOCK, (i + 1) * SUB_BLOCK),)], kc_ref, vc_ref,
                    [(slice((i - 1) * SUB_BLOCK, i * SUB_BLOCK),)], True) for i in range(1, ATTN_SUB_BLOCKS)]
    else:
        blocks = [(residue_rows(i), kp_ref, vp_ref, residue_rows(i), has_prev)
                  for i in range(ATTN_SUB_BLOCKS)]

    scores, vts = [], []
    for rows, kp_src, vp_src, prev_rows, _ in blocks:
        for sl in slabs:
            q = load(q_ref, rows, sl)
            qq = jnp.concatenate([jnp.where(hl, q, jnp.zeros_like(q)) for hl in head_lanes], axis=0)
            kk = jnp.concatenate([load(kc_ref, rows, sl), load(kp_src, prev_rows, sl)], axis=0)
            vv = jnp.concatenate([load(vc_ref, rows, sl), load(vp_src, prev_rows, sl)], axis=0)
            scores.append(_dot_nt(kk, qq))
            vts.append(jnp.concatenate([_dot_nt(eye, vv).astype(BF16), ones_rows], axis=0))
    for ti, (task_dil, first_head) in enumerate(tasks):
        _sample_attention(*task_ins[4 * ti:4 * ti + 4], *task_outs[3 * ti:3 * ti + 3], step,
                          dil=task_dil, first_head=first_head, t_new=t_new)
    maxes, probs = [], []
    for bi, (_, _, _, _, has_prev) in enumerate(blocks):
        visible = jnp.logical_or(vis_cur, jnp.logical_and(vis_prev, has_prev))
        for s in scores[bi * GROUP_SLABS:(bi + 1) * GROUP_SLABS]:
            s = jnp.where(visible, s, NEG)
            m = jnp.max(s, axis=0, keepdims=True)
            maxes.append(m)
            probs.append(jnp.exp2(s - m).astype(BF16))
    outs = [_dot(vt, e) for vt, e in zip(vts, probs)]
    for bi, (rows, _, _, _, _) in enumerate(blocks):
        lse_rows = jnp.zeros((H_G, SUB_BLOCK), F32)
        for p, sl in enumerate(slabs):
            ot, m = outs[bi * GROUP_SLABS + p], maxes[bi * GROUP_SLABS + p]
            den = ot[LANES:LANES + 1, :]
            lse = m * LN2 + jnp.log(den)
            inv = 1.0 / den
            o_halves = []
            for half in range(HEADS_PER_VREG):
                qs = slice(half * SUB_BLOCK, (half + 1) * SUB_BLOCK)
                o_halves.append(ot[half * HEAD_DIM:(half + 1) * HEAD_DIM, qs] * inv[:, qs])
                lse_rows = jnp.where(head_row == p * HEADS_PER_VREG + half, lse[:, qs], lse_rows)
            store(o_ref, rows, sl, jnp.concatenate(o_halves, axis=0).T.astype(BF16))
        store(lse_ref, rows, slice(None), jnp.concatenate([lse_rows, pad], axis=0).T)


def _attn_call(q, k, v, gi, batch, seq, sample_tasks, t_new):
    dil = GROUPS[gi][1]
    n_tok = batch * seq
    if dil == 1:
        rows, parts = ATTN_SUB_BLOCKS * SUB_BLOCK, 1
        assert seq % rows == 0
        steps = seq // rows
        view = lambda width: (n_tok, width)
        cur_map = lambda b, s, p: (b * steps + s, 0)
        cur = lambda width: pl.BlockSpec((rows, width), cur_map)
        prev = pl.BlockSpec((SUB_BLOCK, D_GROUP),
                            lambda b, s, p: (jnp.maximum((b * steps + s) * ATTN_SUB_BLOCKS - 1, 0), 0))
    else:
        span, parts = SUB_BLOCK * dil, dil // ATTN_SUB_BLOCKS
        assert dil % ATTN_SUB_BLOCKS == 0 and span % TOKEN_TILE == 0 and seq % span == 0
        assert TOKEN_TILE % (dil * BF16_ROWS) == 0
        steps, span_tiles, part_rows = seq // span, span // TOKEN_TILE, TOKEN_TILE // parts
        view = lambda width: (n_tok // TOKEN_TILE, parts, part_rows, width)
        cur_map = lambda b, s, p: (b * steps + s, p, 0, 0)
        cur = lambda width: pl.BlockSpec((span_tiles, 1, part_rows, width), cur_map)
        prev = pl.BlockSpec((span_tiles, 1, part_rows, D_GROUP),
                            lambda b, s, p: (b * steps + jnp.maximum(s - 1, 0), p, 0, 0))
    q, k, v = (t.reshape(view(D_GROUP)) for t in (q, k, v))
    in_specs = [cur(D_GROUP), cur(D_GROUP), prev, cur(D_GROUP), prev]
    out_specs = [cur(D_GROUP), cur(LANES)]
    out_shape = [jax.ShapeDtypeStruct(view(D_GROUP), BF16), jax.ShapeDtypeStruct(view(LANES), F32)]
    operands, alias_operands, aliases = [q, k, k, v, v], [], {}
    seq_of = lambda b, s, p: (b * steps + s) * parts + p
    for task in sample_tasks:
        n_seq, q_rows, _ = task["q"].shape
        assert n_seq == batch * steps * parts
        wb = task["buf"].shape[4]
        n_h, h0 = task["n_heads"], task["first_head"]
        cols = n_h * HEAD_DIM
        col_block = (task["group"] * D_GROUP + h0 * HEAD_DIM) // cols
        buf_spec = pl.BlockSpec((1, 2, n_h, HEAD_DIM, wb), lambda b, s, p, hb=h0 // n_h: (seq_of(b, s, p), 0, hb, 0, 0))
        new_spec = pl.BlockSpec((cols, LANES), lambda b, s, p, cb=col_block: (cb, 0))
        in_specs += [pl.BlockSpec((1, q_rows, cols), lambda b, s, p, cb=col_block: (seq_of(b, s, p), 0, cb)),
                     new_spec, new_spec, buf_spec]
        operands += [task["q"], task["knt"], task["vnt"], task["buf"]]
        if task.get("nbuf") is not None:
            aliases[len(alias_operands)] = len(out_shape) + 2
            alias_operands.append(task["nbuf"])
        out_specs += [pl.BlockSpec((1, q_rows, cols), lambda b, s, p: (seq_of(b, s, p), 0, 0)),
                      pl.BlockSpec((1, q_rows, LANES), lambda b, s, p: (seq_of(b, s, p), 0, 0)), buf_spec]
        out_shape += [jax.ShapeDtypeStruct((n_seq, q_rows, cols), F32),
                      jax.ShapeDtypeStruct((n_seq, q_rows, LANES), F32),
                      jax.ShapeDtypeStruct(task["buf"].shape, F32)]
    n_main = len(operands)
    res = pl.pallas_call(
        functools.partial(_attn_kernel, dil=dil, n_alias=len(alias_operands), t_new=t_new,
                          tasks=tuple((GROUPS[t["group"]][1], t["first_head"]) for t in sample_tasks)),
        grid=(batch, steps, parts),
        in_specs=in_specs + [pl.BlockSpec(memory_space=pl.ANY)] * len(alias_operands),
        out_specs=out_specs,
        out_shape=out_shape,
        input_output_aliases={n_main + i: o for i, o in aliases.items()},
        compiler_params=pltpu.CompilerParams(
            dimension_semantics=("arbitrary", "arbitrary", "arbitrary"), vmem_limit_bytes=VMEM_LIMIT),
        name=f"attn_g{gi}",
    )(*operands, *alias_operands)
    task_res = [tuple(res[2 + 3 * i:5 + 3 * i]) for i in range(len(sample_tasks))]
    return res[0].reshape(n_tok, D_GROUP), res[1].reshape(n_tok, LANES), task_res


def _sample_attention(q_ref, knt_ref, vnt_ref, buf_ref, o_ref, lse_ref, nbuf_ref, seq_index, *,
                      dil, first_head, t_new):
    assert dil & (dil - 1) == 0
    q_rows = q_ref.shape[1]
    n_heads, wb = buf_ref.shape[2], buf_ref.shape[4]
    width = wb + LANES
    shift = lax.rem(LANES - t_new * seq_index, LANES)
    knt = pltpu.roll(knt_ref[...], shift, 1)
    vnt = pltpu.roll(vnt_ref[...], shift, 1)
    q = q_ref[0]

    t_row = lax.broadcasted_iota(jnp.int32, (q_rows, width), 0)
    p_col = lax.broadcasted_iota(jnp.int32, (q_rows, width), 1)
    back = wb + t_row - p_col
    valid = jnp.logical_and(jnp.logical_and(back >= 0, (back & (dil - 1)) == 0),
                            jnp.logical_and(back <= (N_KEYS - 1) * dil, p_col < wb + t_new))
    lane = lax.broadcasted_iota(jnp.int32, (q_rows, LANES), 1)
    lse_tile = jnp.zeros((q_rows, LANES), F32)
    o_heads = []
    for h in range(n_heads):
        rows = slice(h * HEAD_DIM, (h + 1) * HEAD_DIM)
        kx = jnp.concatenate([buf_ref[0, 0, h], knt[rows, :]], axis=1)
        vx = jnp.concatenate([buf_ref[0, 1, h], vnt[rows, :]], axis=1)
        nbuf_ref[0, 0, h] = pltpu.roll(kx, width - t_new, 1)[:, :wb]
        nbuf_ref[0, 1, h] = pltpu.roll(vx, width - t_new, 1)[:, :wb]
        s = jnp.where(valid, _dot(q[:, rows].astype(BF16), kx.astype(BF16)), NEG)
        m = jnp.max(s, axis=1, keepdims=True)
        e = jnp.exp2(s - m)
        den = jnp.sum(e, axis=1, keepdims=True)
        o_heads.append(_dot_nt(e.astype(BF16), vx.astype(BF16)) / den)
        lse_tile = jnp.where(lane == first_head + h, m * LN2 + jnp.log(den), lse_tile)
    o_ref[0] = jnp.concatenate(o_heads, axis=1)
    lse_ref[0] = lse_tile


def _merge_kernel(x_ref, o0_ref, o1_ref, o2_ref, l0_ref, l1_ref, l2_ref, pe_ref,
                  gmix_ref, wuv_ref, wg_ref, gv_ref, ws_ref, bias_ref, woa_ref, wob_ref, wo_ref,
                  gffn_ref, wff1_ref, wff2_ref, gple_ref, wpg_ref, wpp_ref,
                  *rest, prompt, tm, n_tiles, sample):
    if sample is None:
        (y_ref, va_ref), scratch = rest[:2], rest[2:]
    else:
        task_ins, (y_ref, va_ref), task_outs, scratch = rest[:4], rest[4:6], rest[6:9], rest[9:]
    d_model = x_ref.shape[1]
    d_a = gv_ref.shape[1]
    c_g = d_a // G_A
    x = x_ref[...]
    h = _rms(x, gmix_ref[...]).astype(BF16)

    u = jax.nn.gelu(_dot(h, wuv_ref[:, 0:d_a]))
    va = _rms(jax.nn.gelu(_dot(h, wuv_ref[:, d_a:2 * d_a])), gv_ref[...])
    if prompt:
        @pl.when(pl.program_id(1) == n_tiles - 1)
        def _():
            va_ref[0] = va[tm - CHUNK:tm, :]
    else:
        va_ref[...] = va
    va16 = va.astype(BF16)
    r_i = lax.broadcasted_iota(jnp.int32, (CHUNK, CHUNK), 0)
    c_i = lax.broadcasted_iota(jnp.int32, (CHUNK, CHUNK), 1)
    w_tril = [jnp.where(r_i >= c_i, ws_ref[g], jnp.zeros((CHUNK, CHUNK), BF16)) for g in range(G_A)]
    mix_rows = []
    for ci in range(tm // CHUNK):
        rs = slice(ci * CHUNK, (ci + 1) * CHUNK)
        cols = [_dot(w_tril[g], va16[rs, g * c_g:(g + 1) * c_g]) for g in range(G_A)]
        mix_rows.append(jnp.concatenate(cols, axis=1) + bias_ref[...])
    a = u * jnp.concatenate(mix_rows, axis=0)

    o_refs, l_refs = (o0_ref, o1_ref, o2_ref), (l0_ref, l1_ref, l2_ref)
    o_slabs, lses = [], []
    for gi, (_, dil) in enumerate(GROUPS):
        if not prompt or dil == 1:
            o_slabs.append([o_refs[gi][:, cc * LANES:(cc + 1) * LANES].astype(F32)
                            for cc in range(GROUP_SLABS)])
            lses.append(l_refs[gi][...])
            continue
        o_sc, l_sc = scratch[2 * (gi - 1)], scratch[2 * (gi - 1) + 1]
        piece = tm // dil
        for r in range(dil):
            rows = slice(r * piece, (r + 1) * piece)
            for cc in range(GROUP_SLABS):
                o_sc[cc, pl.ds(r, piece, stride=dil), :] = (
                    o_refs[gi][rows, cc * LANES:(cc + 1) * LANES].astype(F32))
            l_sc[pl.ds(r, piece, stride=dil), :] = l_refs[gi][rows, :]
        o_slabs.append([o_sc[cc] for cc in range(GROUP_SLABS)])
        lses.append(l_sc[...])

    m = jnp.maximum(jnp.maximum(lses[0], lses[1]), lses[2])
    ws = [jnp.exp(l - m) for l in lses]
    inv = 1.0 / (ws[0] + ws[1] + ws[2])
    ws = [w * inv for w in ws]
    lane = lax.broadcasted_iota(jnp.int32, (tm, LANES), 1)
    first_head = lane < HEAD_DIM
    b_cols = []
    for p in range(GROUP_SLABS):
        acc = jnp.zeros((tm, LANES), F32)
        for w, slabs in zip(ws, o_slabs):
            w_pair = jnp.where(first_head, w[:, 2 * p:2 * p + 1], w[:, 2 * p + 1:2 * p + 2])
            acc = acc + w_pair * slabs[p]
        b_cols.append(acc)
    b = jnp.concatenate(b_cols, axis=1)

    gate_a = jax.nn.sigmoid(_dot(h, wg_ref[:, 0:d_model]))
    gate_b = jax.nn.sigmoid(_dot(h, wg_ref[:, d_model:2 * d_model]))
    merged = gate_a * _dot(a.astype(BF16), woa_ref[...]) + gate_b * _dot(b.astype(BF16), wob_ref[...])
    x = x + _dot(merged.astype(BF16), wo_ref[...])

    if sample is not None:
        parts = H_G // task_ins[3].shape[2]
        step = pl.program_id(0) * n_tiles + pl.program_id(1)
        _sample_attention(*task_ins, *task_outs, step // parts, dil=sample[0],
                          first_head=(step % parts) * (H_G // parts), t_new=sample[1])

    h2 = _rms(x, gffn_ref[...]).astype(BF16)
    d_ff = wff1_ref.shape[1]
    ff_step = 1024
    ffn = jnp.zeros((tm, d_model), F32)
    for c0 in range(0, d_ff, ff_step):
        hid = jnp.square(jnp.maximum(_dot(h2, wff1_ref[:, c0:c0 + ff_step]), 0.0))
        ffn = ffn + _dot(hid.astype(BF16), wff2_ref[c0:c0 + ff_step, :])
    x = x + ffn

    gate = jax.nn.sigmoid(_dot(_rms(x, gple_ref[...]).astype(BF16), wpg_ref[...]))
    y_ref[...] = x + gate * _dot(pe_ref[...].astype(BF16), wpp_ref[...])


def _merge_call(x2, outs, lses, pe2, wts, ws_mat, bias, *, prompt, batch, tm, sample_task=None, t_new=None):
    n_tok, d_model = x2.shape
    seq = n_tok // batch
    n_tiles = seq // tm
    d_a = wts["g_v"].shape[1]
    tile_index = lambda b, i: b * n_tiles + i
    tok = lambda b, i: (tile_index(b, i), 0)
    row_spec = lambda width: pl.BlockSpec((tm, width), tok)

    names = ("g_mix", "w_uv", "w_gates", "g_v", "ws", "bias", "w_oa", "w_ob", "w_o",
             "g_ffn", "w_ff1", "w_ff2", "g_ple", "w_ple_gate", "w_ple_proj")
    consts = dict(wts, ws=ws_mat, bias=bias)
    in_specs = ([row_spec(d_model)]
                + [row_spec(D_GROUP)] * N_GROUPS + [row_spec(LANES)] * N_GROUPS
                + [row_spec(pe2.shape[1])] + [_resident(consts[k].shape) for k in names])
    scratch = []
    if prompt:
        va_shape = jax.ShapeDtypeStruct((batch, CHUNK, d_a), F32)
        va_spec = pl.BlockSpec((1, CHUNK, d_a), lambda b, i: (b, 0, 0))
        for _, dil in GROUPS:
            if dil > 1:
                scratch += [pltpu.VMEM((GROUP_SLABS, tm, LANES), F32), pltpu.VMEM((tm, LANES), F32)]
    else:
        va_shape = jax.ShapeDtypeStruct((n_tok, d_a), F32)
        va_spec = row_spec(d_a)
    operands = [x2, *outs, *lses, pe2, *[consts[k] for k in names]]
    out_specs = [row_spec(d_model), va_spec]
    out_shape = [jax.ShapeDtypeStruct((n_tok, d_model), F32), va_shape]
    sample = None
    if sample_task is not None:
        task = sample_task
        n_seq, q_rows, _ = task["q"].shape
        n_h, wb = task["n_heads"], task["buf"].shape[4]
        parts = H_G // n_h
        assert n_seq * parts == batch * n_tiles
        cols = n_h * HEAD_DIM
        col0 = task["group"] * D_GROUP // cols
        seq_of = lambda b, i: tile_index(b, i) // parts
        part_of = lambda b, i: tile_index(b, i) % parts
        buf_spec = pl.BlockSpec((1, 2, n_h, HEAD_DIM, wb), lambda b, i: (seq_of(b, i), 0, part_of(b, i), 0, 0))
        new_spec = pl.BlockSpec((cols, LANES), lambda b, i: (col0 + part_of(b, i), 0))
        in_specs += [pl.BlockSpec((1, q_rows, cols), lambda b, i: (seq_of(b, i), 0, col0 + part_of(b, i))),
                     new_spec, new_spec, buf_spec]
        operands += [task["q"], task["knt"], task["vnt"], task["buf"]]
        out_specs += [pl.BlockSpec((1, q_rows, cols), lambda b, i: (seq_of(b, i), 0, part_of(b, i))),
                      pl.BlockSpec((1, None, q_rows, LANES), lambda b, i: (seq_of(b, i), part_of(b, i), 0, 0)),
                      buf_spec]
        out_shape += [jax.ShapeDtypeStruct((n_seq, q_rows, D_GROUP), F32),
                      jax.ShapeDtypeStruct((n_seq, parts, q_rows, LANES), F32),
                      jax.ShapeDtypeStruct(task["buf"].shape, F32)]
        sample = (GROUPS[task["group"]][1], t_new)
    return pl.pallas_call(
        functools.partial(_merge_kernel, prompt=prompt, tm=tm, n_tiles=n_tiles, sample=sample),
        grid=(batch, n_tiles),
        in_specs=in_specs,
        out_specs=out_specs,
        out_shape=out_shape,
        scratch_shapes=scratch,
        compiler_params=pltpu.CompilerParams(
            dimension_semantics=("arbitrary", "arbitrary"), vmem_limit_bytes=VMEM_LIMIT),
        name="merge_prompt" if prompt else "merge_sample",
    )(*operands)


def kernel(x_prompt, x_sample, p_prompt, p_sample, cache_kv_w128, cache_kv_w512, cache_kv_w2048,
           g_mix, w_in, g_v, w_s, b_s, g_q, g_k, w_oa, w_ob, w_o,
           g_ffn, w_ff1, w_ff2, g_ple, w_ple_gate, w_ple_proj):
    depth = w_in.shape[0]
    batch, seq, d_model = x_prompt.shape
    dec_batch, t_new, _ = x_sample.shape
    d_a = g_v.shape[1]
    n_sample = dec_batch * t_new
    assert n_sample == CHUNK and w_in.shape[2] == 2 * d_a + 3 * D_QKV + 2 * d_model
    caches = (cache_kv_w128, cache_kv_w512, cache_kv_w2048)

    xp = x_prompt.reshape(batch * seq, d_model)
    xs = x_sample.reshape(n_sample, d_model)
    pos_p = np.arange(seq, dtype=np.int32)
    pos_s = PAST_LEN + np.tile(np.arange(t_new, dtype=np.int32), dec_batch)

    kv_p = [[] for _ in GROUPS]
    kv_s = [[] for _ in GROUPS]
    cv_p, cv_s = [], []
    for l in range(depth):
        wts = {
            "g_mix": g_mix[l][None, :], "w_uv": w_in[l][:, 0:2 * d_a].astype(BF16),
            "w_gates": w_in[l][:, 2 * d_a + 3 * D_QKV:].astype(BF16), "g_v": g_v[l][None, :],
            "w_oa": w_oa[l].astype(BF16), "w_ob": w_ob[l].astype(BF16), "w_o": w_o[l].astype(BF16),
            "g_ffn": g_ffn[l][None, :], "w_ff1": w_ff1[l].astype(BF16), "w_ff2": w_ff2[l].astype(BF16),
            "g_ple": g_ple[l][None, :], "w_ple_gate": w_ple_gate[l].astype(BF16),
            "w_ple_proj": w_ple_proj[l].astype(BF16),
        }
        w_qkv = w_in[l][:, 2 * d_a:2 * d_a + 3 * D_QKV].astype(BF16)
        c_g = d_a // G_A
        ws_p = w_s[l].astype(BF16)
        bias_p = jnp.repeat(b_s[l].T, c_g, axis=1)
        tok = np.arange(CHUNK)
        same_seq = jnp.asarray(tok[:, None] // t_new == tok[None, :] // t_new)
        reps = CHUNK // t_new
        ws_s = jnp.where(same_seq, jnp.tile(w_s[l][:, :t_new, :t_new], (1, reps, reps)), 0.0).astype(BF16)
        bias_s = jnp.tile(jnp.repeat(b_s[l][:, :t_new].T, c_g, axis=1), (reps, 1))

        tm = TOKEN_TILE
        res = _qkv_call(xp, pos_p, g_mix[l], w_qkv, g_q[l], g_k[l], prompt=True, batch=batch, tm=tm)
        q_g, k_g, v_g, kv_tails = res[0:3], res[3:6], res[6:9], res[9:12]
        qs, knt, vnt = _qkv_call(
            xs, pos_s, g_mix[l], w_qkv, g_q[l], g_k[l], prompt=False, batch=1, tm=n_sample)

        q_rows = 8
        assert LANES % t_new == 0 and t_new <= q_rows and N_GROUPS == 3
        q_pad = jnp.pad(qs.reshape(dec_batch, t_new, D_QKV), ((0, 0), (0, q_rows - t_new), (0, 0)))
        bufs = [jnp.transpose(c[l], (0, 1, 3, 4, 2)) for c in caches]
        task = lambda g, n_h: dict(group=g, first_head=0, n_heads=n_h, q=q_pad, knt=knt, vnt=vnt, buf=bufs[g])
        o2, l2, _ = _attn_call(q_g[2], k_g[2], v_g[2], 2, batch, seq, [], t_new)
        o1, l1, ((so1, sl1, nbuf1),) = _attn_call(q_g[1], k_g[1], v_g[1], 1, batch, seq, [task(1, H_G)], t_new)
        o0, l0, ((so0, sl0, nbuf0),) = _attn_call(q_g[0], k_g[0], v_g[0], 0, batch, seq, [task(0, H_G)], t_new)

        xp, va_p, so2, sl2, nbuf2 = _merge_call(
            xp, (o0, o1, o2), (l0, l1, l2), p_prompt[l].reshape(batch * seq, -1), wts, ws_p, bias_p,
            prompt=True, batch=batch, tm=tm, sample_task=task(2, H_G // 2), t_new=t_new)
        s_outs = (so0, so1, so2)
        s_lses = (sl0, sl1, jnp.sum(sl2, axis=1))
        for gi, nbuf in enumerate((nbuf0, nbuf1, nbuf2)):
            kv_s[gi].append(jnp.transpose(nbuf, (0, 1, 4, 2, 3)))
        for gi, st in enumerate(kv_tails):
            kv_p[gi].append(jnp.transpose(st, (0, 1, 4, 2, 3)))
        cv_p.append(va_p)

        s_outs = [o[:, :t_new].reshape(n_sample, D_GROUP) for o in s_outs]
        s_lses = [s[:, :t_new].reshape(n_sample, LANES) for s in s_lses]
        xs, va_s = _merge_call(xs, s_outs, s_lses, p_sample[l].reshape(n_sample, -1), wts, ws_s, bias_s,
                               prompt=False, batch=1, tm=n_sample)
        cv_s.append(va_s.reshape(dec_batch, t_new, d_a))

    return (xp.reshape(batch, seq, d_model), xs.reshape(dec_batch, t_new, d_model),
            jnp.stack(kv_p[0]), jnp.stack(kv_p[1]), jnp.stack(kv_p[2]),
            jnp.stack(kv_s[0]), jnp.stack(kv_s[1]), jnp.stack(kv_s[2]),
            jnp.stack(cv_p), jnp.stack(cv_s))
```

```python
import functools
import math

import jax
import jax.numpy as jnp
import numpy as np
from jax import lax
from jax.experimental import pallas as pl
from jax.experimental.pallas import tpu as pltpu

F32 = jnp.float32
BF16 = jnp.bfloat16

LANES = 128
BF16_ROWS = 16
MXU_DIM = 256
TOKEN_TILE = 256
HEAD_DIM = 64
H_G = 8
HEADS_PER_VREG = LANES // HEAD_DIM
GROUPS = ((128, 1), (512, 4), (2048, 16))
N_GROUPS = len(GROUPS)
D_GROUP = H_G * HEAD_DIM
GROUP_SLABS = D_GROUP // LANES
D_QKV = N_GROUPS * D_GROUP
N_KEYS = 129
SUB_BLOCK = 128
ROT_DIM = HEAD_DIM // 4
ROT_HALF = ROT_DIM // 2
ROPE_THETA = 500000.0
CHUNK = 128
G_A = 4
PAST_LEN = 8192
ATTN_SUB_BLOCKS = 4
EPS = 1e-6
NEG = -1e30
LN2 = math.log(2.0)
LOG2E = 1.0 / LN2
VMEM_LIMIT = 62 * 1024 * 1024


def _rms(x, g):
    return x * lax.rsqrt(jnp.mean(x * x, axis=-1, keepdims=True) + EPS) * g


def _resident(shape):
    zeros = (0,) * len(shape)
    return pl.BlockSpec(shape, lambda *_: zeros, pipeline_mode=pl.Buffered(1))


def _dot(a, b):
    return jnp.dot(a, b, preferred_element_type=F32)


def _dot_nt(a, b):
    return lax.dot_general(a, b, (((1,), (1,)), ((), ())), preferred_element_type=F32)


def _project_qkv(x_ref, gmix_ref, gq_ref, gk_ref, rot_ref, bd_ref, w_refs, perm_refs, emit):
    h = _rms(x_ref[...], gmix_ref[...]).astype(BF16)
    hs = [h if p_ref is None else _dot(p_ref[...], h).astype(BF16) for p_ref in perm_refs]
    bd = bd_ref[...]
    wide = bd.shape[0]

    for ti, g_ref in enumerate((gq_ref, gk_ref, None)):
        for gi in range(N_GROUPS):
            z = _dot(hs[gi], w_refs[ti * N_GROUPS + gi][...])
            if g_ref is None:
                emit(ti, gi, [z[:, c * LANES:(c + 1) * LANES] for c in range(GROUP_SLABS)])
                continue
            t0 = 3 * LANES * min(gi, rot_ref.shape[1] // (3 * LANES) - 1)
            rc, ru, rd = (rot_ref[:, t0 + k * LANES:t0 + (k + 1) * LANES] for k in range(3))

            def rot(zn):
                return zn * rc + pltpu.roll(zn, LANES - ROT_HALF, 1) * ru + pltpu.roll(zn, ROT_HALF, 1) * rd

            vals = []
            for c in range(0, D_GROUP, wide):
                zc = z[:, c:c + wide]
                ss = _dot((zc * zc).astype(BF16), bd)
                zn = zc * lax.rsqrt(ss * (1.0 / HEAD_DIM) + EPS)
                vals += [rot(zn[:, j:j + LANES] * g_ref[...]) for j in range(0, wide, LANES)]
            emit(ti, gi, vals)


def _qkv_prompt_kernel(*refs, tm, n_tiles):
    n_perm = sum(dil > 1 for _, dil in GROUPS)
    n_w = 3 * N_GROUPS
    ins, w_refs, perms, rest = refs[:6], refs[6:6 + n_w], refs[6 + n_w:6 + n_w + n_perm], refs[6 + n_w + n_perm:]
    outs, kv_refs = rest[:3 * N_GROUPS], rest[3 * N_GROUPS:4 * N_GROUPS]
    kv_slabs, order_slab = rest[4 * N_GROUPS:4 * N_GROUPS + 2], rest[4 * N_GROUPS + 2]
    perms = list(perms)
    perm_refs = [perms.pop(0) if dil > 1 else None for _, dil in GROUPS]

    def emit(ti, gi, vals):
        out = outs[ti * N_GROUPS + gi]
        for cc, val in enumerate(vals):
            out[:, cc * LANES:(cc + 1) * LANES] = val.astype(BF16)
            if ti > 0:
                kv_slabs[ti - 1][gi * GROUP_SLABS + cc] = val

    _project_qkv(*ins, w_refs, perm_refs, emit)

    i = pl.program_id(1)
    seq = n_tiles * tm
    for gi, (win, dil) in enumerate(GROUPS):
        rows = min(win, tm)
        first = (seq - win) // tm if win >= tm else n_tiles - 1
        piece = tm // dil

        @pl.when(i >= first)
        def _(gi=gi, dil=dil, rows=rows, piece=piece):
            for kv in range(2):
                for cc in range(GROUP_SLABS):
                    c = gi * GROUP_SLABS + cc
                    if dil > 1:
                        for r in range(dil):
                            order_slab[pl.ds(r, piece, stride=dil), :] = kv_slabs[kv][c, r * piece:(r + 1) * piece, :]
                        val = order_slab[...]
                    else:
                        val = kv_slabs[kv][c]
                    t = val[tm - rows:tm].T
                    for hh in range(HEADS_PER_VREG):
                        kv_refs[gi][0, kv, cc * HEADS_PER_VREG + hh] = t[hh * HEAD_DIM:(hh + 1) * HEAD_DIM, :]


def _qkv_sample_kernel(*refs):
    n_w = 3 * N_GROUPS
    ins, w_refs, outs = refs[:6], refs[6:6 + n_w], refs[6 + n_w:9 + n_w]

    def emit(ti, gi, vals):
        for cc, val in enumerate(vals):
            c = gi * GROUP_SLABS + cc
            if ti == 0:
                outs[ti][:, c * LANES:(c + 1) * LANES] = val
            else:
                outs[ti][c * LANES:(c + 1) * LANES, :] = val.T

    _project_qkv(*ins, w_refs, [None] * N_GROUPS, emit)


def _residue_major_source(dil, tm):
    row = np.arange(tm, dtype=np.int32)
    piece = tm // dil
    return (row % piece) * dil + row // piece


def _rotary_tables(pos):
    inv = np.float32(ROPE_THETA) ** (-np.arange(ROT_HALF, dtype=np.float32) * np.float32(2.0 / ROT_DIM))
    ang = pos.astype(np.float32)[:, None] * inv[None, :]
    cos, sin = np.cos(ang), np.sin(ang)
    n = pos.shape[0]
    pad = np.zeros((n, HEAD_DIM - ROT_DIM), np.float32)
    zero = np.zeros((n, ROT_HALF), np.float32)
    rc = np.concatenate([cos, cos, pad + 1.0], axis=1)
    ru = np.concatenate([-sin, zero, pad], axis=1)
    rd = np.concatenate([zero, sin, pad], axis=1)
    return np.concatenate([np.tile(t, (1, HEADS_PER_VREG)) for t in (rc, ru, rd)], axis=1).astype(np.float32)


def _column_block(n_rows, width, index):
    return pl.BlockSpec((n_rows, width), lambda *_: (0, index), pipeline_mode=pl.Buffered(1))


def _qkv_call(x2, pos, g_mix, w_in, qkv_col0, g_q, g_k, *, prompt, batch, tm):
    n_tok, d_model = x2.shape
    seq = n_tok // batch
    n_tiles = seq // tm
    perms = []
    if prompt:
        assert all(tm % (dil * BF16_ROWS) == 0 for _, dil in GROUPS)
        tile_start = (np.arange(seq, dtype=np.int32) // tm) * tm
        tables = []
        for _, dil in GROUPS:
            src = _residue_major_source(dil, tm)
            tables.append(_rotary_tables(pos[tile_start + np.tile(src, n_tiles)]))
            if dil > 1:
                perms.append(jnp.asarray(src[:, None] == np.arange(tm)[None, :], dtype=BF16))
        rot = jnp.asarray(np.concatenate(tables, axis=1))
    else:
        rot = jnp.asarray(_rotary_tables(pos))
    gq = jnp.tile(g_q.astype(F32), HEADS_PER_VREG)[None, :] * (LOG2E * HEAD_DIM ** -0.5)
    gk = jnp.tile(g_k.astype(F32), HEADS_PER_VREG)[None, :]
    lane = np.arange(MXU_DIM)
    bd = jnp.asarray(lane[:, None] // HEAD_DIM == lane[None, :] // HEAD_DIM, dtype=BF16)

    tile_index = lambda b, i: b * n_tiles + i
    tok = lambda b, i: (tile_index(b, i), 0)
    in_specs = [
        pl.BlockSpec((tm, d_model), tok),
        _resident((1, d_model)),
        _resident((1, LANES)),
        _resident((1, LANES)),
        pl.BlockSpec((tm, rot.shape[1]), lambda b, i: (i, 0)),
        _resident((MXU_DIM, MXU_DIM)),
    ] + [_column_block(d_model, D_GROUP, qkv_col0 // D_GROUP + j) for j in range(3 * N_GROUPS)]
    in_specs += [_resident(p.shape) for p in perms]
    assert qkv_col0 % D_GROUP == 0
    if prompt:
        out_shape = [jax.ShapeDtypeStruct((n_tok, D_GROUP), BF16)] * (3 * N_GROUPS)
        out_specs = [pl.BlockSpec((tm, D_GROUP), tok)] * (3 * N_GROUPS)
        for win, _ in GROUPS:
            rows = min(win, tm)
            first = (seq - win) // tm if win >= tm else n_tiles - 1
            out_shape.append(jax.ShapeDtypeStruct((batch, 2, H_G, HEAD_DIM, win), F32))
            out_specs.append(pl.BlockSpec(
                (1, 2, H_G, HEAD_DIM, rows),
                lambda b, i, first=first: (b, 0, 0, 0, jnp.maximum(i - first, 0))))
        scratch = [pltpu.VMEM((D_QKV // LANES, tm, LANES), F32)] * 2 + [pltpu.VMEM((tm, LANES), F32)]
        body = functools.partial(_qkv_prompt_kernel, tm=tm, n_tiles=n_tiles)
    else:
        assert n_tiles == 1 and batch == 1 and tm == LANES
        out_shape = [jax.ShapeDtypeStruct((n_tok, D_QKV), F32)] + [jax.ShapeDtypeStruct((D_QKV, n_tok), F32)] * 2
        out_specs = [pl.BlockSpec((tm, D_QKV), tok)] + [pl.BlockSpec((D_QKV, tm), lambda b, i: (0, 0))] * 2
        scratch = []
        body = _qkv_sample_kernel

    return pl.pallas_call(
        body,
        grid=(batch, n_tiles),
        in_specs=in_specs,
        out_specs=out_specs,
        out_shape=out_shape,
        scratch_shapes=scratch,
        compiler_params=pltpu.CompilerParams(
            dimension_semantics=("arbitrary", "arbitrary"), vmem_limit_bytes=VMEM_LIMIT),
        name="qkv_prompt" if prompt else "qkv_sample",
    )(x2, g_mix[None, :], gq, gk, rot, bd, *[w_in] * (3 * N_GROUPS), *perms)


def _attn_kernel(q_ref, kc_ref, kp_ref, vc_ref, vp_ref, *rest, dil, tasks, n_alias, t_new):
    n_in = 4 * len(tasks)
    n_skip = n_in + n_alias
    task_ins, (o_ref, lse_ref), task_outs = rest[:n_in], rest[n_skip:n_skip + 2], rest[n_skip + 2:]
    step = (pl.program_id(0) * pl.num_programs(1) + pl.program_id(1)) * pl.num_programs(2) + pl.program_id(2)

    has_prev = pl.program_id(1) > 0
    piece = TOKEN_TILE // dil

    def residue_rows(i):
        return [(t, 0, slice(i * piece, (i + 1) * piece)) for t in range(SUB_BLOCK // piece)]

    def load(ref, rows, sl):
        parts = [ref[rs + (sl,)] for rs in rows]
        return parts[0] if len(parts) == 1 else jnp.concatenate(parts, axis=0)

    def store(ref, rows, sl, val):
        n = val.shape[0] // len(rows)
        for t, rs in enumerate(rows):
            ref[rs + (sl,)] = val[t * n:(t + 1) * n]
    key = lax.broadcasted_iota(jnp.int32, (2 * SUB_BLOCK, HEADS_PER_VREG * SUB_BLOCK), 0)
    qry = lax.broadcasted_iota(jnp.int32, (2 * SUB_BLOCK, HEADS_PER_VREG * SUB_BLOCK), 1) % SUB_BLOCK
    vis_cur = key <= qry
    vis_prev = key - SUB_BLOCK >= qry
    lane16 = lax.broadcasted_iota(jnp.int32, (SUB_BLOCK, LANES), 1).astype(F32).astype(BF16)
    head_lanes = (lane16 < HEAD_DIM, lane16 >= HEAD_DIM)
    eye_r = lax.broadcasted_iota(jnp.int32, (LANES, LANES), 0)
    eye_c = lax.broadcasted_iota(jnp.int32, (LANES, LANES), 1)
    eye = (eye_r == eye_c).astype(F32).astype(BF16)
    ones_rows = jnp.ones((BF16_ROWS, 2 * SUB_BLOCK), BF16)
    head_row = lax.broadcasted_iota(jnp.int32, (H_G, SUB_BLOCK), 0)

    pad = jnp.zeros((LANES - H_G, SUB_BLOCK), F32)
    slabs = [slice(p * LANES, (p + 1) * LANES) for p in range(GROUP_SLABS)]

    if dil == 1:
        blocks = [([(slice(0, SUB_BLOCK),)], kp_ref, vp_ref, [(slice(0, SUB_BLOCK),)], has_prev)]
        blocks += [([(slice(i * SUB_BLOCK, (i + 1) * SUB_BLOCK),)], kc_ref, vc_ref,
                    [(slice((i - 1) * SUB_BLOCK, i * SUB_BLOCK),)], True) for i in range(1, ATTN_SUB_BLOCKS)]
    else:
        blocks = [(residue_rows(i), kp_ref, vp_ref, residue_rows(i), has_prev)
                  for i in range(ATTN_SUB_BLOCKS)]

    scores, vts = [], []
    for rows, kp_src, vp_src, prev_rows, _ in blocks:
        for sl in slabs:
            q = load(q_ref, rows, sl)
            qq = jnp.concatenate([jnp.where(hl, q, jnp.zeros_like(q)) for hl in head_lanes], axis=0)
            kk = jnp.concatenate([load(kc_ref, rows, sl), load(kp_src, prev_rows, sl)], axis=0)
            vv = jnp.concatenate([load(vc_ref, rows, sl), load(vp_src, prev_rows, sl)], axis=0)
            scores.append(_dot_nt(kk, qq))
            vts.append(jnp.concatenate([_dot_nt(eye, vv).astype(BF16), ones_rows], axis=0))
    for ti, (task_dil, first_head) in enumerate(tasks):
        _sample_attention(*task_ins[4 * ti:4 * ti + 4], *task_outs[3 * ti:3 * ti + 3], step,
                          dil=task_dil, first_head=first_head, t_new=t_new)
    maxes, probs = [], []
    for bi, (_, _, _, _, has_prev) in enumerate(blocks):
        visible = jnp.logical_or(vis_cur, jnp.logical_and(vis_prev, has_prev))
        for s in scores[bi * GROUP_SLABS:(bi + 1) * GROUP_SLABS]:
            s = jnp.where(visible, s, NEG)
            m = jnp.max(s, axis=0, keepdims=True)
            maxes.append(m)
            probs.append(jnp.exp2(s - m).astype(BF16))
    outs = [_dot(vt, e) for vt, e in zip(vts, probs)]
    for bi, (rows, _, _, _, _) in enumerate(blocks):
        lse_rows = jnp.zeros((H_G, SUB_BLOCK), F32)
        for p, sl in enumerate(slabs):
            ot, m = outs[bi * GROUP_SLABS + p], maxes[bi * GROUP_SLABS + p]
            den = ot[LANES:LANES + 1, :]
            lse = m * LN2 + jnp.log(den)
            inv = 1.0 / den
            o_halves = []
            for half in range(HEADS_PER_VREG):
                qs = slice(half * SUB_BLOCK, (half + 1) * SUB_BLOCK)
                o_halves.append(ot[half * HEAD_DIM:(half + 1) * HEAD_DIM, qs] * inv[:, qs])
                lse_rows = jnp.where(head_row == p * HEADS_PER_VREG + half, lse[:, qs], lse_rows)
            store(o_ref, rows, sl, jnp.concatenate(o_halves, axis=0).T.astype(BF16))
        store(lse_ref, rows, slice(None), jnp.concatenate([lse_rows, pad], axis=0).T)


def _attn_call(q, k, v, gi, batch, seq, sample_tasks, t_new):
    dil = GROUPS[gi][1]
    n_tok = batch * seq
    if dil == 1:
        rows, parts = ATTN_SUB_BLOCKS * SUB_BLOCK, 1
        assert seq % rows == 0
        steps = seq // rows
        view = lambda width: (n_tok, width)
        cur_map = lambda b, s, p: (b * steps + s, 0)
        cur = lambda width: pl.BlockSpec((rows, width), cur_map)
        prev = pl.BlockSpec((SUB_BLOCK, D_GROUP),
                            lambda b, s, p: (jnp.maximum((b * steps + s) * ATTN_SUB_BLOCKS - 1, 0), 0))
    else:
        span, parts = SUB_BLOCK * dil, dil // ATTN_SUB_BLOCKS
        assert dil % ATTN_SUB_BLOCKS == 0 and span % TOKEN_TILE == 0 and seq % span == 0
        assert TOKEN_TILE % (dil * BF16_ROWS) == 0
        steps, span_tiles, part_rows = seq // span, span // TOKEN_TILE, TOKEN_TILE // parts
        view = lambda width: (n_tok // TOKEN_TILE, parts, part_rows, width)
        cur_map = lambda b, s, p: (b * steps + s, p, 0, 0)
        cur = lambda width: pl.BlockSpec((span_tiles, 1, part_rows, width), cur_map)
        prev = pl.BlockSpec((span_tiles, 1, part_rows, D_GROUP),
                            lambda b, s, p: (b * steps + jnp.maximum(s - 1, 0), p, 0, 0))
    q, k, v = (t.reshape(view(D_GROUP)) for t in (q, k, v))
    in_specs = [cur(D_GROUP), cur(D_GROUP), prev, cur(D_GROUP), prev]
    out_specs = [cur(D_GROUP), cur(LANES)]
    out_shape = [jax.ShapeDtypeStruct(view(D_GROUP), BF16), jax.ShapeDtypeStruct(view(LANES), F32)]
    operands, alias_operands, aliases = [q, k, k, v, v], [], {}
    seq_of = lambda b, s, p: (b * steps + s) * parts + p
    for task in sample_tasks:
        n_seq, q_rows, _ = task["q"].shape
        assert n_seq == batch * steps * parts
        wb = task["buf"].shape[4]
        n_h, h0 = task["n_heads"], task["first_head"]
        cols = n_h * HEAD_DIM
        col_block = (task["group"] * D_GROUP + h0 * HEAD_DIM) // cols
        buf_spec = pl.BlockSpec((1, 2, n_h, HEAD_DIM, wb), lambda b, s, p, hb=h0 // n_h: (seq_of(b, s, p), 0, hb, 0, 0))
        new_spec = pl.BlockSpec((cols, LANES), lambda b, s, p, cb=col_block: (cb, 0))
        in_specs += [pl.BlockSpec((1, q_rows, cols), lambda b, s, p, cb=col_block: (seq_of(b, s, p), 0, cb)),
                     new_spec, new_spec, buf_spec]
        operands += [task["q"], task["knt"], task["vnt"], task["buf"]]
        if task.get("nbuf") is not None:
            aliases[len(alias_operands)] = len(out_shape) + 2
            alias_operands.append(task["nbuf"])
        out_specs += [pl.BlockSpec((1, q_rows, cols), lambda b, s, p: (seq_of(b, s, p), 0, 0)),
                      pl.BlockSpec((1, q_rows, LANES), lambda b, s, p: (seq_of(b, s, p), 0, 0)), buf_spec]
        out_shape += [jax.ShapeDtypeStruct((n_seq, q_rows, cols), F32),
                      jax.ShapeDtypeStruct((n_seq, q_rows, LANES), F32),
                      jax.ShapeDtypeStruct(task["buf"].shape, F32)]
    n_main = len(operands)
    res = pl.pallas_call(
        functools.partial(_attn_kernel, dil=dil, n_alias=len(alias_operands), t_new=t_new,
                          tasks=tuple((GROUPS[t["group"]][1], t["first_head"]) for t in sample_tasks)),
        grid=(batch, steps, parts),
        in_specs=in_specs + [pl.BlockSpec(memory_space=pl.ANY)] * len(alias_operands),
        out_specs=out_specs,
        out_shape=out_shape,
        input_output_aliases={n_main + i: o for i, o in aliases.items()},
        compiler_params=pltpu.CompilerParams(
            dimension_semantics=("arbitrary", "arbitrary", "arbitrary"), vmem_limit_bytes=VMEM_LIMIT),
        name=f"attn_g{gi}",
    )(*operands, *alias_operands)
    task_res = [tuple(res[2 + 3 * i:5 + 3 * i]) for i in range(len(sample_tasks))]
    return res[0].reshape(n_tok, D_GROUP), res[1].reshape(n_tok, LANES), task_res


def _sample_attention(q_ref, knt_ref, vnt_ref, buf_ref, o_ref, lse_ref, nbuf_ref, seq_index, *,
                      dil, first_head, t_new):
    assert dil & (dil - 1) == 0
    q_rows = q_ref.shape[1]
    n_heads, wb = buf_ref.shape[2], buf_ref.shape[4]
    width = wb + LANES
    shift = lax.rem(LANES - t_new * seq_index, LANES)
    knt = pltpu.roll(knt_ref[...], shift, 1)
    vnt = pltpu.roll(vnt_ref[...], shift, 1)
    q = q_ref[0]

    t_row = lax.broadcasted_iota(jnp.int32, (q_rows, width), 0)
    p_col = lax.broadcasted_iota(jnp.int32, (q_rows, width), 1)
    back = wb + t_row - p_col
    valid = jnp.logical_and(jnp.logical_and(back >= 0, (back & (dil - 1)) == 0),
                            jnp.logical_and(back <= (N_KEYS - 1) * dil, p_col < wb + t_new))
    valid = jnp.concatenate([valid] * n_heads, axis=0)
    lane = lax.broadcasted_iota(jnp.int32, (q_rows, LANES), 1)
    cols = n_heads * HEAD_DIM

    kx = jnp.concatenate([buf_ref[0, 0].reshape(cols, wb), knt], axis=1)
    vx = jnp.concatenate([buf_ref[0, 1].reshape(cols, wb), vnt], axis=1)
    nbuf_ref[0, 0] = pltpu.roll(kx, width - t_new, 1)[:, :wb].reshape(n_heads, HEAD_DIM, wb)
    nbuf_ref[0, 1] = pltpu.roll(vx, width - t_new, 1)[:, :wb].reshape(n_heads, HEAD_DIM, wb)

    q_all = jnp.concatenate([q] * n_heads, axis=0)
    r_head = lax.broadcasted_iota(jnp.int32, q_all.shape, 0) // q_rows
    c_head = lax.broadcasted_iota(jnp.int32, q_all.shape, 1) // HEAD_DIM
    q_all = jnp.where(r_head == c_head, q_all, 0.0).astype(BF16)
    s = jnp.where(valid, _dot(q_all, kx.astype(BF16)), NEG)
    m = jnp.max(s, axis=1, keepdims=True)
    e = jnp.exp2(s - m)
    den = jnp.sum(e, axis=1, keepdims=True)
    o_all = _dot_nt(e.astype(BF16), vx.astype(BF16)) / den
    lse = m * LN2 + jnp.log(den)
    o_ref[0] = jnp.concatenate(
        [o_all[h * q_rows:(h + 1) * q_rows, h * HEAD_DIM:(h + 1) * HEAD_DIM] for h in range(n_heads)], axis=1)
    lse_ref[0] = functools.reduce(jnp.add, [
        jnp.where(lane == first_head + h, lse[h * q_rows:(h + 1) * q_rows], 0.0) for h in range(n_heads)])


def _merge_kernel(x_ref, o0_ref, o1_ref, o2_ref, l0_ref, l1_ref, l2_ref, pe_ref,
                  gmix_ref, wu_ref, wva_ref, wga0_ref, wga1_ref, wgb0_ref, wgb1_ref,
                  gv_ref, ws_ref, bias_ref, woa_ref, wob_ref, wo_ref,
                  gffn_ref, wff1_ref, wff2_ref, gple_ref, wpg_ref, wpp_ref,
                  *rest, prompt, tm, n_tiles, sample):
    if sample is None:
        (y_ref, va_ref), scratch = rest[:2], rest[2:]
    else:
        task_ins, (y_ref, va_ref), task_outs, scratch = rest[:4], rest[4:6], rest[6:9], rest[9:]
    d_model = x_ref.shape[1]
    d_a = gv_ref.shape[1]
    c_g = d_a // G_A
    x = x_ref[...]
    h = _rms(x, gmix_ref[...]).astype(BF16)

    u = jax.nn.gelu(_dot(h, wu_ref[...]))
    va = _rms(jax.nn.gelu(_dot(h, wva_ref[...])), gv_ref[...])
    if prompt:
        @pl.when(pl.program_id(1) == n_tiles - 1)
        def _():
            va_ref[0] = va[tm - CHUNK:tm, :]
    else:
        va_ref[...] = va
    va16 = va.astype(BF16)
    r_i = lax.broadcasted_iota(jnp.int32, (CHUNK, CHUNK), 0)
    c_i = lax.broadcasted_iota(jnp.int32, (CHUNK, CHUNK), 1)
    w_tril = [jnp.where(r_i >= c_i, ws_ref[g], jnp.zeros((CHUNK, CHUNK), BF16)) for g in range(G_A)]
    mix_rows = []
    for ci in range(tm // CHUNK):
        rs = slice(ci * CHUNK, (ci + 1) * CHUNK)
        cols = [_dot(w_tril[g], va16[rs, g * c_g:(g + 1) * c_g]) for g in range(G_A)]
        mix_rows.append(jnp.concatenate(cols, axis=1) + bias_ref[...])
    a = u * jnp.concatenate(mix_rows, axis=0)

    o_refs, l_refs = (o0_ref, o1_ref, o2_ref), (l0_ref, l1_ref, l2_ref)
    o_slabs, lses = [], []
    for gi, (_, dil) in enumerate(GROUPS):
        if not prompt or dil == 1:
            o_slabs.append([o_refs[gi][:, cc * LANES:(cc + 1) * LANES].astype(F32)
                            for cc in range(GROUP_SLABS)])
            lses.append(l_refs[gi][...])
            continue
        o_sc, l_sc = scratch[2 * (gi - 1)], scratch[2 * (gi - 1) + 1]
        piece = tm // dil
        for r in range(dil):
            rows = slice(r * piece, (r + 1) * piece)
            for cc in range(GROUP_SLABS):
                o_sc[cc, pl.ds(r, piece, stride=dil), :] = (
                    o_refs[gi][rows, cc * LANES:(cc + 1) * LANES].astype(F32))
            l_sc[pl.ds(r, piece, stride=dil), :] = l_refs[gi][rows, :]
        o_slabs.append([o_sc[cc] for cc in range(GROUP_SLABS)])
        lses.append(l_sc[...])

    m = jnp.maximum(jnp.maximum(lses[0], lses[1]), lses[2])
    ws = [jnp.exp(l - m) for l in lses]
    inv = 1.0 / (ws[0] + ws[1] + ws[2])
    ws = [w * inv for w in ws]
    lane = lax.broadcasted_iota(jnp.int32, (tm, LANES), 1)
    first_head = lane < HEAD_DIM
    b_cols = []
    for p in range(GROUP_SLABS):
        acc = jnp.zeros((tm, LANES), F32)
        for w, slabs in zip(ws, o_slabs):
            w_pair = jnp.where(first_head, w[:, 2 * p:2 * p + 1], w[:, 2 * p + 1:2 * p + 2])
            acc = acc + w_pair * slabs[p]
        b_cols.append(acc)
    b = jnp.concatenate(b_cols, axis=1)

    gate_a = jax.nn.sigmoid(jnp.concatenate([_dot(h, wga0_ref[...]), _dot(h, wga1_ref[...])], axis=1))
    gate_b = jax.nn.sigmoid(jnp.concatenate([_dot(h, wgb0_ref[...]), _dot(h, wgb1_ref[...])], axis=1))
    merged = gate_a * _dot(a.astype(BF16), woa_ref[...]) + gate_b * _dot(b.astype(BF16), wob_ref[...])
    x = x + _dot(merged.astype(BF16), wo_ref[...])

    if sample is not None:
        parts = H_G // task_ins[3].shape[2]
        step = pl.program_id(0) * n_tiles + pl.program_id(1)
        _sample_attention(*task_ins, *task_outs, step // parts, dil=sample[0],
                          first_head=(step % parts) * (H_G // parts), t_new=sample[1])

    h2 = _rms(x, gffn_ref[...]).astype(BF16)
    d_ff = wff1_ref.shape[1]
    ff_step = 1024
    ffn = jnp.zeros((tm, d_model), F32)
    for c0 in range(0, d_ff, ff_step):
        hid = jnp.square(jnp.maximum(_dot(h2, wff1_ref[:, c0:c0 + ff_step]), 0.0))
        ffn = ffn + _dot(hid.astype(BF16), wff2_ref[c0:c0 + ff_step, :])
    x = x + ffn

    gate = jax.nn.sigmoid(_dot(_rms(x, gple_ref[...]).astype(BF16), wpg_ref[...]))
    y_ref[...] = x + gate * _dot(pe_ref[...].astype(BF16), wpp_ref[...])


def _merge_call(x2, outs, lses, pe2, wts, ws_mat, bias, *, prompt, batch, tm, sample_task=None, t_new=None):
    n_tok, d_model = x2.shape
    seq = n_tok // batch
    n_tiles = seq // tm
    d_a = wts["g_v"].shape[1]
    tile_index = lambda b, i: b * n_tiles + i
    tok = lambda b, i: (tile_index(b, i), 0)
    row_spec = lambda width: pl.BlockSpec((tm, width), tok)

    names = ("g_mix", "w_in", "w_in", "w_in", "w_in", "w_in", "w_in", "g_v", "ws", "bias", "w_oa", "w_ob", "w_o",
             "g_ffn", "w_ff1", "w_ff2", "g_ple", "w_ple_gate", "w_ple_proj")
    consts = dict(wts, ws=ws_mat, bias=bias)
    half_model = d_model // 2
    assert d_a == half_model and wts["gate_col0"] % half_model == 0
    g0 = wts["gate_col0"] // half_model
    w_in_blocks = iter([0, 1, g0, g0 + 1, g0 + 2, g0 + 3])
    const_specs = [_column_block(d_model, half_model, next(w_in_blocks)) if k == "w_in"
                   else _resident(consts[k].shape) for k in names]
    in_specs = ([row_spec(d_model)]
                + [row_spec(D_GROUP)] * N_GROUPS + [row_spec(LANES)] * N_GROUPS
                + [row_spec(pe2.shape[1])] + const_specs)
    scratch = []
    if prompt:
        va_shape = jax.ShapeDtypeStruct((batch, CHUNK, d_a), F32)
        va_spec = pl.BlockSpec((1, CHUNK, d_a), lambda b, i: (b, 0, 0))
        for _, dil in GROUPS:
            if dil > 1:
                scratch += [pltpu.VMEM((GROUP_SLABS, tm, LANES), F32), pltpu.VMEM((tm, LANES), F32)]
    else:
        va_shape = jax.ShapeDtypeStruct((n_tok, d_a), F32)
        va_spec = row_spec(d_a)
    operands = [x2, *outs, *lses, pe2, *[consts[k] for k in names]]
    out_specs = [row_spec(d_model), va_spec]
    out_shape = [jax.ShapeDtypeStruct((n_tok, d_model), F32), va_shape]
    sample = None
    if sample_task is not None:
        task = sample_task
        n_seq, q_rows, _ = task["q"].shape
        n_h, wb = task["n_heads"], task["buf"].shape[4]
        parts = H_G // n_h
        assert n_seq * parts == batch * n_tiles
        cols = n_h * HEAD_DIM
        col0 = task["group"] * D_GROUP // cols
        seq_of = lambda b, i: tile_index(b, i) // parts
        part_of = lambda b, i: tile_index(b, i) % parts
        buf_spec = pl.BlockSpec((1, 2, n_h, HEAD_DIM, wb), lambda b, i: (seq_of(b, i), 0, part_of(b, i), 0, 0))
        new_spec = pl.BlockSpec((cols, LANES), lambda b, i: (col0 + part_of(b, i), 0))
        in_specs += [pl.BlockSpec((1, q_rows, cols), lambda b, i: (seq_of(b, i), 0, col0 + part_of(b, i))),
                     new_spec, new_spec, buf_spec]
        operands += [task["q"], task["knt"], task["vnt"], task["buf"]]
        out_specs += [pl.BlockSpec((1, q_rows, cols), lambda b, i: (seq_of(b, i), 0, part_of(b, i))),
                      pl.BlockSpec((1, None, q_rows, LANES), lambda b, i: (seq_of(b, i), part_of(b, i), 0, 0)),
                      buf_spec]
        out_shape += [jax.ShapeDtypeStruct((n_seq, q_rows, D_GROUP), F32),
                      jax.ShapeDtypeStruct((n_seq, parts, q_rows, LANES), F32),
                      jax.ShapeDtypeStruct(task["buf"].shape, F32)]
        sample = (GROUPS[task["group"]][1], t_new)
    return pl.pallas_call(
        functools.partial(_merge_kernel, prompt=prompt, tm=tm, n_tiles=n_tiles, sample=sample),
        grid=(batch, n_tiles),
        in_specs=in_specs,
        out_specs=out_specs,
        out_shape=out_shape,
        scratch_shapes=scratch,
        compiler_params=pltpu.CompilerParams(
            dimension_semantics=("arbitrary", "arbitrary"), vmem_limit_bytes=VMEM_LIMIT),
        name="merge_prompt" if prompt else "merge_sample",
    )(*operands)


def kernel(x_prompt, x_sample, p_prompt, p_sample, cache_kv_w128, cache_kv_w512, cache_kv_w2048,
           g_mix, w_in, g_v, w_s, b_s, g_q, g_k, w_oa, w_ob, w_o,
           g_ffn, w_ff1, w_ff2, g_ple, w_ple_gate, w_ple_proj):
    depth = w_in.shape[0]
    batch, seq, d_model = x_prompt.shape
    dec_batch, t_new, _ = x_sample.shape
    d_a = g_v.shape[1]
    n_sample = dec_batch * t_new
    assert n_sample == CHUNK and w_in.shape[2] == 2 * d_a + 3 * D_QKV + 2 * d_model
    caches = (cache_kv_w128, cache_kv_w512, cache_kv_w2048)

    xp = x_prompt.reshape(batch * seq, d_model)
    xs = x_sample.reshape(n_sample, d_model)
    pos_p = np.arange(seq, dtype=np.int32)
    pos_s = PAST_LEN + np.tile(np.arange(t_new, dtype=np.int32), dec_batch)

    kv_p = [[] for _ in GROUPS]
    kv_s = [[] for _ in GROUPS]
    cv_p, cv_s = [], []
    for l in range(depth):
        w_in16 = w_in[l].astype(BF16)
        qkv_col0 = 2 * d_a
        wts = {
            "g_mix": g_mix[l][None, :], "w_in": w_in16, "gate_col0": qkv_col0 + 3 * D_QKV,
            "g_v": g_v[l][None, :],
            "w_oa": w_oa[l].astype(BF16), "w_ob": w_ob[l].astype(BF16), "w_o": w_o[l].astype(BF16),
            "g_ffn": g_ffn[l][None, :], "w_ff1": w_ff1[l].astype(BF16), "w_ff2": w_ff2[l].astype(BF16),
            "g_ple": g_ple[l][None, :], "w_ple_gate": w_ple_gate[l].astype(BF16),
            "w_ple_proj": w_ple_proj[l].astype(BF16),
        }
        c_g = d_a // G_A
        ws_p = w_s[l].astype(BF16)
        bias_p = jnp.repeat(b_s[l].T, c_g, axis=1)
        tok = np.arange(CHUNK)
        same_seq = jnp.asarray(tok[:, None] // t_new == tok[None, :] // t_new)
        row_of = jnp.asarray(tok[:, None] % t_new == np.arange(t_new)[None, :], dtype=F32)
        ws_s = jnp.einsum("rt,gts,cs->grc", row_of, w_s[l][:, :t_new, :t_new], row_of,
                          precision=lax.Precision.HIGHEST)
        ws_s = jnp.where(same_seq, ws_s, 0.0).astype(BF16)
        reps = CHUNK // t_new
        bias_s = jnp.tile(jnp.repeat(b_s[l][:, :t_new].T, c_g, axis=1), (reps, 1))

        tm = TOKEN_TILE
        res = _qkv_call(xp, pos_p, g_mix[l], w_in16, qkv_col0, g_q[l], g_k[l], prompt=True, batch=batch, tm=tm)
        q_g, k_g, v_g, kv_tails = res[0:3], res[3:6], res[6:9], res[9:12]
        qs, knt, vnt = _qkv_call(
            xs, pos_s, g_mix[l], w_in16, qkv_col0, g_q[l], g_k[l], prompt=False, batch=1, tm=n_sample)

        q_rows = 8
        assert LANES % t_new == 0 and t_new <= q_rows and N_GROUPS == 3
        q_pad = jnp.pad(qs.reshape(dec_batch, t_new, D_QKV), ((0, 0), (0, q_rows - t_new), (0, 0)))
        bufs = [jnp.transpose(c[l], (0, 1, 3, 4, 2)) for c in caches]
        task = lambda g, n_h: dict(group=g, first_head=0, n_heads=n_h, q=q_pad, knt=knt, vnt=vnt, buf=bufs[g])
        o2, l2, _ = _attn_call(q_g[2], k_g[2], v_g[2], 2, batch, seq, [], t_new)
        o1, l1, ((so1, sl1, nbuf1),) = _attn_call(q_g[1], k_g[1], v_g[1], 1, batch, seq, [task(1, H_G)], t_new)
        o0, l0, ((so0, sl0, nbuf0),) = _attn_call(q_g[0], k_g[0], v_g[0], 0, batch, seq, [task(0, H_G)], t_new)

        xp, va_p, so2, sl2, nbuf2 = _merge_call(
            xp, (o0, o1, o2), (l0, l1, l2), p_prompt[l].reshape(batch * seq, -1), wts, ws_p, bias_p,
            prompt=True, batch=batch, tm=tm, sample_task=task(2, H_G // 2), t_new=t_new)
        s_outs = (so0, so1, so2)
        s_lses = (sl0, sl1, jnp.sum(sl2, axis=1))
        for gi, nbuf in enumerate((nbuf0, nbuf1, nbuf2)):
            kv_s[gi].append(jnp.transpose(nbuf, (0, 1, 4, 2, 3)))
        for gi, st in enumerate(kv_tails):
            kv_p[gi].append(jnp.transpose(st, (0, 1, 4, 2, 3)))
        cv_p.append(va_p)

        s_outs = [o[:, :t_new].reshape(n_sample, D_GROUP) for o in s_outs]
        s_lses = [s[:, :t_new].reshape(n_sample, LANES) for s in s_lses]
        xs, va_s = _merge_call(xs, s_outs, s_lses, p_sample[l].reshape(n_sample, -1), wts, ws_s, bias_s,
                               prompt=False, batch=1, tm=n_sample)
        cv_s.append(va_s.reshape(dec_batch, t_new, d_a))

    return (xp.reshape(batch, seq, d_model), xs.reshape(dec_batch, t_new, d_model),
            jnp.stack(kv_p[0]), jnp.stack(kv_p[1]), jnp.stack(kv_p[2]),
            jnp.stack(kv_s[0]), jnp.stack(kv_s[1]), jnp.stack(kv_s[2]),
            jnp.stack(cv_p), jnp.stack(cv_s))
```

```python
import functools
import math

import jax
import jax.numpy as jnp
import numpy as np
from jax import lax
from jax.experimental import pallas as pl
from jax.experimental.pallas import tpu as pltpu

F32 = jnp.float32
BF16 = jnp.bfloat16

LANES = 128
BF16_ROWS = 16
MXU_DIM = 256
TOKEN_TILE = 256
HEAD_DIM = 64
H_G = 8
HEADS_PER_VREG = LANES // HEAD_DIM
GROUPS = ((128, 1), (512, 4), (2048, 16))
N_GROUPS = len(GROUPS)
D_GROUP = H_G * HEAD_DIM
GROUP_SLABS = D_GROUP // LANES
D_QKV = N_GROUPS * D_GROUP
N_KEYS = 129
SUB_BLOCK = 128
ROT_DIM = HEAD_DIM // 4
ROT_HALF = ROT_DIM // 2
ROPE_THETA = 500000.0
CHUNK = 128
G_A = 4
PAST_LEN = 8192
ATTN_SUB_BLOCKS = 4
EPS = 1e-6
NEG = -1e30
LN2 = math.log(2.0)
LOG2E = 1.0 / LN2
VMEM_LIMIT = 62 * 1024 * 1024


def _rms(x, g):
    return x * lax.rsqrt(jnp.mean(x * x, axis=-1, keepdims=True) + EPS) * g


def _resident(shape):
    zeros = (0,) * len(shape)
    return pl.BlockSpec(shape, lambda *_: zeros, pipeline_mode=pl.Buffered(1))


def _dot(a, b):
    return jnp.dot(a, b, preferred_element_type=F32)


def _dot_nt(a, b):
    return lax.dot_general(a, b, (((1,), (1,)), ((), ())), preferred_element_type=F32)


def _project_qkv(x_ref, gmix_ref, gq_ref, gk_ref, rc_ref, ru_ref, rd_ref, bd_ref, w_refs, perm_refs,
                 dils, emit):
    rows = x_ref.shape[0]
    h = _rms(x_ref[...], gmix_ref[...]).astype(BF16)
    hs = [h if p_ref is None else _dot(p_ref[...], h).astype(BF16) for p_ref in perm_refs]
    bd = bd_ref[...]
    wide = bd.shape[0]

    def table(ref, dil):
        if dil == 1:
            return ref[...]
        return jnp.concatenate([ref[pl.ds(r, rows // dil, stride=dil), :] for r in range(dil)], axis=0)

    tables = {dil: tuple(table(ref, dil) for ref in (rc_ref, ru_ref, rd_ref)) for dil in set(dils)}

    for ti, g_ref in enumerate((gq_ref, gk_ref, None)):
        for gi in range(N_GROUPS):
            z = _dot(hs[gi], w_refs[ti * N_GROUPS + gi][...])
            if g_ref is None:
                emit(ti, gi, [z[:, c * LANES:(c + 1) * LANES] for c in range(GROUP_SLABS)])
                continue
            rc, ru, rd = tables[dils[gi]]

            def rot(zn):
                return zn * rc + pltpu.roll(zn, LANES - ROT_HALF, 1) * ru + pltpu.roll(zn, ROT_HALF, 1) * rd

            vals = []
            for c in range(0, D_GROUP, wide):
                zc = z[:, c:c + wide]
                ss = _dot((zc * zc).astype(BF16), bd)
                zn = zc * lax.rsqrt(ss * (1.0 / HEAD_DIM) + EPS)
                vals += [rot(zn[:, j:j + LANES] * g_ref[...]) for j in range(0, wide, LANES)]
            emit(ti, gi, vals)


def _qkv_prompt_kernel(*refs, tm, n_tiles):
    n_perm = sum(dil > 1 for _, dil in GROUPS)
    n_w = 3 * N_GROUPS
    ins, w_refs, perms, rest = refs[:8], refs[8:8 + n_w], refs[8 + n_w:8 + n_w + n_perm], refs[8 + n_w + n_perm:]
    outs, kv_refs = rest[:3 * N_GROUPS], rest[3 * N_GROUPS:4 * N_GROUPS]
    kv_slabs, order_slab = rest[4 * N_GROUPS:4 * N_GROUPS + 2], rest[4 * N_GROUPS + 2]
    perms = list(perms)
    perm_refs = [perms.pop(0) if dil > 1 else None for _, dil in GROUPS]

    def emit(ti, gi, vals):
        out = outs[ti * N_GROUPS + gi]
        for cc, val in enumerate(vals):
            out[:, cc * LANES:(cc + 1) * LANES] = val.astype(BF16)
            if ti > 0:
                kv_slabs[ti - 1][gi * GROUP_SLABS + cc] = val

    _project_qkv(*ins, w_refs, perm_refs, [dil for _, dil in GROUPS], emit)

    i = pl.program_id(1)
    seq = n_tiles * tm
    for gi, (win, dil) in enumerate(GROUPS):
        rows = min(win, tm)
        first = (seq - win) // tm if win >= tm else n_tiles - 1
        piece = tm // dil

        @pl.when(i >= first)
        def _(gi=gi, dil=dil, rows=rows, piece=piece):
            for kv in range(2):
                for cc in range(GROUP_SLABS):
                    c = gi * GROUP_SLABS + cc
                    if dil > 1:
                        for r in range(dil):
                            order_slab[pl.ds(r, piece, stride=dil), :] = kv_slabs[kv][c, r * piece:(r + 1) * piece, :]
                        val = order_slab[...]
                    else:
                        val = kv_slabs[kv][c]
                    t = val[tm - rows:tm].T
                    for hh in range(HEADS_PER_VREG):
                        kv_refs[gi][0, kv, cc * HEADS_PER_VREG + hh] = t[hh * HEAD_DIM:(hh + 1) * HEAD_DIM, :]


def _qkv_sample_kernel(*refs):
    n_w = 3 * N_GROUPS
    ins, w_refs, outs = refs[:8], refs[8:8 + n_w], refs[8 + n_w:11 + n_w]

    def emit(ti, gi, vals):
        for cc, val in enumerate(vals):
            c = gi * GROUP_SLABS + cc
            if ti == 0:
                outs[ti][:, c * LANES:(c + 1) * LANES] = val
            else:
                outs[ti][c * LANES:(c + 1) * LANES, :] = val.T

    _project_qkv(*ins, w_refs, [None] * N_GROUPS, [1] * N_GROUPS, emit)


def _residue_major_source(dil, tm):
    row = np.arange(tm, dtype=np.int32)
    piece = tm // dil
    return (row % piece) * dil + row // piece


def _rotary_tables(pos):
    inv = np.float32(ROPE_THETA) ** (-np.arange(ROT_HALF, dtype=np.float32) * np.float32(2.0 / ROT_DIM))
    ang = pos.astype(np.float32)[:, None] * inv[None, :]
    cos, sin = np.cos(ang), np.sin(ang)
    n = pos.shape[0]
    pad = np.zeros((n, HEAD_DIM - ROT_DIM), np.float32)
    zero = np.zeros((n, ROT_HALF), np.float32)
    rc = np.concatenate([cos, cos, pad + 1.0], axis=1)
    ru = np.concatenate([-sin, zero, pad], axis=1)
    rd = np.concatenate([zero, sin, pad], axis=1)
    return [jnp.asarray(np.tile(t, (1, HEADS_PER_VREG)).astype(np.float32)) for t in (rc, ru, rd)]


def _column_block(n_rows, width, index):
    return pl.BlockSpec((n_rows, width), lambda *_: (0, index), pipeline_mode=pl.Buffered(1))


def _qkv_call(x2, pos, g_mix, w_in, qkv_col0, g_q, g_k, *, prompt, batch, tm):
    n_tok, d_model = x2.shape
    seq = n_tok // batch
    n_tiles = seq // tm
    perms = []
    if prompt:
        assert all(tm % (dil * BF16_ROWS) == 0 for _, dil in GROUPS)
        for _, dil in GROUPS:
            if dil > 1:
                src = _residue_major_source(dil, tm)
                perms.append(jnp.asarray(src[:, None] == np.arange(tm)[None, :], dtype=BF16))
    rot = _rotary_tables(pos)
    gq = jnp.tile(g_q.astype(F32), HEADS_PER_VREG)[None, :] * (LOG2E * HEAD_DIM ** -0.5)
    gk = jnp.tile(g_k.astype(F32), HEADS_PER_VREG)[None, :]
    lane = np.arange(MXU_DIM)
    bd = jnp.asarray(lane[:, None] // HEAD_DIM == lane[None, :] // HEAD_DIM, dtype=BF16)

    tile_index = lambda b, i: b * n_tiles + i
    tok = lambda b, i: (tile_index(b, i), 0)
    in_specs = [
        pl.BlockSpec((tm, d_model), tok),
        _resident((1, d_model)),
        _resident((1, LANES)),
        _resident((1, LANES)),
        pl.BlockSpec((tm, LANES), lambda b, i: (i, 0)),
        pl.BlockSpec((tm, LANES), lambda b, i: (i, 0)),
        pl.BlockSpec((tm, LANES), lambda b, i: (i, 0)),
        _resident((MXU_DIM, MXU_DIM)),
    ] + [_column_block(d_model, D_GROUP, qkv_col0 // D_GROUP + j) for j in range(3 * N_GROUPS)]
    in_specs += [_resident(p.shape) for p in perms]
    assert qkv_col0 % D_GROUP == 0
    if prompt:
        out_shape = [jax.ShapeDtypeStruct((n_tok, D_GROUP), BF16)] * (3 * N_GROUPS)
        out_specs = [pl.BlockSpec((tm, D_GROUP), tok)] * (3 * N_GROUPS)
        for win, _ in GROUPS:
            rows = min(win, tm)
            first = (seq - win) // tm if win >= tm else n_tiles - 1
            out_shape.append(jax.ShapeDtypeStruct((batch, 2, H_G, HEAD_DIM, win), F32))
            out_specs.append(pl.BlockSpec(
                (1, 2, H_G, HEAD_DIM, rows),
                lambda b, i, first=first: (b, 0, 0, 0, jnp.maximum(i - first, 0))))
        scratch = [pltpu.VMEM((D_QKV // LANES, tm, LANES), F32)] * 2 + [pltpu.VMEM((tm, LANES), F32)]
        body = functools.partial(_qkv_prompt_kernel, tm=tm, n_tiles=n_tiles)
    else:
        assert n_tiles == 1 and batch == 1 and tm == LANES
        out_shape = [jax.ShapeDtypeStruct((n_tok, D_QKV), F32)] + [jax.ShapeDtypeStruct((D_QKV, n_tok), F32)] * 2
        out_specs = [pl.BlockSpec((tm, D_QKV), tok)] + [pl.BlockSpec((D_QKV, tm), lambda b, i: (0, 0))] * 2
        scratch = []
        body = _qkv_sample_kernel

    return pl.pallas_call(
        body,
        grid=(batch, n_tiles),
        in_specs=in_specs,
        out_specs=out_specs,
        out_shape=out_shape,
        scratch_shapes=scratch,
        compiler_params=pltpu.CompilerParams(
            dimension_semantics=("arbitrary", "arbitrary"), vmem_limit_bytes=VMEM_LIMIT),
        name="qkv_prompt" if prompt else "qkv_sample",
    )(x2, g_mix[None, :], gq, gk, *rot, bd, *[w_in] * (3 * N_GROUPS), *perms)


def _attn_kernel(q_ref, kc_ref, kp_ref, vc_ref, vp_ref, *rest, dil, tasks, n_alias, t_new):
    n_in = 4 * len(tasks)
    n_skip = n_in + n_alias
    task_ins, (o_ref, lse_ref), task_outs = rest[:n_in], rest[n_skip:n_skip + 2], rest[n_skip + 2:]
    step = (pl.program_id(0) * pl.num_programs(1) + pl.program_id(1)) * pl.num_programs(2) + pl.program_id(2)

    has_prev = pl.program_id(1) > 0
    piece = TOKEN_TILE // dil

    def residue_rows(i):
        return [(t, 0, slice(i * piece, (i + 1) * piece)) for t in range(SUB_BLOCK // piece)]

    def load(ref, rows, sl):
        parts = [ref[rs + (sl,)] for rs in rows]
        return parts[0] if len(parts) == 1 else jnp.concatenate(parts, axis=0)

    def store(ref, rows, sl, val):
        n = val.shape[0] // len(rows)
        for t, rs in enumerate(rows):
            ref[rs + (sl,)] = val[t * n:(t + 1) * n]
    key = lax.broadcasted_iota(jnp.int32, (2 * SUB_BLOCK, HEADS_PER_VREG * SUB_BLOCK), 0)
    qry = lax.broadcasted_iota(jnp.int32, (2 * SUB_BLOCK, HEADS_PER_VREG * SUB_BLOCK), 1) % SUB_BLOCK
    vis_cur = key <= qry
    vis_prev = key - SUB_BLOCK >= qry
    lane16 = lax.broadcasted_iota(jnp.int32, (SUB_BLOCK, LANES), 1).astype(F32).astype(BF16)
    head_lanes = (lane16 < HEAD_DIM, lane16 >= HEAD_DIM)
    eye_r = lax.broadcasted_iota(jnp.int32, (LANES, LANES), 0)
    eye_c = lax.broadcasted_iota(jnp.int32, (LANES, LANES), 1)
    eye = (eye_r == eye_c).astype(F32).astype(BF16)
    ones_rows = jnp.ones((BF16_ROWS, 2 * SUB_BLOCK), BF16)
    head_row = lax.broadcasted_iota(jnp.int32, (H_G, SUB_BLOCK), 0)

    pad = jnp.zeros((LANES - H_G, SUB_BLOCK), F32)
    slabs = [slice(p * LANES, (p + 1) * LANES) for p in range(GROUP_SLABS)]

    if dil == 1:
        blocks = [([(slice(0, SUB_BLOCK),)], kp_ref, vp_ref, [(slice(0, SUB_BLOCK),)], has_prev)]
        blocks += [([(slice(i * SUB_BLOCK, (i + 1) * SUB_BLOCK),)], kc_ref, vc_ref,
                    [(slice((i - 1) * SUB_BLOCK, i * SUB_BLOCK),)], True) for i in range(1, ATTN_SUB_BLOCKS)]
    else:
        blocks = [(residue_rows(i), kp_ref, vp_ref, residue_rows(i), has_prev)
                  for i in range(ATTN_SUB_BLOCKS)]

    def score_stage(blks):
        scores, vts = [], []
        for rows, kp_src, vp_src, prev_rows, _ in blks:
            for sl in slabs:
                q = load(q_ref, rows, sl)
                qq = jnp.concatenate([jnp.where(hl, q, jnp.zeros_like(q)) for hl in head_lanes], axis=0)
                kk = jnp.concatenate([load(kc_ref, rows, sl), load(kp_src, prev_rows, sl)], axis=0)
                vv = jnp.concatenate([load(vc_ref, rows, sl), load(vp_src, prev_rows, sl)], axis=0)
                scores.append(_dot_nt(kk, qq))
                vts.append(jnp.concatenate([_dot_nt(eye, vv).astype(BF16), ones_rows], axis=0))
        return scores, vts

    def softmax_stage(blks, scores):
        maxes, probs = [], []
        for bi, (_, _, _, _, has_prev) in enumerate(blks):
            visible = jnp.logical_or(vis_cur, jnp.logical_and(vis_prev, has_prev))
            for s in scores[bi * GROUP_SLABS:(bi + 1) * GROUP_SLABS]:
                s = jnp.where(visible, s, NEG)
                m = jnp.max(s, axis=0, keepdims=True)
                maxes.append(m)
                probs.append(jnp.exp2(s - m).astype(BF16))
        return maxes, probs

    def value_stage(vts, probs):
        return [_dot(vt, e) for vt, e in zip(vts, probs)]

    def store_stage(blks, outs, maxes):
        for bi, (rows, _, _, _, _) in enumerate(blks):
            lse_rows = jnp.zeros((H_G, SUB_BLOCK), F32)
            for p, sl in enumerate(slabs):
                ot, m = outs[bi * GROUP_SLABS + p], maxes[bi * GROUP_SLABS + p]
                den = ot[LANES:LANES + 1, :]
                lse = m * LN2 + jnp.log(den)
                inv = 1.0 / den
                o_halves = []
                for half in range(HEADS_PER_VREG):
                    qs = slice(half * SUB_BLOCK, (half + 1) * SUB_BLOCK)
                    o_halves.append(ot[half * HEAD_DIM:(half + 1) * HEAD_DIM, qs] * inv[:, qs])
                    lse_rows = jnp.where(head_row == p * HEADS_PER_VREG + half, lse[:, qs], lse_rows)
                store(o_ref, rows, sl, jnp.concatenate(o_halves, axis=0).T.astype(BF16))
            store(lse_ref, rows, slice(None), jnp.concatenate([lse_rows, pad], axis=0).T)

    first, second = blocks[:ATTN_SUB_BLOCKS // 2], blocks[ATTN_SUB_BLOCKS // 2:]
    scores_a, vts_a = score_stage(first)
    scores_b, vts_b = score_stage(second)
    for ti, (task_dil, first_head) in enumerate(tasks):
        _sample_attention(*task_ins[4 * ti:4 * ti + 4], *task_outs[3 * ti:3 * ti + 3], step,
                          dil=task_dil, first_head=first_head, t_new=t_new)
    maxes_a, probs_a = softmax_stage(first, scores_a)
    outs_a = value_stage(vts_a, probs_a)
    maxes_b, probs_b = softmax_stage(second, scores_b)
    outs_b = value_stage(vts_b, probs_b)
    store_stage(first, outs_a, maxes_a)
    store_stage(second, outs_b, maxes_b)


def _attn_call(q, k, v, gi, batch, seq, sample_tasks, t_new):
    dil = GROUPS[gi][1]
    n_tok = batch * seq
    if dil == 1:
        rows, parts = ATTN_SUB_BLOCKS * SUB_BLOCK, 1
        assert seq % rows == 0
        steps = seq // rows
        view = lambda width: (n_tok, width)
        cur_map = lambda b, s, p: (b * steps + s, 0)
        cur = lambda width: pl.BlockSpec((rows, width), cur_map)
        prev = pl.BlockSpec((SUB_BLOCK, D_GROUP),
                            lambda b, s, p: (jnp.maximum((b * steps + s) * ATTN_SUB_BLOCKS - 1, 0), 0))
    else:
        span, parts = SUB_BLOCK * dil, dil // ATTN_SUB_BLOCKS
        assert dil % ATTN_SUB_BLOCKS == 0 and span % TOKEN_TILE == 0 and seq % span == 0
        assert TOKEN_TILE % (dil * BF16_ROWS) == 0
        steps, span_tiles, part_rows = seq // span, span // TOKEN_TILE, TOKEN_TILE // parts
        view = lambda width: (n_tok // TOKEN_TILE, parts, part_rows, width)
        cur_map = lambda b, s, p: (b * steps + s, p, 0, 0)
        cur = lambda width: pl.BlockSpec((span_tiles, 1, part_rows, width), cur_map)
        prev = pl.BlockSpec((span_tiles, 1, part_rows, D_GROUP),
                            lambda b, s, p: (b * steps + jnp.maximum(s - 1, 0), p, 0, 0))
    q, k, v = (t.reshape(view(D_GROUP)) for t in (q, k, v))
    in_specs = [cur(D_GROUP), cur(D_GROUP), prev, cur(D_GROUP), prev]
    out_specs = [cur(D_GROUP), cur(LANES)]
    out_shape = [jax.ShapeDtypeStruct(view(D_GROUP), BF16), jax.ShapeDtypeStruct(view(LANES), F32)]
    operands, alias_operands, aliases = [q, k, k, v, v], [], {}
    seq_of = lambda b, s, p: (b * steps + s) * parts + p
    for task in sample_tasks:
        n_seq, q_rows, _ = task["q"].shape
        assert n_seq == batch * steps * parts
        wb = task["buf"].shape[4]
        n_h, h0 = task["n_heads"], task["first_head"]
        cols = n_h * HEAD_DIM
        col_block = (task["group"] * D_GROUP + h0 * HEAD_DIM) // cols
        buf_spec = pl.BlockSpec((1, 2, n_h, HEAD_DIM, wb), lambda b, s, p, hb=h0 // n_h: (seq_of(b, s, p), 0, hb, 0, 0))
        new_spec = pl.BlockSpec((cols, LANES), lambda b, s, p, cb=col_block: (cb, 0))
        in_specs += [pl.BlockSpec((1, q_rows, cols), lambda b, s, p, cb=col_block: (seq_of(b, s, p), 0, cb)),
                     new_spec, new_spec, buf_spec]
        operands += [task["q"], task["knt"], task["vnt"], task["buf"]]
        if task.get("nbuf") is not None:
            aliases[len(alias_operands)] = len(out_shape) + 2
            alias_operands.append(task["nbuf"])
        out_specs += [pl.BlockSpec((1, q_rows, cols), lambda b, s, p: (seq_of(b, s, p), 0, 0)),
                      pl.BlockSpec((1, q_rows, LANES), lambda b, s, p: (seq_of(b, s, p), 0, 0)), buf_spec]
        out_shape += [jax.ShapeDtypeStruct((n_seq, q_rows, cols), F32),
                      jax.ShapeDtypeStruct((n_seq, q_rows, LANES), F32),
                      jax.ShapeDtypeStruct(task["buf"].shape, F32)]
    n_main = len(operands)
    res = pl.pallas_call(
        functools.partial(_attn_kernel, dil=dil, n_alias=len(alias_operands), t_new=t_new,
                          tasks=tuple((GROUPS[t["group"]][1], t["first_head"]) for t in sample_tasks)),
        grid=(batch, steps, parts),
        in_specs=in_specs + [pl.BlockSpec(memory_space=pl.ANY)] * len(alias_operands),
        out_specs=out_specs,
        out_shape=out_shape,
        input_output_aliases={n_main + i: o for i, o in aliases.items()},
        compiler_params=pltpu.CompilerParams(
            dimension_semantics=("arbitrary", "arbitrary", "arbitrary"), vmem_limit_bytes=VMEM_LIMIT),
        name=f"attn_g{gi}",
    )(*operands, *alias_operands)
    task_res = [tuple(res[2 + 3 * i:5 + 3 * i]) for i in range(len(sample_tasks))]
    return res[0].reshape(n_tok, D_GROUP), res[1].reshape(n_tok, LANES), task_res


def _sample_attention(q_ref, knt_ref, vnt_ref, buf_ref, o_ref, lse_ref, nbuf_ref, seq_index, *,
                      dil, first_head, t_new):
    assert dil & (dil - 1) == 0
    q_rows = q_ref.shape[1]
    n_heads, wb = buf_ref.shape[2], buf_ref.shape[4]
    width = wb + LANES
    shift = lax.rem(LANES - t_new * seq_index, LANES)
    knt = pltpu.roll(knt_ref[...], shift, 1)
    vnt = pltpu.roll(vnt_ref[...], shift, 1)
    q = q_ref[0]

    t_row = lax.broadcasted_iota(jnp.int32, (q_rows, width), 0)
    p_col = lax.broadcasted_iota(jnp.int32, (q_rows, width), 1)
    back = wb + t_row - p_col
    valid = jnp.logical_and(jnp.logical_and(back >= 0, (back & (dil - 1)) == 0),
                            jnp.logical_and(back <= (N_KEYS - 1) * dil, p_col < wb + t_new))
    valid = jnp.concatenate([valid] * n_heads, axis=0)
    lane = lax.broadcasted_iota(jnp.int32, (q_rows, LANES), 1)
    cols = n_heads * HEAD_DIM

    kx = jnp.concatenate([buf_ref[0, 0].reshape(cols, wb), knt], axis=1)
    vx = jnp.concatenate([buf_ref[0, 1].reshape(cols, wb), vnt], axis=1)
    nbuf_ref[0, 0] = pltpu.roll(kx, width - t_new, 1)[:, :wb].reshape(n_heads, HEAD_DIM, wb)
    nbuf_ref[0, 1] = pltpu.roll(vx, width - t_new, 1)[:, :wb].reshape(n_heads, HEAD_DIM, wb)

    q_all = jnp.concatenate([q] * n_heads, axis=0)
    r_head = lax.broadcasted_iota(jnp.int32, q_all.shape, 0) // q_rows
    c_head = lax.broadcasted_iota(jnp.int32, q_all.shape, 1) // HEAD_DIM
    q_all = jnp.where(r_head == c_head, q_all, 0.0).astype(BF16)
    s = jnp.where(valid, _dot(q_all, kx.astype(BF16)), NEG)
    m = jnp.max(s, axis=1, keepdims=True)
    e = jnp.exp2(s - m)
    den = jnp.sum(e, axis=1, keepdims=True)
    o_all = _dot_nt(e.astype(BF16), vx.astype(BF16)) / den
    lse = m * LN2 + jnp.log(den)
    o_ref[0] = jnp.concatenate(
        [o_all[h * q_rows:(h + 1) * q_rows, h * HEAD_DIM:(h + 1) * HEAD_DIM] for h in range(n_heads)], axis=1)
    lse_ref[0] = functools.reduce(jnp.add, [
        jnp.where(lane == first_head + h, lse[h * q_rows:(h + 1) * q_rows], 0.0) for h in range(n_heads)])


def _merge_kernel(x_ref, o0_ref, o1_ref, o2_ref, l0_ref, l1_ref, l2_ref, pe_ref,
                  gmix_ref, wu_ref, wva_ref, wga0_ref, wga1_ref, wgb0_ref, wgb1_ref,
                  gv_ref, ws_ref, bias_ref, woa_ref, wob_ref, wo_ref,
                  gffn_ref, wff1_ref, wff2_ref, gple_ref, wpg_ref, wpp_ref,
                  *rest, prompt, tm, n_tiles, sample):
    if sample is None:
        (y_ref, va_ref), scratch = rest[:2], rest[2:]
    else:
        task_ins, (y_ref, va_ref), task_outs, scratch = rest[:4], rest[4:6], rest[6:9], rest[9:]
    d_model = x_ref.shape[1]
    d_a = gv_ref.shape[1]
    c_g = d_a // G_A
    x = x_ref[...]
    h = _rms(x, gmix_ref[...]).astype(BF16)

    u = jax.nn.gelu(_dot(h, wu_ref[...]))
    va = _rms(jax.nn.gelu(_dot(h, wva_ref[...])), gv_ref[...])
    if prompt:
        @pl.when(pl.program_id(1) == n_tiles - 1)
        def _():
            va_ref[0] = va[tm - CHUNK:tm, :]
    else:
        va_ref[...] = va
    va16 = va.astype(BF16)
    r_i = lax.broadcasted_iota(jnp.int32, (CHUNK, CHUNK), 0)
    c_i = lax.broadcasted_iota(jnp.int32, (CHUNK, CHUNK), 1)
    w_tril = [jnp.where(r_i >= c_i, ws_ref[g], jnp.zeros((CHUNK, CHUNK), BF16)) for g in range(G_A)]
    mix_rows = []
    for ci in range(tm // CHUNK):
        rs = slice(ci * CHUNK, (ci + 1) * CHUNK)
        cols = [_dot(w_tril[g], va16[rs, g * c_g:(g + 1) * c_g]) for g in range(G_A)]
        mix_rows.append(jnp.concatenate(cols, axis=1) + bias_ref[...])
    a = u * jnp.concatenate(mix_rows, axis=0)

    o_refs, l_refs = (o0_ref, o1_ref, o2_ref), (l0_ref, l1_ref, l2_ref)
    o_slabs, lses = [], []
    for gi, (_, dil) in enumerate(GROUPS):
        if not prompt or dil == 1:
            o_slabs.append([o_refs[gi][:, cc * LANES:(cc + 1) * LANES].astype(F32)
                            for cc in range(GROUP_SLABS)])
            lses.append(l_refs[gi][...])
            continue
        o_sc, l_sc = scratch[2 * (gi - 1)], scratch[2 * (gi - 1) + 1]
        piece = tm // dil
        for r in range(dil):
            rows = slice(r * piece, (r + 1) * piece)
            for cc in range(GROUP_SLABS):
                o_sc[cc, pl.ds(r, piece, stride=dil), :] = (
                    o_refs[gi][rows, cc * LANES:(cc + 1) * LANES].astype(F32))
            l_sc[pl.ds(r, piece, stride=dil), :] = l_refs[gi][rows, :]
        o_slabs.append([o_sc[cc] for cc in range(GROUP_SLABS)])
        lses.append(l_sc[...])

    m = jnp.maximum(jnp.maximum(lses[0], lses[1]), lses[2])
    ws = [jnp.exp(l - m) for l in lses]
    inv = 1.0 / (ws[0] + ws[1] + ws[2])
    ws = [w * inv for w in ws]
    lane = lax.broadcasted_iota(jnp.int32, (tm, LANES), 1)
    first_head = lane < HEAD_DIM
    b_cols = []
    for p in range(GROUP_SLABS):
        acc = jnp.zeros((tm, LANES), F32)
        for w, slabs in zip(ws, o_slabs):
            w_pair = jnp.where(first_head, w[:, 2 * p:2 * p + 1], w[:, 2 * p + 1:2 * p + 2])
            acc = acc + w_pair * slabs[p]
        b_cols.append(acc)
    b = jnp.concatenate(b_cols, axis=1)

    gate_a = jax.nn.sigmoid(jnp.concatenate([_dot(h, wga0_ref[...]), _dot(h, wga1_ref[...])], axis=1))
    gate_b = jax.nn.sigmoid(jnp.concatenate([_dot(h, wgb0_ref[...]), _dot(h, wgb1_ref[...])], axis=1))
    merged = gate_a * _dot(a.astype(BF16), woa_ref[...]) + gate_b * _dot(b.astype(BF16), wob_ref[...])
    x = x + _dot(merged.astype(BF16), wo_ref[...])

    if sample is not None:
        parts = H_G // task_ins[3].shape[2]
        step = pl.program_id(0) * n_tiles + pl.program_id(1)
        _sample_attention(*task_ins, *task_outs, step // parts, dil=sample[0],
                          first_head=(step % parts) * (H_G // parts), t_new=sample[1])

    h2 = _rms(x, gffn_ref[...]).astype(BF16)
    d_ff = wff1_ref.shape[1]
    ff_step = 1024
    ffn = jnp.zeros((tm, d_model), F32)
    for c0 in range(0, d_ff, ff_step):
        hid = jnp.square(jnp.maximum(_dot(h2, wff1_ref[:, c0:c0 + ff_step]), 0.0))
        ffn = ffn + _dot(hid.astype(BF16), wff2_ref[c0:c0 + ff_step, :])
    x = x + ffn

    gate = jax.nn.sigmoid(_dot(_rms(x, gple_ref[...]).astype(BF16), wpg_ref[...]))
    y_ref[...] = x + gate * _dot(pe_ref[...].astype(BF16), wpp_ref[...])


def _merge_call(x2, outs, lses, pe2, wts, ws_mat, bias, *, prompt, batch, tm, sample_task=None, t_new=None):
    n_tok, d_model = x2.shape
    seq = n_tok // batch
    n_tiles = seq // tm
    d_a = wts["g_v"].shape[1]
    tile_index = lambda b, i: b * n_tiles + i
    tok = lambda b, i: (tile_index(b, i), 0)
    row_spec = lambda width: pl.BlockSpec((tm, width), tok)

    names = ("g_mix", "w_in", "w_in", "w_in", "w_in", "w_in", "w_in", "g_v", "ws", "bias", "w_oa", "w_ob", "w_o",
             "g_ffn", "w_ff1", "w_ff2", "g_ple", "w_ple_gate", "w_ple_proj")
    consts = dict(wts, ws=ws_mat, bias=bias)
    half_model = d_model // 2
    assert d_a == half_model and wts["gate_col0"] % half_model == 0
    g0 = wts["gate_col0"] // half_model
    w_in_blocks = iter([0, 1, g0, g0 + 1, g0 + 2, g0 + 3])
    const_specs = [_column_block(d_model, half_model, next(w_in_blocks)) if k == "w_in"
                   else _resident(consts[k].shape) for k in names]
    in_specs = ([row_spec(d_model)]
                + [row_spec(D_GROUP)] * N_GROUPS + [row_spec(LANES)] * N_GROUPS
                + [row_spec(pe2.shape[1])] + const_specs)
    scratch = []
    if prompt:
        va_shape = jax.ShapeDtypeStruct((batch, CHUNK, d_a), F32)
        va_spec = pl.BlockSpec((1, CHUNK, d_a), lambda b, i: (b, 0, 0))
        for _, dil in GROUPS:
            if dil > 1:
                scratch += [pltpu.VMEM((GROUP_SLABS, tm, LANES), F32), pltpu.VMEM((tm, LANES), F32)]
    else:
        va_shape = jax.ShapeDtypeStruct((n_tok, d_a), F32)
        va_spec = row_spec(d_a)
    operands = [x2, *outs, *lses, pe2, *[consts[k] for k in names]]
    out_specs = [row_spec(d_model), va_spec]
    out_shape = [jax.ShapeDtypeStruct((n_tok, d_model), F32), va_shape]
    sample = None
    if sample_task is not None:
        task = sample_task
        n_seq, q_rows, _ = task["q"].shape
        n_h, wb = task["n_heads"], task["buf"].shape[4]
        parts = H_G // n_h
        assert n_seq * parts == batch * n_tiles
        cols = n_h * HEAD_DIM
        col0 = task["group"] * D_GROUP // cols
        seq_of = lambda b, i: tile_index(b, i) // parts
        part_of = lambda b, i: tile_index(b, i) % parts
        buf_spec = pl.BlockSpec((1, 2, n_h, HEAD_DIM, wb), lambda b, i: (seq_of(b, i), 0, part_of(b, i), 0, 0))
        new_spec = pl.BlockSpec((cols, LANES), lambda b, i: (col0 + part_of(b, i), 0))
        in_specs += [pl.BlockSpec((1, q_rows, cols), lambda b, i: (seq_of(b, i), 0, col0 + part_of(b, i))),
                     new_spec, new_spec, buf_spec]
        operands += [task["q"], task["knt"], task["vnt"], task["buf"]]
        out_specs += [pl.BlockSpec((1, q_rows, cols), lambda b, i: (seq_of(b, i), 0, part_of(b, i))),
                      pl.BlockSpec((1, None, q_rows, LANES), lambda b, i: (seq_of(b, i), part_of(b, i), 0, 0)),
                      buf_spec]
        out_shape += [jax.ShapeDtypeStruct((n_seq, q_rows, D_GROUP), F32),
                      jax.ShapeDtypeStruct((n_seq, parts, q_rows, LANES), F32),
                      jax.ShapeDtypeStruct(task["buf"].shape, F32)]
        sample = (GROUPS[task["group"]][1], t_new)
    return pl.pallas_call(
        functools.partial(_merge_kernel, prompt=prompt, tm=tm, n_tiles=n_tiles, sample=sample),
        grid=(batch, n_tiles),
        in_specs=in_specs,
        out_specs=out_specs,
        out_shape=out_shape,
        scratch_shapes=scratch,
        compiler_params=pltpu.CompilerParams(
            dimension_semantics=("arbitrary", "arbitrary"), vmem_limit_bytes=VMEM_LIMIT),
        name="merge_prompt" if prompt else "merge_sample",
    )(*operands)


def kernel(x_prompt, x_sample, p_prompt, p_sample, cache_kv_w128, cache_kv_w512, cache_kv_w2048,
           g_mix, w_in, g_v, w_s, b_s, g_q, g_k, w_oa, w_ob, w_o,
           g_ffn, w_ff1, w_ff2, g_ple, w_ple_gate, w_ple_proj):
    depth = w_in.shape[0]
    batch, seq, d_model = x_prompt.shape
    dec_batch, t_new, _ = x_sample.shape
    d_a = g_v.shape[1]
    n_sample = dec_batch * t_new
    assert n_sample == CHUNK and w_in.shape[2] == 2 * d_a + 3 * D_QKV + 2 * d_model
    caches = (cache_kv_w128, cache_kv_w512, cache_kv_w2048)

    xp = x_prompt.reshape(batch * seq, d_model)
    xs = x_sample.reshape(n_sample, d_model)
    pos_p = np.arange(seq, dtype=np.int32)
    pos_s = PAST_LEN + np.tile(np.arange(t_new, dtype=np.int32), dec_batch)

    kv_p = [[] for _ in GROUPS]
    kv_s = [[] for _ in GROUPS]
    cv_p, cv_s = [], []
    for l in range(depth):
        w_in16 = w_in[l].astype(BF16)
        qkv_col0 = 2 * d_a
        wts = {
            "g_mix": g_mix[l][None, :], "w_in": w_in16, "gate_col0": qkv_col0 + 3 * D_QKV,
            "g_v": g_v[l][None, :],
            "w_oa": w_oa[l].astype(BF16), "w_ob": w_ob[l].astype(BF16), "w_o": w_o[l].astype(BF16),
            "g_ffn": g_ffn[l][None, :], "w_ff1": w_ff1[l].astype(BF16), "w_ff2": w_ff2[l].astype(BF16),
            "g_ple": g_ple[l][None, :], "w_ple_gate": w_ple_gate[l].astype(BF16),
            "w_ple_proj": w_ple_proj[l].astype(BF16),
        }
        c_g = d_a // G_A
        ws_p = w_s[l].astype(BF16)
        bias_p = jnp.repeat(b_s[l].T, c_g, axis=1)
        tok = np.arange(CHUNK)
        same_seq = jnp.asarray(tok[:, None] // t_new == tok[None, :] // t_new)
        row_of = jnp.asarray(tok[:, None] % t_new == np.arange(t_new)[None, :], dtype=F32)
        ws_s = jnp.einsum("rt,gts,cs->grc", row_of, w_s[l][:, :t_new, :t_new], row_of,
                          precision=lax.Precision.HIGHEST)
        ws_s = jnp.where(same_seq, ws_s, 0.0).astype(BF16)
        reps = CHUNK // t_new
        bias_s = jnp.tile(jnp.repeat(b_s[l][:, :t_new].T, c_g, axis=1), (reps, 1))

        tm = TOKEN_TILE
        res = _qkv_call(xp, pos_p, g_mix[l], w_in16, qkv_col0, g_q[l], g_k[l], prompt=True, batch=batch, tm=tm)
        q_g, k_g, v_g, kv_tails = res[0:3], res[3:6], res[6:9], res[9:12]
        qs, knt, vnt = _qkv_call(
            xs, pos_s, g_mix[l], w_in16, qkv_col0, g_q[l], g_k[l], prompt=False, batch=1, tm=n_sample)

        q_rows = 8
        assert LANES % t_new == 0 and t_new <= q_rows and N_GROUPS == 3
        q_pad = jnp.pad(qs.reshape(dec_batch, t_new, D_QKV), ((0, 0), (0, q_rows - t_new), (0, 0)))
        bufs = [jnp.transpose(c[l], (0, 1, 3, 4, 2)) for c in caches]
        task = lambda g, n_h: dict(group=g, first_head=0, n_heads=n_h, q=q_pad, knt=knt, vnt=vnt, buf=bufs[g])
        o2, l2, _ = _attn_call(q_g[2], k_g[2], v_g[2], 2, batch, seq, [], t_new)
        o1, l1, ((so1, sl1, nbuf1),) = _attn_call(q_g[1], k_g[1], v_g[1], 1, batch, seq, [task(1, H_G)], t_new)
        o0, l0, ((so0, sl0, nbuf0),) = _attn_call(q_g[0], k_g[0], v_g[0], 0, batch, seq, [task(0, H_G)], t_new)

        xp, va_p, so2, sl2, nbuf2 = _merge_call(
            xp, (o0, o1, o2), (l0, l1, l2), p_prompt[l].reshape(batch * seq, -1), wts, ws_p, bias_p,
            prompt=True, batch=batch, tm=tm, sample_task=task(2, H_G // 2), t_new=t_new)
        s_outs = (so0, so1, so2)
        s_lses = (sl0, sl1, jnp.sum(sl2, axis=1))
        for gi, nbuf in enumerate((nbuf0, nbuf1, nbuf2)):
            kv_s[gi].append(jnp.transpose(nbuf, (0, 1, 4, 2, 3)))
        for gi, st in enumerate(kv_tails):
            kv_p[gi].append(jnp.transpose(st, (0, 1, 4, 2, 3)))
        cv_p.append(va_p)

        s_outs = [o[:, :t_new].reshape(n_sample, D_GROUP) for o in s_outs]
        s_lses = [s[:, :t_new].reshape(n_sample, LANES) for s in s_lses]
        xs, va_s = _merge_call(xs, s_outs, s_lses, p_sample[l].reshape(n_sample, -1), wts, ws_s, bias_s,
                               prompt=False, batch=1, tm=n_sample)
        cv_s.append(va_s.reshape(dec_batch, t_new, d_a))

    return (xp.reshape(batch, seq, d_model), xs.reshape(dec_batch, t_new, d_model),
            jnp.stack(kv_p[0]), jnp.stack(kv_p[1]), jnp.stack(kv_p[2]),
            jnp.stack(kv_s[0]), jnp.stack(kv_s[1]), jnp.stack(kv_s[2]),
            jnp.stack(cv_p), jnp.stack(cv_s))
```

```python
import functools
import math

import jax
import jax.numpy as jnp
import numpy as np
from jax import lax
from jax.experimental import pallas as pl
from jax.experimental.pallas import tpu as pltpu

F32 = jnp.float32
BF16 = jnp.bfloat16

LANES = 128
BF16_ROWS = 16
MXU_DIM = 256
TOKEN_TILE = 256
HEAD_DIM = 64
H_G = 8
HEADS_PER_VREG = LANES // HEAD_DIM
GROUPS = ((128, 1), (512, 4), (2048, 16))
N_GROUPS = len(GROUPS)
D_GROUP = H_G * HEAD_DIM
GROUP_SLABS = D_GROUP // LANES
D_QKV = N_GROUPS * D_GROUP
N_KEYS = 129
SUB_BLOCK = 128
ROT_DIM = HEAD_DIM // 4
ROT_HALF = ROT_DIM // 2
ROPE_THETA = 500000.0
CHUNK = 128
G_A = 4
PAST_LEN = 8192
ATTN_SUB_BLOCKS = 4
EPS = 1e-6
NEG = -1e30
LN2 = math.log(2.0)
LOG2E = 1.0 / LN2
VMEM_LIMIT = 62 * 1024 * 1024


def _rms(x, g):
    return x * lax.rsqrt(jnp.mean(x * x, axis=-1, keepdims=True) + EPS) * g


def _resident(shape):
    zeros = (0,) * len(shape)
    return pl.BlockSpec(shape, lambda *_: zeros, pipeline_mode=pl.Buffered(1))


def _dot(a, b):
    return jnp.dot(a, b, preferred_element_type=F32)


def _dot_nt(a, b):
    return lax.dot_general(a, b, (((1,), (1,)), ((), ())), preferred_element_type=F32)


def _project_qkv(x_ref, gmix_ref, gq_ref, gk_ref, rc_ref, ru_ref, rd_ref, bd_ref, w_refs, perm_refs,
                 dils, emit):
    rows = x_ref.shape[0]
    h = _rms(x_ref[...], gmix_ref[...]).astype(BF16)
    hs = [h if p_ref is None else _dot(p_ref[...], h).astype(BF16) for p_ref in perm_refs]
    bd = bd_ref[...]
    wide = bd.shape[0]

    def table(ref, dil):
        if dil == 1:
            return ref[...]
        return jnp.concatenate([ref[pl.ds(r, rows // dil, stride=dil), :] for r in range(dil)], axis=0)

    tables = {dil: tuple(table(ref, dil) for ref in (rc_ref, ru_ref, rd_ref)) for dil in set(dils)}

    for ti, g_ref in enumerate((gq_ref, gk_ref, None)):
        for gi in range(N_GROUPS):
            z = _dot(hs[gi], w_refs[ti * N_GROUPS + gi][...])
            if g_ref is None:
                emit(ti, gi, [z[:, c * LANES:(c + 1) * LANES] for c in range(GROUP_SLABS)])
                continue
            rc, ru, rd = tables[dils[gi]]

            def rot(zn):
                return zn * rc + pltpu.roll(zn, LANES - ROT_HALF, 1) * ru + pltpu.roll(zn, ROT_HALF, 1) * rd

            vals = []
            for c in range(0, D_GROUP, wide):
                zc = z[:, c:c + wide]
                ss = _dot((zc * zc).astype(BF16), bd)
                zn = zc * lax.rsqrt(ss * (1.0 / HEAD_DIM) + EPS)
                vals += [rot(zn[:, j:j + LANES] * g_ref[...]) for j in range(0, wide, LANES)]
            emit(ti, gi, vals)


def _qkv_prompt_kernel(*refs, tm, n_tiles):
    n_perm = sum(dil > 1 for _, dil in GROUPS)
    n_w = 3 * N_GROUPS
    ins, w_refs, perms, rest = refs[:8], refs[8:8 + n_w], refs[8 + n_w:8 + n_w + n_perm], refs[8 + n_w + n_perm:]
    outs, kv_refs = rest[:3 * N_GROUPS], rest[3 * N_GROUPS:4 * N_GROUPS]
    kv_slabs, order_slab = rest[4 * N_GROUPS:4 * N_GROUPS + 2], rest[4 * N_GROUPS + 2]
    perms = list(perms)
    perm_refs = [perms.pop(0) if dil > 1 else None for _, dil in GROUPS]

    def emit(ti, gi, vals):
        out = outs[ti * N_GROUPS + gi]
        for cc, val in enumerate(vals):
            out[:, cc * LANES:(cc + 1) * LANES] = val.astype(BF16)
            if ti > 0:
                kv_slabs[ti - 1][gi * GROUP_SLABS + cc] = val

    _project_qkv(*ins, w_refs, perm_refs, [dil for _, dil in GROUPS], emit)

    i = pl.program_id(1)
    seq = n_tiles * tm
    for gi, (win, dil) in enumerate(GROUPS):
        rows = min(win, tm)
        first = (seq - win) // tm if win >= tm else n_tiles - 1
        piece = tm // dil

        @pl.when(i >= first)
        def _(gi=gi, dil=dil, rows=rows, piece=piece):
            for kv in range(2):
                for cc in range(GROUP_SLABS):
                    c = gi * GROUP_SLABS + cc
                    if dil > 1:
                        for r in range(dil):
                            order_slab[pl.ds(r, piece, stride=dil), :] = kv_slabs[kv][c, r * piece:(r + 1) * piece, :]
                        val = order_slab[...]
                    else:
                        val = kv_slabs[kv][c]
                    t = val[tm - rows:tm].T
                    for hh in range(HEADS_PER_VREG):
                        kv_refs[gi][0, kv, cc * HEADS_PER_VREG + hh] = t[hh * HEAD_DIM:(hh + 1) * HEAD_DIM, :]


def _qkv_sample_kernel(*refs):
    n_w = 3 * N_GROUPS
    ins, w_refs, outs = refs[:8], refs[8:8 + n_w], refs[8 + n_w:11 + n_w]

    def emit(ti, gi, vals):
        for cc, val in enumerate(vals):
            c = gi * GROUP_SLABS + cc
            if ti == 0:
                outs[ti][:, c * LANES:(c + 1) * LANES] = val
            else:
                outs[ti][c * LANES:(c + 1) * LANES, :] = val.T

    _project_qkv(*ins, w_refs, [None] * N_GROUPS, [1] * N_GROUPS, emit)


def _residue_major_source(dil, tm):
    row = np.arange(tm, dtype=np.int32)
    piece = tm // dil
    return (row % piece) * dil + row // piece


def _rotary_tables(pos):
    inv = np.float32(ROPE_THETA) ** (-np.arange(ROT_HALF, dtype=np.float32) * np.float32(2.0 / ROT_DIM))
    ang = pos.astype(np.float32)[:, None] * inv[None, :]
    cos, sin = np.cos(ang), np.sin(ang)
    n = pos.shape[0]
    pad = np.zeros((n, HEAD_DIM - ROT_DIM), np.float32)
    zero = np.zeros((n, ROT_HALF), np.float32)
    rc = np.concatenate([cos, cos, pad + 1.0], axis=1)
    ru = np.concatenate([-sin, zero, pad], axis=1)
    rd = np.concatenate([zero, sin, pad], axis=1)
    return [jnp.asarray(np.tile(t, (1, HEADS_PER_VREG)).astype(np.float32)) for t in (rc, ru, rd)]


def _column_block(n_rows, width, index):
    return pl.BlockSpec((n_rows, width), lambda *_: (0, index), pipeline_mode=pl.Buffered(1))


def _qkv_call(x2, pos, g_mix, w_in, qkv_col0, g_q, g_k, *, prompt, batch, tm):
    n_tok, d_model = x2.shape
    seq = n_tok // batch
    n_tiles = seq // tm
    perms = []
    if prompt:
        assert all(tm % (dil * BF16_ROWS) == 0 for _, dil in GROUPS)
        for _, dil in GROUPS:
            if dil > 1:
                src = _residue_major_source(dil, tm)
                perms.append(jnp.asarray(src[:, None] == np.arange(tm)[None, :], dtype=BF16))
    rot = _rotary_tables(pos)
    gq = jnp.tile(g_q.astype(F32), HEADS_PER_VREG)[None, :] * (LOG2E * HEAD_DIM ** -0.5)
    gk = jnp.tile(g_k.astype(F32), HEADS_PER_VREG)[None, :]
    lane = np.arange(MXU_DIM)
    bd = jnp.asarray(lane[:, None] // HEAD_DIM == lane[None, :] // HEAD_DIM, dtype=BF16)

    tile_index = lambda b, i: b * n_tiles + i
    tok = lambda b, i: (tile_index(b, i), 0)
    in_specs = [
        pl.BlockSpec((tm, d_model), tok),
        _resident((1, d_model)),
        _resident((1, LANES)),
        _resident((1, LANES)),
        pl.BlockSpec((tm, LANES), lambda b, i: (i, 0)),
        pl.BlockSpec((tm, LANES), lambda b, i: (i, 0)),
        pl.BlockSpec((tm, LANES), lambda b, i: (i, 0)),
        _resident((MXU_DIM, MXU_DIM)),
    ] + [_column_block(d_model, D_GROUP, qkv_col0 // D_GROUP + j) for j in range(3 * N_GROUPS)]
    in_specs += [_resident(p.shape) for p in perms]
    assert qkv_col0 % D_GROUP == 0
    if prompt:
        out_shape = [jax.ShapeDtypeStruct((n_tok, D_GROUP), BF16)] * (3 * N_GROUPS)
        out_specs = [pl.BlockSpec((tm, D_GROUP), tok)] * (3 * N_GROUPS)
        for win, _ in GROUPS:
            rows = min(win, tm)
            first = (seq - win) // tm if win >= tm else n_tiles - 1
            out_shape.append(jax.ShapeDtypeStruct((batch, 2, H_G, HEAD_DIM, win), F32))
            out_specs.append(pl.BlockSpec(
                (1, 2, H_G, HEAD_DIM, rows),
                lambda b, i, first=first: (b, 0, 0, 0, jnp.maximum(i - first, 0))))
        scratch = [pltpu.VMEM((D_QKV // LANES, tm, LANES), F32)] * 2 + [pltpu.VMEM((tm, LANES), F32)]
        body = functools.partial(_qkv_prompt_kernel, tm=tm, n_tiles=n_tiles)
    else:
        assert n_tiles == 1 and batch == 1 and tm == LANES
        out_shape = [jax.ShapeDtypeStruct((n_tok, D_QKV), F32)] + [jax.ShapeDtypeStruct((D_QKV, n_tok), F32)] * 2
        out_specs = [pl.BlockSpec((tm, D_QKV), tok)] + [pl.BlockSpec((D_QKV, tm), lambda b, i: (0, 0))] * 2
        scratch = []
        body = _qkv_sample_kernel

    return pl.pallas_call(
        body,
        grid=(batch, n_tiles),
        in_specs=in_specs,
        out_specs=out_specs,
        out_shape=out_shape,
        scratch_shapes=scratch,
        compiler_params=pltpu.CompilerParams(
            dimension_semantics=("arbitrary", "arbitrary"), vmem_limit_bytes=VMEM_LIMIT),
        name="qkv_prompt" if prompt else "qkv_sample",
    )(x2, g_mix[None, :], gq, gk, *rot, bd, *[w_in] * (3 * N_GROUPS), *perms)


def _attn_kernel(q_ref, kc_ref, kp_ref, vc_ref, vp_ref, *rest, dil, tasks, n_alias, t_new):
    n_in = 4 * len(tasks)
    n_skip = n_in + n_alias
    task_ins, (o_ref, lse_ref), task_outs = rest[:n_in], rest[n_skip:n_skip + 2], rest[n_skip + 2:]
    step = (pl.program_id(0) * pl.num_programs(1) + pl.program_id(1)) * pl.num_programs(2) + pl.program_id(2)

    has_prev = pl.program_id(1) > 0
    piece = TOKEN_TILE // dil

    def residue_rows(i):
        return [(t, 0, slice(i * piece, (i + 1) * piece)) for t in range(SUB_BLOCK // piece)]

    def load(ref, rows, sl):
        parts = [ref[rs + (sl,)] for rs in rows]
        return parts[0] if len(parts) == 1 else jnp.concatenate(parts, axis=0)

    def store(ref, rows, sl, val):
        n = val.shape[0] // len(rows)
        for t, rs in enumerate(rows):
            ref[rs + (sl,)] = val[t * n:(t + 1) * n]
    key = lax.broadcasted_iota(jnp.int32, (2 * SUB_BLOCK, HEADS_PER_VREG * SUB_BLOCK), 0)
    qry = lax.broadcasted_iota(jnp.int32, (2 * SUB_BLOCK, HEADS_PER_VREG * SUB_BLOCK), 1) % SUB_BLOCK
    vis_cur = key <= qry
    vis_prev = key - SUB_BLOCK >= qry
    lane16 = lax.broadcasted_iota(jnp.int32, (SUB_BLOCK, LANES), 1).astype(F32).astype(BF16)
    head_lanes = (lane16 < HEAD_DIM, lane16 >= HEAD_DIM)
    eye_r = lax.broadcasted_iota(jnp.int32, (LANES, LANES), 0)
    eye_c = lax.broadcasted_iota(jnp.int32, (LANES, LANES), 1)
    eye = (eye_r == eye_c).astype(F32).astype(BF16)
    ones_rows = jnp.ones((BF16_ROWS, 2 * SUB_BLOCK), BF16)
    head_row = lax.broadcasted_iota(jnp.int32, (H_G, SUB_BLOCK), 0)

    pad = jnp.zeros((LANES - H_G, SUB_BLOCK), F32)
    slabs = [slice(p * LANES, (p + 1) * LANES) for p in range(GROUP_SLABS)]

    if dil == 1:
        blocks = [([(slice(0, SUB_BLOCK),)], kp_ref, vp_ref, [(slice(0, SUB_BLOCK),)], has_prev)]
        blocks += [([(slice(i * SUB_BLOCK, (i + 1) * SUB_BLOCK),)], kc_ref, vc_ref,
                    [(slice((i - 1) * SUB_BLOCK, i * SUB_BLOCK),)], True) for i in range(1, ATTN_SUB_BLOCKS)]
    else:
        blocks = [(residue_rows(i), kp_ref, vp_ref, residue_rows(i), has_prev)
                  for i in range(ATTN_SUB_BLOCKS)]

    def score_stage(blks):
        scores, vts = [], []
        for rows, kp_src, vp_src, prev_rows, _ in blks:
            for sl in slabs:
                q = load(q_ref, rows, sl)
                qq = jnp.concatenate([jnp.where(hl, q, jnp.zeros_like(q)) for hl in head_lanes], axis=0)
                kk = jnp.concatenate([load(kc_ref, rows, sl), load(kp_src, prev_rows, sl)], axis=0)
                vv = jnp.concatenate([load(vc_ref, rows, sl), load(vp_src, prev_rows, sl)], axis=0)
                scores.append(_dot_nt(kk, qq))
                vts.append(jnp.concatenate([_dot_nt(eye, vv).astype(BF16), ones_rows], axis=0))
        return scores, vts

    def softmax_stage(blks, scores):
        maxes, probs = [], []
        for bi, (_, _, _, _, has_prev) in enumerate(blks):
            visible = jnp.logical_or(vis_cur, jnp.logical_and(vis_prev, has_prev))
            for s in scores[bi * GROUP_SLABS:(bi + 1) * GROUP_SLABS]:
                s = jnp.where(visible, s, NEG)
                m = jnp.max(s, axis=0, keepdims=True)
                maxes.append(m)
                probs.append(jnp.exp2(s - m).astype(BF16))
        return maxes, probs

    def value_stage(vts, probs):
        return [_dot(vt, e) for vt, e in zip(vts, probs)]

    def store_stage(blks, outs, maxes):
        for bi, (rows, _, _, _, _) in enumerate(blks):
            lse_rows = jnp.zeros((H_G, SUB_BLOCK), F32)
            for p, sl in enumerate(slabs):
                ot, m = outs[bi * GROUP_SLABS + p], maxes[bi * GROUP_SLABS + p]
                den = ot[LANES:LANES + 1, :]
                lse = m * LN2 + jnp.log(den)
                inv = 1.0 / den
                o_halves = []
                for half in range(HEADS_PER_VREG):
                    qs = slice(half * SUB_BLOCK, (half + 1) * SUB_BLOCK)
                    o_halves.append(ot[half * HEAD_DIM:(half + 1) * HEAD_DIM, qs] * inv[:, qs])
                    lse_rows = jnp.where(head_row == p * HEADS_PER_VREG + half, lse[:, qs], lse_rows)
                store(o_ref, rows, sl, jnp.concatenate(o_halves, axis=0).T.astype(BF16))
            store(lse_ref, rows, slice(None), jnp.concatenate([lse_rows, pad], axis=0).T)

    first, second = blocks[:ATTN_SUB_BLOCKS // 2], blocks[ATTN_SUB_BLOCKS // 2:]
    scores_a, vts_a = score_stage(first)
    scores_b, vts_b = score_stage(second)
    for ti, (task_dil, first_head) in enumerate(tasks):
        _sample_attention(*task_ins[4 * ti:4 * ti + 4], *task_outs[3 * ti:3 * ti + 3], step,
                          dil=task_dil, first_head=first_head, t_new=t_new)
    maxes_a, probs_a = softmax_stage(first, scores_a)
    outs_a = value_stage(vts_a, probs_a)
    maxes_b, probs_b = softmax_stage(second, scores_b)
    outs_b = value_stage(vts_b, probs_b)
    store_stage(first, outs_a, maxes_a)
    store_stage(second, outs_b, maxes_b)


def _attn_call(q, k, v, gi, batch, seq, sample_tasks, t_new):
    dil = GROUPS[gi][1]
    n_tok = batch * seq
    if dil == 1:
        rows, parts = ATTN_SUB_BLOCKS * SUB_BLOCK, 1
        assert seq % rows == 0
        steps = seq // rows
        view = lambda width: (n_tok, width)
        cur_map = lambda b, s, p: (b * steps + s, 0)
        cur = lambda width: pl.BlockSpec((rows, width), cur_map)
        prev = pl.BlockSpec((SUB_BLOCK, D_GROUP),
                            lambda b, s, p: (jnp.maximum((b * steps + s) * ATTN_SUB_BLOCKS - 1, 0), 0))
    else:
        span, parts = SUB_BLOCK * dil, dil // ATTN_SUB_BLOCKS
        assert dil % ATTN_SUB_BLOCKS == 0 and span % TOKEN_TILE == 0 and seq % span == 0
        assert TOKEN_TILE % (dil * BF16_ROWS) == 0
        steps, span_tiles, part_rows = seq // span, span // TOKEN_TILE, TOKEN_TILE // parts
        view = lambda width: (n_tok // TOKEN_TILE, parts, part_rows, width)
        cur_map = lambda b, s, p: (b * steps + s, p, 0, 0)
        cur = lambda width: pl.BlockSpec((span_tiles, 1, part_rows, width), cur_map)
        prev = pl.BlockSpec((span_tiles, 1, part_rows, D_GROUP),
                            lambda b, s, p: (b * steps + jnp.maximum(s - 1, 0), p, 0, 0))
    q, k, v = (t.reshape(view(D_GROUP)) for t in (q, k, v))
    in_specs = [cur(D_GROUP), cur(D_GROUP), prev, cur(D_GROUP), prev]
    out_specs = [cur(D_GROUP), cur(LANES)]
    out_shape = [jax.ShapeDtypeStruct(view(D_GROUP), BF16), jax.ShapeDtypeStruct(view(LANES), F32)]
    operands, alias_operands, aliases = [q, k, k, v, v], [], {}
    seq_of = lambda b, s, p: (b * steps + s) * parts + p
    for task in sample_tasks:
        n_seq, q_rows, _ = task["q"].shape
        assert n_seq == batch * steps * parts
        wb = task["buf"].shape[4]
        n_h, h0 = task["n_heads"], task["first_head"]
        cols = n_h * HEAD_DIM
        col_block = (task["group"] * D_GROUP + h0 * HEAD_DIM) // cols
        buf_spec = pl.BlockSpec((1, 2, n_h, HEAD_DIM, wb), lambda b, s, p, hb=h0 // n_h: (seq_of(b, s, p), 0, hb, 0, 0))
        new_spec = pl.BlockSpec((cols, LANES), lambda b, s, p, cb=col_block: (cb, 0))
        in_specs += [pl.BlockSpec((1, q_rows, cols), lambda b, s, p, cb=col_block: (seq_of(b, s, p), 0, cb)),
                     new_spec, new_spec, buf_spec]
        operands += [task["q"], task["knt"], task["vnt"], task["buf"]]
        if task.get("nbuf") is not None:
            aliases[len(alias_operands)] = len(out_shape) + 2
            alias_operands.append(task["nbuf"])
        out_specs += [pl.BlockSpec((1, q_rows, cols), lambda b, s, p: (seq_of(b, s, p), 0, 0)),
                      pl.BlockSpec((1, q_rows, LANES), lambda b, s, p: (seq_of(b, s, p), 0, 0)), buf_spec]
        out_shape += [jax.ShapeDtypeStruct((n_seq, q_rows, cols), F32),
                      jax.ShapeDtypeStruct((n_seq, q_rows, LANES), F32),
                      jax.ShapeDtypeStruct(task["buf"].shape, F32)]
    n_main = len(operands)
    res = pl.pallas_call(
        functools.partial(_attn_kernel, dil=dil, n_alias=len(alias_operands), t_new=t_new,
                          tasks=tuple((GROUPS[t["group"]][1], t["first_head"]) for t in sample_tasks)),
        grid=(batch, steps, parts),
        in_specs=in_specs + [pl.BlockSpec(memory_space=pl.ANY)] * len(alias_operands),
        out_specs=out_specs,
        out_shape=out_shape,
        input_output_aliases={n_main + i: o for i, o in aliases.items()},
        compiler_params=pltpu.CompilerParams(
            dimension_semantics=("arbitrary", "arbitrary", "arbitrary"), vmem_limit_bytes=VMEM_LIMIT),
        name=f"attn_g{gi}",
    )(*operands, *alias_operands)
    task_res = [tuple(res[2 + 3 * i:5 + 3 * i]) for i in range(len(sample_tasks))]
    return res[0].reshape(n_tok, D_GROUP), res[1].reshape(n_tok, LANES), task_res


def _sample_attention(q_ref, knt_ref, vnt_ref, buf_ref, o_ref, lse_ref, nbuf_ref, seq_index, *,
                      dil, first_head, t_new):
    assert dil & (dil - 1) == 0
    q_rows = q_ref.shape[1]
    n_heads, wb = buf_ref.shape[2], buf_ref.shape[4]
    width = wb + LANES
    shift = lax.rem(LANES - t_new * seq_index, LANES)
    knt = pltpu.roll(knt_ref[...], shift, 1)
    vnt = pltpu.roll(vnt_ref[...], shift, 1)
    q = q_ref[0]

    t_row = lax.broadcasted_iota(jnp.int32, (q_rows, width), 0)
    p_col = lax.broadcasted_iota(jnp.int32, (q_rows, width), 1)
    back = wb + t_row - p_col
    valid = jnp.logical_and(jnp.logical_and(back >= 0, (back & (dil - 1)) == 0),
                            jnp.logical_and(back <= (N_KEYS - 1) * dil, p_col < wb + t_new))
    valid = jnp.concatenate([valid] * n_heads, axis=0)
    lane = lax.broadcasted_iota(jnp.int32, (q_rows, LANES), 1)
    cols = n_heads * HEAD_DIM

    kx = jnp.concatenate([buf_ref[0, 0].reshape(cols, wb), knt], axis=1)
    vx = jnp.concatenate([buf_ref[0, 1].reshape(cols, wb), vnt], axis=1)
    nbuf_ref[0, 0] = pltpu.roll(kx, width - t_new, 1)[:, :wb].reshape(n_heads, HEAD_DIM, wb)
    nbuf_ref[0, 1] = pltpu.roll(vx, width - t_new, 1)[:, :wb].reshape(n_heads, HEAD_DIM, wb)

    q_all = jnp.concatenate([q] * n_heads, axis=0)
    r_head = lax.broadcasted_iota(jnp.int32, q_all.shape, 0) // q_rows
    c_head = lax.broadcasted_iota(jnp.int32, q_all.shape, 1) // HEAD_DIM
    q_all = jnp.where(r_head == c_head, q_all, 0.0).astype(BF16)
    s = jnp.where(valid, _dot(q_all, kx.astype(BF16)), NEG)
    m = jnp.max(s, axis=1, keepdims=True)
    e = jnp.exp2(s - m)
    den = jnp.sum(e, axis=1, keepdims=True)
    o_all = _dot_nt(e.astype(BF16), vx.astype(BF16)) / den
    lse = m * LN2 + jnp.log(den)
    o_ref[0] = jnp.concatenate(
        [o_all[h * q_rows:(h + 1) * q_rows, h * HEAD_DIM:(h + 1) * HEAD_DIM] for h in range(n_heads)], axis=1)
    lse_ref[0] = functools.reduce(jnp.add, [
        jnp.where(lane == first_head + h, lse[h * q_rows:(h + 1) * q_rows], 0.0) for h in range(n_heads)])


def _merge_kernel(x_ref, o0_ref, o1_ref, o2_ref, l0_ref, l1_ref, l2_ref, pe_ref,
                  gmix_ref, wu_ref, wva_ref, wga0_ref, wga1_ref, wgb0_ref, wgb1_ref,
                  gv_ref, ws_ref, bias_ref, woa_ref, wob_ref, wo_ref,
                  gffn_ref, wff1_ref, wff2_ref, gple_ref, wpg_ref, wpp_ref,
                  *rest, prompt, tm, n_tiles, sample):
    if sample is None:
        (y_ref, va_ref), scratch = rest[:2], rest[2:]
    else:
        task_ins, (y_ref, va_ref), task_outs, scratch = rest[:4], rest[4:6], rest[6:9], rest[9:]
    d_model = x_ref.shape[1]
    d_a = gv_ref.shape[1]
    c_g = d_a // G_A
    x = x_ref[...]
    h = _rms(x, gmix_ref[...]).astype(BF16)

    u = jax.nn.gelu(_dot(h, wu_ref[...]))
    va = _rms(jax.nn.gelu(_dot(h, wva_ref[...])), gv_ref[...])
    if prompt:
        @pl.when(pl.program_id(1) == n_tiles - 1)
        def _():
            va_ref[0] = va[tm - CHUNK:tm, :]
    else:
        va_ref[...] = va
    va16 = va.astype(BF16)
    r_i = lax.broadcasted_iota(jnp.int32, (CHUNK, CHUNK), 0)
    c_i = lax.broadcasted_iota(jnp.int32, (CHUNK, CHUNK), 1)
    w_tril = [jnp.where(r_i >= c_i, ws_ref[g], jnp.zeros((CHUNK, CHUNK), BF16)) for g in range(G_A)]
    mix_rows = []
    for ci in range(tm // CHUNK):
        rs = slice(ci * CHUNK, (ci + 1) * CHUNK)
        cols = [_dot(w_tril[g], va16[rs, g * c_g:(g + 1) * c_g]) for g in range(G_A)]
        mix_rows.append(jnp.concatenate(cols, axis=1) + bias_ref[...])
    a = u * jnp.concatenate(mix_rows, axis=0)

    o_refs, l_refs = (o0_ref, o1_ref, o2_ref), (l0_ref, l1_ref, l2_ref)
    o_slabs, lses = [], []
    for gi, (_, dil) in enumerate(GROUPS):
        if not prompt or dil == 1:
            o_slabs.append([o_refs[gi][:, cc * LANES:(cc + 1) * LANES].astype(F32)
                            for cc in range(GROUP_SLABS)])
            lses.append(l_refs[gi][...])
            continue
        o_sc, l_sc = scratch[2 * (gi - 1)], scratch[2 * (gi - 1) + 1]
        piece = tm // dil
        for r in range(dil):
            rows = slice(r * piece, (r + 1) * piece)
            for cc in range(GROUP_SLABS):
                o_sc[cc, pl.ds(r, piece, stride=dil), :] = (
                    o_refs[gi][rows, cc * LANES:(cc + 1) * LANES].astype(F32))
            l_sc[pl.ds(r, piece, stride=dil), :] = l_refs[gi][rows, :]
        o_slabs.append([o_sc[cc] for cc in range(GROUP_SLABS)])
        lses.append(l_sc[...])

    m = jnp.maximum(jnp.maximum(lses[0], lses[1]), lses[2])
    ws = [jnp.exp(l - m) for l in lses]
    inv = 1.0 / (ws[0] + ws[1] + ws[2])
    ws = [w * inv for w in ws]
    lane = lax.broadcasted_iota(jnp.int32, (tm, LANES), 1)
    first_head = lane < HEAD_DIM
    b_cols = []
    for p in range(GROUP_SLABS):
        acc = jnp.zeros((tm, LANES), F32)
        for w, slabs in zip(ws, o_slabs):
            w_pair = jnp.where(first_head, w[:, 2 * p:2 * p + 1], w[:, 2 * p + 1:2 * p + 2])
            acc = acc + w_pair * slabs[p]
        b_cols.append(acc)
    b = jnp.concatenate(b_cols, axis=1)

    gate_a = jax.nn.sigmoid(jnp.concatenate([_dot(h, wga0_ref[...]), _dot(h, wga1_ref[...])], axis=1))
    gate_b = jax.nn.sigmoid(jnp.concatenate([_dot(h, wgb0_ref[...]), _dot(h, wgb1_ref[...])], axis=1))
    merged = gate_a * _dot(a.astype(BF16), woa_ref[...]) + gate_b * _dot(b.astype(BF16), wob_ref[...])
    x = x + _dot(merged.astype(BF16), wo_ref[...])

    h2 = _rms(x, gffn_ref[...]).astype(BF16)
    d_ff = wff1_ref.shape[1]
    ff_step = 1024
    ffn = jnp.zeros((tm, d_model), F32)
    for c0 in range(0, d_ff, ff_step):
        hid = jnp.square(jnp.maximum(_dot(h2, wff1_ref[:, c0:c0 + ff_step]), 0.0))
        ffn = ffn + _dot(hid.astype(BF16), wff2_ref[c0:c0 + ff_step, :])
    x = x + ffn

    if sample is not None:
        parts = H_G // task_ins[3].shape[2]
        step = pl.program_id(0) * n_tiles + pl.program_id(1)
        _sample_attention(*task_ins, *task_outs, step // parts, dil=sample[0],
                          first_head=(step % parts) * (H_G // parts), t_new=sample[1])

    gate = jax.nn.sigmoid(_dot(_rms(x, gple_ref[...]).astype(BF16), wpg_ref[...]))
    y_ref[...] = x + gate * _dot(pe_ref[...].astype(BF16), wpp_ref[...])


def _merge_call(x2, outs, lses, pe2, wts, ws_mat, bias, *, prompt, batch, tm, sample_task=None, t_new=None):
    n_tok, d_model = x2.shape
    seq = n_tok // batch
    n_tiles = seq // tm
    d_a = wts["g_v"].shape[1]
    tile_index = lambda b, i: b * n_tiles + i
    tok = lambda b, i: (tile_index(b, i), 0)
    row_spec = lambda width: pl.BlockSpec((tm, width), tok)

    names = ("g_mix", "w_in", "w_in", "w_in", "w_in", "w_in", "w_in", "g_v", "ws", "bias", "w_oa", "w_ob", "w_o",
             "g_ffn", "w_ff1", "w_ff2", "g_ple", "w_ple_gate", "w_ple_proj")
    consts = dict(wts, ws=ws_mat, bias=bias)
    half_model = d_model // 2
    assert d_a == half_model and wts["gate_col0"] % half_model == 0
    g0 = wts["gate_col0"] // half_model
    w_in_blocks = iter([0, 1, g0, g0 + 1, g0 + 2, g0 + 3])
    const_specs = [_column_block(d_model, half_model, next(w_in_blocks)) if k == "w_in"
                   else _resident(consts[k].shape) for k in names]
    in_specs = ([row_spec(d_model)]
                + [row_spec(D_GROUP)] * N_GROUPS + [row_spec(LANES)] * N_GROUPS
                + [row_spec(pe2.shape[1])] + const_specs)
    scratch = []
    if prompt:
        va_shape = jax.ShapeDtypeStruct((batch, CHUNK, d_a), F32)
        va_spec = pl.BlockSpec((1, CHUNK, d_a), lambda b, i: (b, 0, 0))
        for _, dil in GROUPS:
            if dil > 1:
                scratch += [pltpu.VMEM((GROUP_SLABS, tm, LANES), F32), pltpu.VMEM((tm, LANES), F32)]
    else:
        va_shape = jax.ShapeDtypeStruct((n_tok, d_a), F32)
        va_spec = row_spec(d_a)
    operands = [x2, *outs, *lses, pe2, *[consts[k] for k in names]]
    out_specs = [row_spec(d_model), va_spec]
    out_shape = [jax.ShapeDtypeStruct((n_tok, d_model), F32), va_shape]
    sample = None
    if sample_task is not None:
        task = sample_task
        n_seq, q_rows, _ = task["q"].shape
        n_h, wb = task["n_heads"], task["buf"].shape[4]
        parts = H_G // n_h
        assert n_seq * parts == batch * n_tiles
        cols = n_h * HEAD_DIM
        col0 = task["group"] * D_GROUP // cols
        seq_of = lambda b, i: tile_index(b, i) // parts
        part_of = lambda b, i: tile_index(b, i) % parts
        buf_spec = pl.BlockSpec((1, 2, n_h, HEAD_DIM, wb), lambda b, i: (seq_of(b, i), 0, part_of(b, i), 0, 0))
        new_spec = pl.BlockSpec((cols, LANES), lambda b, i: (col0 + part_of(b, i), 0))
        in_specs += [pl.BlockSpec((1, q_rows, cols), lambda b, i: (seq_of(b, i), 0, col0 + part_of(b, i))),
                     new_spec, new_spec, buf_spec]
        operands += [task["q"], task["knt"], task["vnt"], task["buf"]]
        out_specs += [pl.BlockSpec((1, q_rows, cols), lambda b, i: (seq_of(b, i), 0, part_of(b, i))),
                      pl.BlockSpec((1, None, q_rows, LANES), lambda b, i: (seq_of(b, i), part_of(b, i), 0, 0)),
                      buf_spec]
        out_shape += [jax.ShapeDtypeStruct((n_seq, q_rows, D_GROUP), F32),
                      jax.ShapeDtypeStruct((n_seq, parts, q_rows, LANES), F32),
                      jax.ShapeDtypeStruct(task["buf"].shape, F32)]
        sample = (GROUPS[task["group"]][1], t_new)
    return pl.pallas_call(
        functools.partial(_merge_kernel, prompt=prompt, tm=tm, n_tiles=n_tiles, sample=sample),
        grid=(batch, n_tiles),
        in_specs=in_specs,
        out_specs=out_specs,
        out_shape=out_shape,
        scratch_shapes=scratch,
        compiler_params=pltpu.CompilerParams(
            dimension_semantics=("arbitrary", "arbitrary"), vmem_limit_bytes=VMEM_LIMIT),
        name="merge_prompt" if prompt else "merge_sample",
    )(*operands)


def kernel(x_prompt, x_sample, p_prompt, p_sample, cache_kv_w128, cache_kv_w512, cache_kv_w2048,
           g_mix, w_in, g_v, w_s, b_s, g_q, g_k, w_oa, w_ob, w_o,
           g_ffn, w_ff1, w_ff2, g_ple, w_ple_gate, w_ple_proj):
    depth = w_in.shape[0]
    batch, seq, d_model = x_prompt.shape
    dec_batch, t_new, _ = x_sample.shape
    d_a = g_v.shape[1]
    n_sample = dec_batch * t_new
    assert n_sample == CHUNK and w_in.shape[2] == 2 * d_a + 3 * D_QKV + 2 * d_model
    caches = (cache_kv_w128, cache_kv_w512, cache_kv_w2048)

    xp = x_prompt.reshape(batch * seq, d_model)
    xs = x_sample.reshape(n_sample, d_model)
    pos_p = np.arange(seq, dtype=np.int32)
    pos_s = PAST_LEN + np.tile(np.arange(t_new, dtype=np.int32), dec_batch)

    kv_p = [[] for _ in GROUPS]
    kv_s = [[] for _ in GROUPS]
    cv_p, cv_s = [], []
    for l in range(depth):
        w_in16 = w_in[l].astype(BF16)
        qkv_col0 = 2 * d_a
        wts = {
            "g_mix": g_mix[l][None, :], "w_in": w_in16, "gate_col0": qkv_col0 + 3 * D_QKV,
            "g_v": g_v[l][None, :],
            "w_oa": w_oa[l].astype(BF16), "w_ob": w_ob[l].astype(BF16), "w_o": w_o[l].astype(BF16),
            "g_ffn": g_ffn[l][None, :], "w_ff1": w_ff1[l].astype(BF16), "w_ff2": w_ff2[l].astype(BF16),
            "g_ple": g_ple[l][None, :], "w_ple_gate": w_ple_gate[l].astype(BF16),
            "w_ple_proj": w_ple_proj[l].astype(BF16),
        }
        c_g = d_a // G_A
        ws_p = w_s[l].astype(BF16)
        bias_p = jnp.repeat(b_s[l].T, c_g, axis=1)
        tok = np.arange(CHUNK)
        same_seq = jnp.asarray(tok[:, None] // t_new == tok[None, :] // t_new)
        row_of = jnp.asarray(tok[:, None] % t_new == np.arange(t_new)[None, :], dtype=F32)
        ws_s = jnp.einsum("rt,gts,cs->grc", row_of, w_s[l][:, :t_new, :t_new], row_of,
                          precision=lax.Precision.HIGHEST)
        ws_s = jnp.where(same_seq, ws_s, 0.0).astype(BF16)
        reps = CHUNK // t_new
        bias_s = jnp.tile(jnp.repeat(b_s[l][:, :t_new].T, c_g, axis=1), (reps, 1))

        tm = TOKEN_TILE
        res = _qkv_call(xp, pos_p, g_mix[l], w_in16, qkv_col0, g_q[l], g_k[l], prompt=True, batch=batch, tm=tm)
        q_g, k_g, v_g, kv_tails = res[0:3], res[3:6], res[6:9], res[9:12]
        qs, knt, vnt = _qkv_call(
            xs, pos_s, g_mix[l], w_in16, qkv_col0, g_q[l], g_k[l], prompt=False, batch=1, tm=n_sample)

        q_rows = 8
        assert LANES % t_new == 0 and t_new <= q_rows and N_GROUPS == 3
        q_pad = jnp.pad(qs.reshape(dec_batch, t_new, D_QKV), ((0, 0), (0, q_rows - t_new), (0, 0)))
        bufs = [jnp.transpose(c[l], (0, 1, 3, 4, 2)) for c in caches]
        task = lambda g, n_h, h0=0, **kw: dict(group=g, first_head=h0, n_heads=n_h, q=q_pad, knt=knt, vnt=vnt,
                                               buf=bufs[g], **kw)
        half = H_G // 2
        o2, l2, ((so1a, sl1a, nbuf1),) = _attn_call(q_g[2], k_g[2], v_g[2], 2, batch, seq,
                                                  [task(1, half)], t_new)
        o1, l1, ((so1b, sl1b, nbuf1),) = _attn_call(q_g[1], k_g[1], v_g[1], 1, batch, seq,
                                                  [task(1, half, half, nbuf=nbuf1)], t_new)
        so1, sl1 = jnp.concatenate([so1a, so1b], axis=2), sl1a + sl1b
        o0, l0, ((so0, sl0, nbuf0),) = _attn_call(q_g[0], k_g[0], v_g[0], 0, batch, seq, [task(0, H_G)], t_new)

        xp, va_p, so2, sl2, nbuf2 = _merge_call(
            xp, (o0, o1, o2), (l0, l1, l2), p_prompt[l].reshape(batch * seq, -1), wts, ws_p, bias_p,
            prompt=True, batch=batch, tm=tm, sample_task=task(2, H_G // 2), t_new=t_new)
        s_outs = (so0, so1, so2)
        s_lses = (sl0, sl1, jnp.sum(sl2, axis=1))
        for gi, nbuf in enumerate((nbuf0, nbuf1, nbuf2)):
            kv_s[gi].append(jnp.transpose(nbuf, (0, 1, 4, 2, 3)))
        for gi, st in enumerate(kv_tails):
            kv_p[gi].append(jnp.transpose(st, (0, 1, 4, 2, 3)))
        cv_p.append(va_p)

        s_outs = [o[:, :t_new].reshape(n_sample, D_GROUP) for o in s_outs]
        s_lses = [s[:, :t_new].reshape(n_sample, LANES) for s in s_lses]
        xs, va_s = _merge_call(xs, s_outs, s_lses, p_sample[l].reshape(n_sample, -1), wts, ws_s, bias_s,
                               prompt=False, batch=1, tm=n_sample)
        cv_s.append(va_s.reshape(dec_batch, t_new, d_a))

    return (xp.reshape(batch, seq, d_model), xs.reshape(dec_batch, t_new, d_model),
            jnp.stack(kv_p[0]), jnp.stack(kv_p[1]), jnp.stack(kv_p[2]),
            jnp.stack(kv_s[0]), jnp.stack(kv_s[1]), jnp.stack(kv_s[2]),
            jnp.stack(cv_p), jnp.stack(cv_s))
```

```python
import functools
import math

import jax
import jax.numpy as jnp
import numpy as np
from jax import lax
from jax.experimental import pallas as pl
from jax.experimental.pallas import tpu as pltpu

F32 = jnp.float32
BF16 = jnp.bfloat16

LANES = 128
BF16_ROWS = 16
MXU_DIM = 256
TOKEN_TILE = 256
QKV_TILE = 512
HEAD_DIM = 64
H_G = 8
HEADS_PER_VREG = LANES // HEAD_DIM
GROUPS = ((128, 1), (512, 4), (2048, 16))
N_GROUPS = len(GROUPS)
D_GROUP = H_G * HEAD_DIM
GROUP_SLABS = D_GROUP // LANES
D_QKV = N_GROUPS * D_GROUP
N_KEYS = 129
SUB_BLOCK = 128
ROT_DIM = HEAD_DIM // 4
ROT_HALF = ROT_DIM // 2
ROPE_THETA = 500000.0
CHUNK = 128
G_A = 4
PAST_LEN = 8192
ATTN_SUB_BLOCKS = 4
EPS = 1e-6
NEG = -1e30
LN2 = math.log(2.0)
LOG2E = 1.0 / LN2
VMEM_LIMIT = 62 * 1024 * 1024


def _rms(x, g):
    return x * lax.rsqrt(jnp.mean(x * x, axis=-1, keepdims=True) + EPS) * g


def _resident(shape):
    zeros = (0,) * len(shape)
    return pl.BlockSpec(shape, lambda *_: zeros, pipeline_mode=pl.Buffered(1))


def _dot(a, b):
    return jnp.dot(a, b, preferred_element_type=F32)


def _dot_nt(a, b):
    return lax.dot_general(a, b, (((1,), (1,)), ((), ())), preferred_element_type=F32)


def _project_qkv(x_ref, gmix_ref, gq_ref, gk_ref, rc_ref, ru_ref, rd_ref, bd_ref, w_refs, perm_refs,
                 dils, emit):
    rows = x_ref.shape[0]
    tile = min(rows, TOKEN_TILE)
    starts = range(0, rows, tile)
    h = _rms(x_ref[...], gmix_ref[...]).astype(BF16)

    def permuted(p_ref):
        parts = [_dot(p_ref[...], h[s:s + tile]).astype(BF16) for s in starts]
        return parts[0] if len(parts) == 1 else jnp.concatenate(parts, axis=0)

    hs = [h if p_ref is None else permuted(p_ref) for p_ref in perm_refs]
    bd = bd_ref[...]
    wide = bd.shape[0]

    def table(ref, dil):
        if dil == 1:
            return ref[...]
        return jnp.concatenate([ref[pl.ds(s + r, tile // dil, stride=dil), :]
                                for s in starts for r in range(dil)], axis=0)

    tables = {dil: tuple(table(ref, dil) for ref in (rc_ref, ru_ref, rd_ref)) for dil in set(dils)}

    for ti, g_ref in enumerate((gq_ref, gk_ref, None)):
        for gi in range(N_GROUPS):
            z = _dot(hs[gi], w_refs[ti * N_GROUPS + gi][...])
            if g_ref is None:
                emit(ti, gi, [z[:, c * LANES:(c + 1) * LANES] for c in range(GROUP_SLABS)])
                continue
            rc, ru, rd = tables[dils[gi]]

            def rot(zn):
                return zn * rc + pltpu.roll(zn, LANES - ROT_HALF, 1) * ru + pltpu.roll(zn, ROT_HALF, 1) * rd

            vals = []
            for c in range(0, D_GROUP, wide):
                zc = z[:, c:c + wide]
                ss = _dot((zc * zc).astype(BF16), bd)
                zn = zc * lax.rsqrt(ss * (1.0 / HEAD_DIM) + EPS)
                vals += [rot(zn[:, j:j + LANES] * g_ref[...]) for j in range(0, wide, LANES)]
            emit(ti, gi, vals)


def _qkv_prompt_kernel(*refs, tm, n_tiles):
    n_perm = sum(dil > 1 for _, dil in GROUPS)
    n_w = 3 * N_GROUPS
    ins, w_refs, perms, rest = refs[:8], refs[8:8 + n_w], refs[8 + n_w:8 + n_w + n_perm], refs[8 + n_w + n_perm:]
    outs, kv_refs = rest[:3 * N_GROUPS], rest[3 * N_GROUPS:4 * N_GROUPS]
    kv_slabs, order_slab = rest[4 * N_GROUPS:4 * N_GROUPS + 2], rest[4 * N_GROUPS + 2]
    perms = list(perms)
    perm_refs = [perms.pop(0) if dil > 1 else None for _, dil in GROUPS]

    def emit(ti, gi, vals):
        out = outs[ti * N_GROUPS + gi]
        for cc, val in enumerate(vals):
            out[:, cc * LANES:(cc + 1) * LANES] = val.astype(BF16)
            if ti > 0:
                kv_slabs[ti - 1][gi * GROUP_SLABS + cc] = val

    _project_qkv(*ins, w_refs, perm_refs, [dil for _, dil in GROUPS], emit)

    i = pl.program_id(1)
    seq = n_tiles * tm
    for gi, (win, dil) in enumerate(GROUPS):
        rows = min(win, tm)
        first = (seq - win) // tm if win >= tm else n_tiles - 1
        piece = TOKEN_TILE // dil

        @pl.when(i >= first)
        def _(gi=gi, dil=dil, rows=rows, piece=piece):
            for kv in range(2):
                for cc in range(GROUP_SLABS):
                    c = gi * GROUP_SLABS + cc
                    if dil > 1:
                        for s in range(0, tm, TOKEN_TILE):
                            for r in range(dil):
                                order_slab[pl.ds(s + r, piece, stride=dil), :] = (
                                    kv_slabs[kv][c, s + r * piece:s + (r + 1) * piece, :])
                        val = order_slab[...]
                    else:
                        val = kv_slabs[kv][c]
                    t = val[tm - rows:tm].T
                    for hh in range(HEADS_PER_VREG):
                        kv_refs[gi][0, kv, cc * HEADS_PER_VREG + hh] = t[hh * HEAD_DIM:(hh + 1) * HEAD_DIM, :]


def _qkv_sample_kernel(*refs):
    n_w = 3 * N_GROUPS
    ins, w_refs, outs = refs[:8], refs[8:8 + n_w], refs[8 + n_w:11 + n_w]

    def emit(ti, gi, vals):
        for cc, val in enumerate(vals):
            c = gi * GROUP_SLABS + cc
            if ti == 0:
                outs[ti][:, c * LANES:(c + 1) * LANES] = val
            else:
                outs[ti][c * LANES:(c + 1) * LANES, :] = val.T

    _project_qkv(*ins, w_refs, [None] * N_GROUPS, [1] * N_GROUPS, emit)


def _residue_major_source(dil, tm):
    row = np.arange(tm, dtype=np.int32)
    piece = tm // dil
    return (row % piece) * dil + row // piece


def _rotary_tables(pos):
    inv = np.float32(ROPE_THETA) ** (-np.arange(ROT_HALF, dtype=np.float32) * np.float32(2.0 / ROT_DIM))
    ang = pos.astype(np.float32)[:, None] * inv[None, :]
    cos, sin = np.cos(ang), np.sin(ang)
    n = pos.shape[0]
    pad = np.zeros((n, HEAD_DIM - ROT_DIM), np.float32)
    zero = np.zeros((n, ROT_HALF), np.float32)
    rc = np.concatenate([cos, cos, pad + 1.0], axis=1)
    ru = np.concatenate([-sin, zero, pad], axis=1)
    rd = np.concatenate([zero, sin, pad], axis=1)
    return [jnp.asarray(np.tile(t, (1, HEADS_PER_VREG)).astype(np.float32)) for t in (rc, ru, rd)]


def _column_block(n_rows, width, index):
    return pl.BlockSpec((n_rows, width), lambda *_: (0, index), pipeline_mode=pl.Buffered(1))


def _qkv_call(x2, pos, g_mix, w_in, qkv_col0, g_q, g_k, *, prompt, batch, tm):
    n_tok, d_model = x2.shape
    seq = n_tok // batch
    n_tiles = seq // tm
    perms = []
    if prompt:
        assert tm % TOKEN_TILE == 0 and all(TOKEN_TILE % (dil * BF16_ROWS) == 0 for _, dil in GROUPS)
        for _, dil in GROUPS:
            if dil > 1:
                src = _residue_major_source(dil, TOKEN_TILE)
                perms.append(jnp.asarray(src[:, None] == np.arange(TOKEN_TILE)[None, :], dtype=BF16))
    rot = _rotary_tables(pos)
    gq = jnp.tile(g_q.astype(F32), HEADS_PER_VREG)[None, :] * (LOG2E * HEAD_DIM ** -0.5)
    gk = jnp.tile(g_k.astype(F32), HEADS_PER_VREG)[None, :]
    lane = np.arange(MXU_DIM)
    bd = jnp.asarray(lane[:, None] // HEAD_DIM == lane[None, :] // HEAD_DIM, dtype=BF16)

    tile_index = lambda b, i: b * n_tiles + i
    tok = lambda b, i: (tile_index(b, i), 0)
    in_specs = [
        pl.BlockSpec((tm, d_model), tok),
        _resident((1, d_model)),
        _resident((1, LANES)),
        _resident((1, LANES)),
        pl.BlockSpec((tm, LANES), lambda b, i: (i, 0)),
        pl.BlockSpec((tm, LANES), lambda b, i: (i, 0)),
        pl.BlockSpec((tm, LANES), lambda b, i: (i, 0)),
        _resident((MXU_DIM, MXU_DIM)),
    ] + [_column_block(d_model, D_GROUP, qkv_col0 // D_GROUP + j) for j in range(3 * N_GROUPS)]
    in_specs += [_resident(p.shape) for p in perms]
    assert qkv_col0 % D_GROUP == 0
    if prompt:
        out_shape = [jax.ShapeDtypeStruct((n_tok, D_GROUP), BF16)] * (3 * N_GROUPS)
        out_specs = [pl.BlockSpec((tm, D_GROUP), tok)] * (3 * N_GROUPS)
        for win, _ in GROUPS:
            rows = min(win, tm)
            first = (seq - win) // tm if win >= tm else n_tiles - 1
            out_shape.append(jax.ShapeDtypeStruct((batch, 2, H_G, HEAD_DIM, win), F32))
            out_specs.append(pl.BlockSpec(
                (1, 2, H_G, HEAD_DIM, rows),
                lambda b, i, first=first: (b, 0, 0, 0, jnp.maximum(i - first, 0))))
        scratch = [pltpu.VMEM((D_QKV // LANES, tm, LANES), F32)] * 2 + [pltpu.VMEM((tm, LANES), F32)]
        body = functools.partial(_qkv_prompt_kernel, tm=tm, n_tiles=n_tiles)
    else:
        assert n_tiles == 1 and batch == 1 and tm == LANES
        out_shape = [jax.ShapeDtypeStruct((n_tok, D_QKV), F32)] + [jax.ShapeDtypeStruct((D_QKV, n_tok), F32)] * 2
        out_specs = [pl.BlockSpec((tm, D_QKV), tok)] + [pl.BlockSpec((D_QKV, tm), lambda b, i: (0, 0))] * 2
        scratch = []
        body = _qkv_sample_kernel

    return pl.pallas_call(
        body,
        grid=(batch, n_tiles),
        in_specs=in_specs,
        out_specs=out_specs,
        out_shape=out_shape,
        scratch_shapes=scratch,
        compiler_params=pltpu.CompilerParams(
            dimension_semantics=("arbitrary", "arbitrary"), vmem_limit_bytes=VMEM_LIMIT),
        name="qkv_prompt" if prompt else "qkv_sample",
    )(x2, g_mix[None, :], gq, gk, *rot, bd, *[w_in] * (3 * N_GROUPS), *perms)


def _attn_kernel(q_ref, kc_ref, kp_ref, vc_ref, vp_ref, *rest, dil, tasks, n_alias, t_new):
    n_in = 4 * len(tasks)
    n_skip = n_in + n_alias
    task_ins, (o_ref, lse_ref), task_outs = rest[:n_in], rest[n_skip:n_skip + 2], rest[n_skip + 2:]
    step = (pl.program_id(0) * pl.num_programs(1) + pl.program_id(1)) * pl.num_programs(2) + pl.program_id(2)

    has_prev = pl.program_id(1) > 0
    piece = TOKEN_TILE // dil

    def residue_rows(i):
        return [(t, 0, slice(i * piece, (i + 1) * piece)) for t in range(SUB_BLOCK // piece)]

    def load(ref, rows, sl):
        parts = [ref[rs + (sl,)] for rs in rows]
        return parts[0] if len(parts) == 1 else jnp.concatenate(parts, axis=0)

    def store(ref, rows, sl, val):
        n = val.shape[0] // len(rows)
        for t, rs in enumerate(rows):
            ref[rs + (sl,)] = val[t * n:(t + 1) * n]
    key = lax.broadcasted_iota(jnp.int32, (2 * SUB_BLOCK, HEADS_PER_VREG * SUB_BLOCK), 0)
    qry = lax.broadcasted_iota(jnp.int32, (2 * SUB_BLOCK, HEADS_PER_VREG * SUB_BLOCK), 1) % SUB_BLOCK
    vis_cur = key <= qry
    vis_prev = key - SUB_BLOCK >= qry
    lane16 = lax.broadcasted_iota(jnp.int32, (SUB_BLOCK, LANES), 1).astype(F32).astype(BF16)
    head_lanes = (lane16 < HEAD_DIM, lane16 >= HEAD_DIM)
    eye_r = lax.broadcasted_iota(jnp.int32, (LANES, LANES), 0)
    eye_c = lax.broadcasted_iota(jnp.int32, (LANES, LANES), 1)
    eye = (eye_r == eye_c).astype(F32).astype(BF16)
    ones_rows = jnp.ones((BF16_ROWS, 2 * SUB_BLOCK), BF16)
    head_row = lax.broadcasted_iota(jnp.int32, (H_G, SUB_BLOCK), 0)

    pad = jnp.zeros((LANES - H_G, SUB_BLOCK), F32)
    slabs = [slice(p * LANES, (p + 1) * LANES) for p in range(GROUP_SLABS)]

    if dil == 1:
        blocks = [([(slice(0, SUB_BLOCK),)], kp_ref, vp_ref, [(slice(0, SUB_BLOCK),)], has_prev)]
        blocks += [([(slice(i * SUB_BLOCK, (i + 1) * SUB_BLOCK),)], kc_ref, vc_ref,
                    [(slice((i - 1) * SUB_BLOCK, i * SUB_BLOCK),)], True) for i in range(1, ATTN_SUB_BLOCKS)]
    else:
        blocks = [(residue_rows(i), kp_ref, vp_ref, residue_rows(i), has_prev)
                  for i in range(ATTN_SUB_BLOCKS)]

    def score_stage(blks):
        scores, vts = [], []
        for rows, kp_src, vp_src, prev_rows, _ in blks:
            for sl in slabs:
                q = load(q_ref, rows, sl)
                qq = jnp.concatenate([jnp.where(hl, q, jnp.zeros_like(q)) for hl in head_lanes], axis=0)
                kk = jnp.concatenate([load(kc_ref, rows, sl), load(kp_src, prev_rows, sl)], axis=0)
                vv = jnp.concatenate([load(vc_ref, rows, sl), load(vp_src, prev_rows, sl)], axis=0)
                scores.append(_dot_nt(kk, qq))
                vts.append(jnp.concatenate([_dot_nt(eye, vv).astype(BF16), ones_rows], axis=0))
        return scores, vts

    def softmax_stage(blks, scores):
        maxes, probs = [], []
        for bi, (_, _, _, _, has_prev) in enumerate(blks):
            visible = jnp.logical_or(vis_cur, jnp.logical_and(vis_prev, has_prev))
            for s in scores[bi * GROUP_SLABS:(bi + 1) * GROUP_SLABS]:
                s = jnp.where(visible, s, NEG)
                m = jnp.max(s, axis=0, keepdims=True)
                maxes.append(m)
                probs.append(jnp.exp2(s - m).astype(BF16))
        return maxes, probs

    def value_stage(vts, probs):
        return [_dot(vt, e) for vt, e in zip(vts, probs)]

    def store_stage(blks, outs, maxes):
        for bi, (rows, _, _, _, _) in enumerate(blks):
            lse_rows = jnp.zeros((H_G, SUB_BLOCK), F32)
            for p, sl in enumerate(slabs):
                ot, m = outs[bi * GROUP_SLABS + p], maxes[bi * GROUP_SLABS + p]
                den = ot[LANES:LANES + 1, :]
                lse = m * LN2 + jnp.log(den)
                inv = 1.0 / den
                o_halves = []
                for half in range(HEADS_PER_VREG):
                    qs = slice(half * SUB_BLOCK, (half + 1) * SUB_BLOCK)
                    o_halves.append(ot[half * HEAD_DIM:(half + 1) * HEAD_DIM, qs] * inv[:, qs])
                    lse_rows = jnp.where(head_row == p * HEADS_PER_VREG + half, lse[:, qs], lse_rows)
                store(o_ref, rows, sl, jnp.concatenate(o_halves, axis=0).T.astype(BF16))
            store(lse_ref, rows, slice(None), jnp.concatenate([lse_rows, pad], axis=0).T)

    first, second = blocks[:ATTN_SUB_BLOCKS // 2], blocks[ATTN_SUB_BLOCKS // 2:]
    scores_a, vts_a = score_stage(first)
    scores_b, vts_b = score_stage(second)
    for ti, (task_dil, first_head) in enumerate(tasks):
        _sample_attention(*task_ins[4 * ti:4 * ti + 4], *task_outs[3 * ti:3 * ti + 3], step,
                          dil=task_dil, first_head=first_head, t_new=t_new)
    maxes_a, probs_a = softmax_stage(first, scores_a)
    outs_a = value_stage(vts_a, probs_a)
    maxes_b, probs_b = softmax_stage(second, scores_b)
    outs_b = value_stage(vts_b, probs_b)
    store_stage(first, outs_a, maxes_a)
    store_stage(second, outs_b, maxes_b)


def _attn_call(q, k, v, gi, batch, seq, sample_tasks, t_new):
    dil = GROUPS[gi][1]
    n_tok = batch * seq
    if dil == 1:
        rows, parts = ATTN_SUB_BLOCKS * SUB_BLOCK, 1
        assert seq % rows == 0
        steps = seq // rows
        view = lambda width: (n_tok, width)
        cur_map = lambda b, s, p: (b * steps + s, 0)
        cur = lambda width: pl.BlockSpec((rows, width), cur_map)
        prev = pl.BlockSpec((SUB_BLOCK, D_GROUP),
                            lambda b, s, p: (jnp.maximum((b * steps + s) * ATTN_SUB_BLOCKS - 1, 0), 0))
    else:
        span, parts = SUB_BLOCK * dil, dil // ATTN_SUB_BLOCKS
        assert dil % ATTN_SUB_BLOCKS == 0 and span % TOKEN_TILE == 0 and seq % span == 0
        assert TOKEN_TILE % (dil * BF16_ROWS) == 0
        steps, span_tiles, part_rows = seq // span, span // TOKEN_TILE, TOKEN_TILE // parts
        view = lambda width: (n_tok // TOKEN_TILE, parts, part_rows, width)
        cur_map = lambda b, s, p: (b * steps + s, p, 0, 0)
        cur = lambda width: pl.BlockSpec((span_tiles, 1, part_rows, width), cur_map)
        prev = pl.BlockSpec((span_tiles, 1, part_rows, D_GROUP),
                            lambda b, s, p: (b * steps + jnp.maximum(s - 1, 0), p, 0, 0))
    q, k, v = (t.reshape(view(D_GROUP)) for t in (q, k, v))
    in_specs = [cur(D_GROUP), cur(D_GROUP), prev, cur(D_GROUP), prev]
    out_specs = [cur(D_GROUP), cur(LANES)]
    out_shape = [jax.ShapeDtypeStruct(view(D_GROUP), BF16), jax.ShapeDtypeStruct(view(LANES), F32)]
    operands, alias_operands, aliases = [q, k, k, v, v], [], {}
    seq_of = lambda b, s, p: (b * steps + s) * parts + p
    for task in sample_tasks:
        n_seq, q_rows, _ = task["q"].shape
        assert n_seq == batch * steps * parts
        wb = task["buf"].shape[4]
        n_h, h0 = task["n_heads"], task["first_head"]
        cols = n_h * HEAD_DIM
        col_block = (task["group"] * D_GROUP + h0 * HEAD_DIM) // cols
        buf_spec = pl.BlockSpec((1, 2, n_h, HEAD_DIM, wb), lambda b, s, p, hb=h0 // n_h: (seq_of(b, s, p), 0, hb, 0, 0))
        new_spec = pl.BlockSpec((cols, LANES), lambda b, s, p, cb=col_block: (cb, 0))
        in_specs += [pl.BlockSpec((1, q_rows, cols), lambda b, s, p, cb=col_block: (seq_of(b, s, p), 0, cb)),
                     new_spec, new_spec, buf_spec]
        operands += [task["q"], task["knt"], task["vnt"], task["buf"]]
        if task.get("nbuf") is not None:
            aliases[len(alias_operands)] = len(out_shape) + 2
            alias_operands.append(task["nbuf"])
        out_specs += [pl.BlockSpec((1, q_rows, cols), lambda b, s, p: (seq_of(b, s, p), 0, 0)),
                      pl.BlockSpec((1, q_rows, LANES), lambda b, s, p: (seq_of(b, s, p), 0, 0)), buf_spec]
        out_shape += [jax.ShapeDtypeStruct((n_seq, q_rows, cols), F32),
                      jax.ShapeDtypeStruct((n_seq, q_rows, LANES), F32),
                      jax.ShapeDtypeStruct(task["buf"].shape, F32)]
    n_main = len(operands)
    res = pl.pallas_call(
        functools.partial(_attn_kernel, dil=dil, n_alias=len(alias_operands), t_new=t_new,
                          tasks=tuple((GROUPS[t["group"]][1], t["first_head"]) for t in sample_tasks)),
        grid=(batch, steps, parts),
        in_specs=in_specs + [pl.BlockSpec(memory_space=pl.ANY)] * len(alias_operands),
        out_specs=out_specs,
        out_shape=out_shape,
        input_output_aliases={n_main + i: o for i, o in aliases.items()},
        compiler_params=pltpu.CompilerParams(
            dimension_semantics=("arbitrary", "arbitrary", "arbitrary"), vmem_limit_bytes=VMEM_LIMIT),
        name=f"attn_g{gi}",
    )(*operands, *alias_operands)
    task_res = [tuple(res[2 + 3 * i:5 + 3 * i]) for i in range(len(sample_tasks))]
    return res[0].reshape(n_tok, D_GROUP), res[1].reshape(n_tok, LANES), task_res


def _sample_attention(q_ref, knt_ref, vnt_ref, buf_ref, o_ref, lse_ref, nbuf_ref, seq_index, *,
                      dil, first_head, t_new):
    assert dil & (dil - 1) == 0
    q_rows = q_ref.shape[1]
    n_heads, wb = buf_ref.shape[2], buf_ref.shape[4]
    width = wb + LANES
    shift = lax.rem(LANES - t_new * seq_index, LANES)
    knt = pltpu.roll(knt_ref[...], shift, 1)
    vnt = pltpu.roll(vnt_ref[...], shift, 1)
    q = q_ref[0]

    t_row = lax.broadcasted_iota(jnp.int32, (q_rows, width), 0)
    p_col = lax.broadcasted_iota(jnp.int32, (q_rows, width), 1)
    back = wb + t_row - p_col
    valid = jnp.logical_and(jnp.logical_and(back >= 0, (back & (dil - 1)) == 0),
                            jnp.logical_and(back <= (N_KEYS - 1) * dil, p_col < wb + t_new))
    valid = jnp.concatenate([valid] * n_heads, axis=0)
    lane = lax.broadcasted_iota(jnp.int32, (q_rows, LANES), 1)
    cols = n_heads * HEAD_DIM

    kx = jnp.concatenate([buf_ref[0, 0].reshape(cols, wb), knt], axis=1)
    vx = jnp.concatenate([buf_ref[0, 1].reshape(cols, wb), vnt], axis=1)
    nbuf_ref[0, 0] = pltpu.roll(kx, width - t_new, 1)[:, :wb].reshape(n_heads, HEAD_DIM, wb)
    nbuf_ref[0, 1] = pltpu.roll(vx, width - t_new, 1)[:, :wb].reshape(n_heads, HEAD_DIM, wb)

    q_all = jnp.concatenate([q] * n_heads, axis=0)
    r_head = lax.broadcasted_iota(jnp.int32, q_all.shape, 0) // q_rows
    c_head = lax.broadcasted_iota(jnp.int32, q_all.shape, 1) // HEAD_DIM
    q_all = jnp.where(r_head == c_head, q_all, 0.0).astype(BF16)
    s = jnp.where(valid, _dot(q_all, kx.astype(BF16)), NEG)
    m = jnp.max(s, axis=1, keepdims=True)
    e = jnp.exp2(s - m)
    den = jnp.sum(e, axis=1, keepdims=True)
    o_all = _dot_nt(e.astype(BF16), vx.astype(BF16)) / den
    lse = m * LN2 + jnp.log(den)
    o_ref[0] = jnp.concatenate(
        [o_all[h * q_rows:(h + 1) * q_rows, h * HEAD_DIM:(h + 1) * HEAD_DIM] for h in range(n_heads)], axis=1)
    lse_ref[0] = functools.reduce(jnp.add, [
        jnp.where(lane == first_head + h, lse[h * q_rows:(h + 1) * q_rows], 0.0) for h in range(n_heads)])


def _merge_kernel(x_ref, o0_ref, o1_ref, o2_ref, l0_ref, l1_ref, l2_ref, pe_ref,
                  gmix_ref, wu_ref, wva_ref, wga0_ref, wga1_ref, wgb0_ref, wgb1_ref,
                  gv_ref, ws_ref, bias_ref, woa_ref, wob_ref, wo_ref,
                  gffn_ref, wff1_ref, wff2_ref, gple_ref, wpg_ref, wpp_ref,
                  *rest, prompt, tm, n_tiles, sample):
    if sample is None:
        (y_ref, va_ref), scratch = rest[:2], rest[2:]
    else:
        task_ins, (y_ref, va_ref), task_outs, scratch = rest[:4], rest[4:6], rest[6:9], rest[9:]
    d_model = x_ref.shape[1]
    d_a = gv_ref.shape[1]
    c_g = d_a // G_A
    x = x_ref[...]
    h = _rms(x, gmix_ref[...]).astype(BF16)

    u = jax.nn.gelu(_dot(h, wu_ref[...]))
    va = _rms(jax.nn.gelu(_dot(h, wva_ref[...])), gv_ref[...])
    if prompt:
        @pl.when(pl.program_id(1) == n_tiles - 1)
        def _():
            va_ref[0] = va[tm - CHUNK:tm, :]
    else:
        va_ref[...] = va
    va16 = va.astype(BF16)
    r_i = lax.broadcasted_iota(jnp.int32, (CHUNK, CHUNK), 0)
    c_i = lax.broadcasted_iota(jnp.int32, (CHUNK, CHUNK), 1)
    w_tril = [jnp.where(r_i >= c_i, ws_ref[g], jnp.zeros((CHUNK, CHUNK), BF16)) for g in range(G_A)]
    mix_rows = []
    for ci in range(tm // CHUNK):
        rs = slice(ci * CHUNK, (ci + 1) * CHUNK)
        cols = [_dot(w_tril[g], va16[rs, g * c_g:(g + 1) * c_g]) for g in range(G_A)]
        mix_rows.append(jnp.concatenate(cols, axis=1) + bias_ref[...])
    a = u * jnp.concatenate(mix_rows, axis=0)

    o_refs, l_refs = (o0_ref, o1_ref, o2_ref), (l0_ref, l1_ref, l2_ref)
    o_slabs, lses = [], []
    for gi, (_, dil) in enumerate(GROUPS):
        if not prompt or dil == 1:
            o_slabs.append([o_refs[gi][:, cc * LANES:(cc + 1) * LANES].astype(F32)
                            for cc in range(GROUP_SLABS)])
            lses.append(l_refs[gi][...])
            continue
        o_sc, l_sc = scratch[2 * (gi - 1)], scratch[2 * (gi - 1) + 1]
        piece = tm // dil
        for r in range(dil):
            rows = slice(r * piece, (r + 1) * piece)
            for cc in range(GROUP_SLABS):
                o_sc[cc, pl.ds(r, piece, stride=dil), :] = (
                    o_refs[gi][rows, cc * LANES:(cc + 1) * LANES].astype(F32))
            l_sc[pl.ds(r, piece, stride=dil), :] = l_refs[gi][rows, :]
        o_slabs.append([o_sc[cc] for cc in range(GROUP_SLABS)])
        lses.append(l_sc[...])

    m = jnp.maximum(jnp.maximum(lses[0], lses[1]), lses[2])
    ws = [jnp.exp(l - m) for l in lses]
    inv = 1.0 / (ws[0] + ws[1] + ws[2])
    ws = [w * inv for w in ws]
    lane = lax.broadcasted_iota(jnp.int32, (tm, LANES), 1)
    first_head = lane < HEAD_DIM
    b_cols = []
    for p in range(GROUP_SLABS):
        acc = jnp.zeros((tm, LANES), F32)
        for w, slabs in zip(ws, o_slabs):
            w_pair = jnp.where(first_head, w[:, 2 * p:2 * p + 1], w[:, 2 * p + 1:2 * p + 2])
            acc = acc + w_pair * slabs[p]
        b_cols.append(acc)
    b = jnp.concatenate(b_cols, axis=1)

    gate_a = jax.nn.sigmoid(jnp.concatenate([_dot(h, wga0_ref[...]), _dot(h, wga1_ref[...])], axis=1))
    gate_b = jax.nn.sigmoid(jnp.concatenate([_dot(h, wgb0_ref[...]), _dot(h, wgb1_ref[...])], axis=1))
    merged = gate_a * _dot(a.astype(BF16), woa_ref[...]) + gate_b * _dot(b.astype(BF16), wob_ref[...])
    x = x + _dot(merged.astype(BF16), wo_ref[...])

    h2 = _rms(x, gffn_ref[...]).astype(BF16)
    d_ff = wff1_ref.shape[1]
    ff_step = 1024
    ffn = jnp.zeros((tm, d_model), F32)
    for c0 in range(0, d_ff, ff_step):
        hid = jnp.square(jnp.maximum(_dot(h2, wff1_ref[:, c0:c0 + ff_step]), 0.0))
        ffn = ffn + _dot(hid.astype(BF16), wff2_ref[c0:c0 + ff_step, :])
    x = x + ffn

    if sample is not None:
        parts = H_G // task_ins[3].shape[2]
        step = pl.program_id(0) * n_tiles + pl.program_id(1)
        _sample_attention(*task_ins, *task_outs, step // parts, dil=sample[0],
                          first_head=(step % parts) * (H_G // parts), t_new=sample[1])

    gate = jax.nn.sigmoid(_dot(_rms(x, gple_ref[...]).astype(BF16), wpg_ref[...]))
    y_ref[...] = x + gate * _dot(pe_ref[...].astype(BF16), wpp_ref[...])


def _merge_call(x2, outs, lses, pe2, wts, ws_mat, bias, *, prompt, batch, tm, sample_task=None, t_new=None):
    n_tok, d_model = x2.shape
    seq = n_tok // batch
    n_tiles = seq // tm
    d_a = wts["g_v"].shape[1]
    tile_index = lambda b, i: b * n_tiles + i
    tok = lambda b, i: (tile_index(b, i), 0)
    row_spec = lambda width: pl.BlockSpec((tm, width), tok)

    names = ("g_mix", "w_in", "w_in", "w_in", "w_in", "w_in", "w_in", "g_v", "ws", "bias", "w_oa", "w_ob", "w_o",
             "g_ffn", "w_ff1", "w_ff2", "g_ple", "w_ple_gate", "w_ple_proj")
    consts = dict(wts, ws=ws_mat, bias=bias)
    half_model = d_model // 2
    assert d_a == half_model and wts["gate_col0"] % half_model == 0
    g0 = wts["gate_col0"] // half_model
    w_in_blocks = iter([0, 1, g0, g0 + 1, g0 + 2, g0 + 3])
    const_specs = [_column_block(d_model, half_model, next(w_in_blocks)) if k == "w_in"
                   else _resident(consts[k].shape) for k in names]
    in_specs = ([row_spec(d_model)]
                + [row_spec(D_GROUP)] * N_GROUPS + [row_spec(LANES)] * N_GROUPS
                + [row_spec(pe2.shape[1])] + const_specs)
    scratch = []
    if prompt:
        va_shape = jax.ShapeDtypeStruct((batch, CHUNK, d_a), F32)
        va_spec = pl.BlockSpec((1, CHUNK, d_a), lambda b, i: (b, 0, 0))
        for _, dil in GROUPS:
            if dil > 1:
                scratch += [pltpu.VMEM((GROUP_SLABS, tm, LANES), F32), pltpu.VMEM((tm, LANES), F32)]
    else:
        va_shape = jax.ShapeDtypeStruct((n_tok, d_a), F32)
        va_spec = row_spec(d_a)
    operands = [x2, *outs, *lses, pe2, *[consts[k] for k in names]]
    out_specs = [row_spec(d_model), va_spec]
    out_shape = [jax.ShapeDtypeStruct((n_tok, d_model), F32), va_shape]
    sample = None
    if sample_task is not None:
        task = sample_task
        n_seq, q_rows, _ = task["q"].shape
        n_h, wb = task["n_heads"], task["buf"].shape[4]
        parts = H_G // n_h
        assert n_seq * parts == batch * n_tiles
        cols = n_h * HEAD_DIM
        col0 = task["group"] * D_GROUP // cols
        seq_of = lambda b, i: tile_index(b, i) // parts
        part_of = lambda b, i: tile_index(b, i) % parts
        buf_spec = pl.BlockSpec((1, 2, n_h, HEAD_DIM, wb), lambda b, i: (seq_of(b, i), 0, part_of(b, i), 0, 0))
        new_spec = pl.BlockSpec((cols, LANES), lambda b, i: (col0 + part_of(b, i), 0))
        in_specs += [pl.BlockSpec((1, q_rows, cols), lambda b, i: (seq_of(b, i), 0, col0 + part_of(b, i))),
                     new_spec, new_spec, buf_spec]
        operands += [task["q"], task["knt"], task["vnt"], task["buf"]]
        out_specs += [pl.BlockSpec((1, q_rows, cols), lambda b, i: (seq_of(b, i), 0, part_of(b, i))),
                      pl.BlockSpec((1, None, q_rows, LANES), lambda b, i: (seq_of(b, i), part_of(b, i), 0, 0)),
                      buf_spec]
        out_shape += [jax.ShapeDtypeStruct((n_seq, q_rows, D_GROUP), F32),
                      jax.ShapeDtypeStruct((n_seq, parts, q_rows, LANES), F32),
                      jax.ShapeDtypeStruct(task["buf"].shape, F32)]
        sample = (GROUPS[task["group"]][1], t_new)
    return pl.pallas_call(
        functools.partial(_merge_kernel, prompt=prompt, tm=tm, n_tiles=n_tiles, sample=sample),
        grid=(batch, n_tiles),
        in_specs=in_specs,
        out_specs=out_specs,
        out_shape=out_shape,
        scratch_shapes=scratch,
        compiler_params=pltpu.CompilerParams(
            dimension_semantics=("arbitrary", "arbitrary"), vmem_limit_bytes=VMEM_LIMIT),
        name="merge_prompt" if prompt else "merge_sample",
    )(*operands)


def kernel(x_prompt, x_sample, p_prompt, p_sample, cache_kv_w128, cache_kv_w512, cache_kv_w2048,
           g_mix, w_in, g_v, w_s, b_s, g_q, g_k, w_oa, w_ob, w_o,
           g_ffn, w_ff1, w_ff2, g_ple, w_ple_gate, w_ple_proj):
    depth = w_in.shape[0]
    batch, seq, d_model = x_prompt.shape
    dec_batch, t_new, _ = x_sample.shape
    d_a = g_v.shape[1]
    n_sample = dec_batch * t_new
    assert n_sample == CHUNK and w_in.shape[2] == 2 * d_a + 3 * D_QKV + 2 * d_model
    caches = (cache_kv_w128, cache_kv_w512, cache_kv_w2048)

    xp = x_prompt.reshape(batch * seq, d_model)
    xs = x_sample.reshape(n_sample, d_model)
    pos_p = np.arange(seq, dtype=np.int32)
    pos_s = PAST_LEN + np.tile(np.arange(t_new, dtype=np.int32), dec_batch)

    kv_p = [[] for _ in GROUPS]
    kv_s = [[] for _ in GROUPS]
    cv_p, cv_s = [], []
    for l in range(depth):
        w_in16 = w_in[l].astype(BF16)
        qkv_col0 = 2 * d_a
        wts = {
            "g_mix": g_mix[l][None, :], "w_in": w_in16, "gate_col0": qkv_col0 + 3 * D_QKV,
            "g_v": g_v[l][None, :],
            "w_oa": w_oa[l].astype(BF16), "w_ob": w_ob[l].astype(BF16), "w_o": w_o[l].astype(BF16),
            "g_ffn": g_ffn[l][None, :], "w_ff1": w_ff1[l].astype(BF16), "w_ff2": w_ff2[l].astype(BF16),
            "g_ple": g_ple[l][None, :], "w_ple_gate": w_ple_gate[l].astype(BF16),
            "w_ple_proj": w_ple_proj[l].astype(BF16),
        }
        c_g = d_a // G_A
        ws_p = w_s[l].astype(BF16)
        bias_p = jnp.repeat(b_s[l].T, c_g, axis=1)
        tok = np.arange(CHUNK)
        same_seq = jnp.asarray(tok[:, None] // t_new == tok[None, :] // t_new)
        row_of = jnp.asarray(tok[:, None] % t_new == np.arange(t_new)[None, :], dtype=F32)
        ws_s = jnp.einsum("rt,gts,cs->grc", row_of, w_s[l][:, :t_new, :t_new], row_of,
                          precision=lax.Precision.HIGHEST)
        ws_s = jnp.where(same_seq, ws_s, 0.0).astype(BF16)
        reps = CHUNK // t_new
        bias_s = jnp.tile(jnp.repeat(b_s[l][:, :t_new].T, c_g, axis=1), (reps, 1))

        tm = TOKEN_TILE
        res = _qkv_call(xp, pos_p, g_mix[l], w_in16, qkv_col0, g_q[l], g_k[l], prompt=True, batch=batch,
                        tm=QKV_TILE)
        q_g, k_g, v_g, kv_tails = res[0:3], res[3:6], res[6:9], res[9:12]
        qs, knt, vnt = _qkv_call(
            xs, pos_s, g_mix[l], w_in16, qkv_col0, g_q[l], g_k[l], prompt=False, batch=1, tm=n_sample)

        q_rows = 8
        assert LANES % t_new == 0 and t_new <= q_rows and N_GROUPS == 3
        q_pad = jnp.pad(qs.reshape(dec_batch, t_new, D_QKV), ((0, 0), (0, q_rows - t_new), (0, 0)))
        bufs = [jnp.transpose(c[l], (0, 1, 3, 4, 2)) for c in caches]
        task = lambda g, n_h, h0=0, **kw: dict(group=g, first_head=h0, n_heads=n_h, q=q_pad, knt=knt, vnt=vnt,
                                               buf=bufs[g], **kw)
        half = H_G // 2
        o2, l2, ((so1a, sl1a, nbuf1),) = _attn_call(q_g[2], k_g[2], v_g[2], 2, batch, seq,
                                                  [task(1, half)], t_new)
        o1, l1, ((so1b, sl1b, nbuf1),) = _attn_call(q_g[1], k_g[1], v_g[1], 1, batch, seq,
                                                  [task(1, half, half, nbuf=nbuf1)], t_new)
        so1, sl1 = jnp.concatenate([so1a, so1b], axis=2), sl1a + sl1b
        o0, l0, ((so0, sl0, nbuf0),) = _attn_call(q_g[0], k_g[0], v_g[0], 0, batch, seq, [task(0, H_G)], t_new)

        xp, va_p, so2, sl2, nbuf2 = _merge_call(
            xp, (o0, o1, o2), (l0, l1, l2), p_prompt[l].reshape(batch * seq, -1), wts, ws_p, bias_p,
            prompt=True, batch=batch, tm=tm, sample_task=task(2, H_G // 2), t_new=t_new)
        s_outs = (so0, so1, so2)
        s_lses = (sl0, sl1, jnp.sum(sl2, axis=1))
        for gi, nbuf in enumerate((nbuf0, nbuf1, nbuf2)):
            kv_s[gi].append(jnp.transpose(nbuf, (0, 1, 4, 2, 3)))
        for gi, st in enumerate(kv_tails):
            kv_p[gi].append(jnp.transpose(st, (0, 1, 4, 2, 3)))
        cv_p.append(va_p)

        s_outs = [o[:, :t_new].reshape(n_sample, D_GROUP) for o in s_outs]
        s_lses = [s[:, :t_new].reshape(n_sample, LANES) for s in s_lses]
        xs, va_s = _merge_call(xs, s_outs, s_lses, p_sample[l].reshape(n_sample, -1), wts, ws_s, bias_s,
                               prompt=False, batch=1, tm=n_sample)
        cv_s.append(va_s.reshape(dec_batch, t_new, d_a))

    return (xp.reshape(batch, seq, d_model), xs.reshape(dec_batch, t_new, d_model),
            jnp.stack(kv_p[0]), jnp.stack(kv_p[1]), jnp.stack(kv_p[2]),
            jnp.stack(kv_s[0]), jnp.stack(kv_s[1]), jnp.stack(kv_s[2]),
            jnp.stack(cv_p), jnp.stack(cv_s))
```

```python
import functools
import math

import jax
import jax.numpy as jnp
import numpy as np
from jax import lax
from jax.experimental import pallas as pl
from jax.experimental.pallas import tpu as pltpu

F32 = jnp.float32
BF16 = jnp.bfloat16

LANES = 128
BF16_ROWS = 16
MXU_DIM = 256
TOKEN_TILE = 256
QKV_TILE = 512
HEAD_DIM = 64
H_G = 8
HEADS_PER_VREG = LANES // HEAD_DIM
GROUPS = ((128, 1), (512, 4), (2048, 16))
N_GROUPS = len(GROUPS)
D_GROUP = H_G * HEAD_DIM
GROUP_SLABS = D_GROUP // LANES
D_QKV = N_GROUPS * D_GROUP
N_KEYS = 129
SUB_BLOCK = 128
ROT_DIM = HEAD_DIM // 4
ROT_HALF = ROT_DIM // 2
ROPE_THETA = 500000.0
CHUNK = 128
G_A = 4
PAST_LEN = 8192
ATTN_SUB_BLOCKS = 8
EPS = 1e-6
NEG = -1e30
LN2 = math.log(2.0)
LOG2E = 1.0 / LN2
VMEM_LIMIT = 62 * 1024 * 1024


def _rms(x, g):
    return x * lax.rsqrt(jnp.mean(x * x, axis=-1, keepdims=True) + EPS) * g


def _resident(shape):
    zeros = (0,) * len(shape)
    return pl.BlockSpec(shape, lambda *_: zeros, pipeline_mode=pl.Buffered(1))


def _dot(a, b):
    return jnp.dot(a, b, preferred_element_type=F32)


def _dot_nt(a, b):
    return lax.dot_general(a, b, (((1,), (1,)), ((), ())), preferred_element_type=F32)


def _project_qkv(x_ref, gmix_ref, gq_ref, gk_ref, rc_ref, ru_ref, rd_ref, bd_ref, w_refs, perm_refs,
                 dils, emit):
    rows = x_ref.shape[0]
    tile = min(rows, TOKEN_TILE)
    starts = range(0, rows, tile)
    h = _rms(x_ref[...], gmix_ref[...]).astype(BF16)

    def permuted(p_ref):
        parts = [_dot(p_ref[...], h[s:s + tile]).astype(BF16) for s in starts]
        return parts[0] if len(parts) == 1 else jnp.concatenate(parts, axis=0)

    hs = [h if p_ref is None else permuted(p_ref) for p_ref in perm_refs]
    bd = bd_ref[...]
    wide = bd.shape[0]

    def table(ref, dil):
        if dil == 1:
            return ref[...]
        return jnp.concatenate([ref[pl.ds(s + r, tile // dil, stride=dil), :]
                                for s in starts for r in range(dil)], axis=0)

    tables = {dil: tuple(table(ref, dil) for ref in (rc_ref, ru_ref, rd_ref)) for dil in set(dils)}

    for ti, g_ref in enumerate((gq_ref, gk_ref, None)):
        for gi in range(N_GROUPS):
            z = _dot(hs[gi], w_refs[ti * N_GROUPS + gi][...])
            if g_ref is None:
                emit(ti, gi, [z[:, c * LANES:(c + 1) * LANES] for c in range(GROUP_SLABS)])
                continue
            rc, ru, rd = tables[dils[gi]]

            def rot(zn):
                return zn * rc + pltpu.roll(zn, LANES - ROT_HALF, 1) * ru + pltpu.roll(zn, ROT_HALF, 1) * rd

            vals = []
            for c in range(0, D_GROUP, wide):
                zc = z[:, c:c + wide]
                ss = _dot((zc * zc).astype(BF16), bd)
                zn = zc * lax.rsqrt(ss * (1.0 / HEAD_DIM) + EPS)
                vals += [rot(zn[:, j:j + LANES] * g_ref[...]) for j in range(0, wide, LANES)]
            emit(ti, gi, vals)


def _qkv_prompt_kernel(*refs, tm, n_tiles):
    n_perm = sum(dil > 1 for _, dil in GROUPS)
    n_w = 3 * N_GROUPS
    ins, w_refs, perms, rest = refs[:8], refs[8:8 + n_w], refs[8 + n_w:8 + n_w + n_perm], refs[8 + n_w + n_perm:]
    outs, kv_refs = rest[:3 * N_GROUPS], rest[3 * N_GROUPS:4 * N_GROUPS]
    kv_slabs, order_slab = rest[4 * N_GROUPS:4 * N_GROUPS + 2], rest[4 * N_GROUPS + 2]
    perms = list(perms)
    perm_refs = [perms.pop(0) if dil > 1 else None for _, dil in GROUPS]

    def emit(ti, gi, vals):
        out = outs[ti * N_GROUPS + gi]
        for cc, val in enumerate(vals):
            out[:, cc * LANES:(cc + 1) * LANES] = val.astype(BF16)
            if ti > 0:
                kv_slabs[ti - 1][gi * GROUP_SLABS + cc] = val

    _project_qkv(*ins, w_refs, perm_refs, [dil for _, dil in GROUPS], emit)

    i = pl.program_id(1)
    seq = n_tiles * tm
    for gi, (win, dil) in enumerate(GROUPS):
        rows = min(win, tm)
        first = (seq - win) // tm if win >= tm else n_tiles - 1
        piece = TOKEN_TILE // dil

        @pl.when(i >= first)
        def _(gi=gi, dil=dil, rows=rows, piece=piece):
            for kv in range(2):
                for cc in range(GROUP_SLABS):
                    c = gi * GROUP_SLABS + cc
                    if dil > 1:
                        for s in range(0, tm, TOKEN_TILE):
                            for r in range(dil):
                                order_slab[pl.ds(s + r, piece, stride=dil), :] = (
                                    kv_slabs[kv][c, s + r * piece:s + (r + 1) * piece, :])
                        val = order_slab[...]
                    else:
                        val = kv_slabs[kv][c]
                    t = val[tm - rows:tm].T
                    for hh in range(HEADS_PER_VREG):
                        kv_refs[gi][0, kv, cc * HEADS_PER_VREG + hh] = t[hh * HEAD_DIM:(hh + 1) * HEAD_DIM, :]


def _qkv_sample_kernel(*refs):
    n_w = 3 * N_GROUPS
    ins, w_refs, outs = refs[:8], refs[8:8 + n_w], refs[8 + n_w:11 + n_w]

    def emit(ti, gi, vals):
        for cc, val in enumerate(vals):
            c = gi * GROUP_SLABS + cc
            if ti == 0:
                outs[ti][:, c * LANES:(c + 1) * LANES] = val
            else:
                outs[ti][c * LANES:(c + 1) * LANES, :] = val.T

    _project_qkv(*ins, w_refs, [None] * N_GROUPS, [1] * N_GROUPS, emit)


def _residue_major_source(dil, tm):
    row = np.arange(tm, dtype=np.int32)
    piece = tm // dil
    return (row % piece) * dil + row // piece


def _rotary_tables(pos):
    inv = np.float32(ROPE_THETA) ** (-np.arange(ROT_HALF, dtype=np.float32) * np.float32(2.0 / ROT_DIM))
    ang = pos.astype(np.float32)[:, None] * inv[None, :]
    cos, sin = np.cos(ang), np.sin(ang)
    n = pos.shape[0]
    pad = np.zeros((n, HEAD_DIM - ROT_DIM), np.float32)
    zero = np.zeros((n, ROT_HALF), np.float32)
    rc = np.concatenate([cos, cos, pad + 1.0], axis=1)
    ru = np.concatenate([-sin, zero, pad], axis=1)
    rd = np.concatenate([zero, sin, pad], axis=1)
    return [jnp.asarray(np.tile(t, (1, HEADS_PER_VREG)).astype(np.float32)) for t in (rc, ru, rd)]


def _column_block(n_rows, width, index):
    return pl.BlockSpec((n_rows, width), lambda *_: (0, index), pipeline_mode=pl.Buffered(1))


def _qkv_call(x2, pos, g_mix, w_in, qkv_col0, g_q, g_k, *, prompt, batch, tm):
    n_tok, d_model = x2.shape
    seq = n_tok // batch
    n_tiles = seq // tm
    perms = []
    if prompt:
        assert tm % TOKEN_TILE == 0 and all(TOKEN_TILE % (dil * BF16_ROWS) == 0 for _, dil in GROUPS)
        for _, dil in GROUPS:
            if dil > 1:
                src = _residue_major_source(dil, TOKEN_TILE)
                perms.append(jnp.asarray(src[:, None] == np.arange(TOKEN_TILE)[None, :], dtype=BF16))
    rot = _rotary_tables(pos)
    gq = jnp.tile(g_q.astype(F32), HEADS_PER_VREG)[None, :] * (LOG2E * HEAD_DIM ** -0.5)
    gk = jnp.tile(g_k.astype(F32), HEADS_PER_VREG)[None, :]
    lane = np.arange(MXU_DIM)
    bd = jnp.asarray(lane[:, None] // HEAD_DIM == lane[None, :] // HEAD_DIM, dtype=BF16)

    tile_index = lambda b, i: b * n_tiles + i
    tok = lambda b, i: (tile_index(b, i), 0)
    in_specs = [
        pl.BlockSpec((tm, d_model), tok),
        _resident((1, d_model)),
        _resident((1, LANES)),
        _resident((1, LANES)),
        pl.BlockSpec((tm, LANES), lambda b, i: (i, 0)),
        pl.BlockSpec((tm, LANES), lambda b, i: (i, 0)),
        pl.BlockSpec((tm, LANES), lambda b, i: (i, 0)),
        _resident((MXU_DIM, MXU_DIM)),
    ] + [_column_block(d_model, D_GROUP, qkv_col0 // D_GROUP + j) for j in range(3 * N_GROUPS)]
    in_specs += [_resident(p.shape) for p in perms]
    assert qkv_col0 % D_GROUP == 0
    if prompt:
        out_shape = [jax.ShapeDtypeStruct((n_tok, D_GROUP), BF16)] * (3 * N_GROUPS)
        out_specs = [pl.BlockSpec((tm, D_GROUP), tok)] * (3 * N_GROUPS)
        for win, _ in GROUPS:
            rows = min(win, tm)
            first = (seq - win) // tm if win >= tm else n_tiles - 1
            out_shape.append(jax.ShapeDtypeStruct((batch, 2, H_G, HEAD_DIM, win), F32))
            out_specs.append(pl.BlockSpec(
                (1, 2, H_G, HEAD_DIM, rows),
                lambda b, i, first=first: (b, 0, 0, 0, jnp.maximum(i - first, 0))))
        scratch = [pltpu.VMEM((D_QKV // LANES, tm, LANES), F32)] * 2 + [pltpu.VMEM((tm, LANES), F32)]
        body = functools.partial(_qkv_prompt_kernel, tm=tm, n_tiles=n_tiles)
    else:
        assert n_tiles == 1 and batch == 1 and tm == LANES
        out_shape = [jax.ShapeDtypeStruct((n_tok, D_QKV), F32)] + [jax.ShapeDtypeStruct((D_QKV, n_tok), F32)] * 2
        out_specs = [pl.BlockSpec((tm, D_QKV), tok)] + [pl.BlockSpec((D_QKV, tm), lambda b, i: (0, 0))] * 2
        scratch = []
        body = _qkv_sample_kernel

    return pl.pallas_call(
        body,
        grid=(batch, n_tiles),
        in_specs=in_specs,
        out_specs=out_specs,
        out_shape=out_shape,
        scratch_shapes=scratch,
        compiler_params=pltpu.CompilerParams(
            dimension_semantics=("arbitrary", "arbitrary"), vmem_limit_bytes=VMEM_LIMIT),
        name="qkv_prompt" if prompt else "qkv_sample",
    )(x2, g_mix[None, :], gq, gk, *rot, bd, *[w_in] * (3 * N_GROUPS), *perms)


def _attn_kernel(q_ref, kc_ref, kp_ref, vc_ref, vp_ref, *rest, dil, sub_blocks, tasks, n_alias, t_new):
    n_in = 4 * len(tasks)
    n_skip = n_in + n_alias
    task_ins, (o_ref, lse_ref), task_outs = rest[:n_in], rest[n_skip:n_skip + 2], rest[n_skip + 2:]
    step = (pl.program_id(0) * pl.num_programs(1) + pl.program_id(1)) * pl.num_programs(2) + pl.program_id(2)

    has_prev = pl.program_id(1) > 0
    piece = TOKEN_TILE // dil

    def residue_rows(i):
        return [(t, 0, slice(i * piece, (i + 1) * piece)) for t in range(SUB_BLOCK // piece)]

    def load(ref, rows, sl):
        parts = [ref[rs + (sl,)] for rs in rows]
        return parts[0] if len(parts) == 1 else jnp.concatenate(parts, axis=0)

    def store(ref, rows, sl, val):
        n = val.shape[0] // len(rows)
        for t, rs in enumerate(rows):
            ref[rs + (sl,)] = val[t * n:(t + 1) * n]
    key = lax.broadcasted_iota(jnp.int32, (2 * SUB_BLOCK, HEADS_PER_VREG * SUB_BLOCK), 0)
    qry = lax.broadcasted_iota(jnp.int32, (2 * SUB_BLOCK, HEADS_PER_VREG * SUB_BLOCK), 1) % SUB_BLOCK
    vis_cur = key <= qry
    vis_prev = key - SUB_BLOCK >= qry
    lane16 = lax.broadcasted_iota(jnp.int32, (SUB_BLOCK, LANES), 1).astype(F32).astype(BF16)
    head_lanes = (lane16 < HEAD_DIM, lane16 >= HEAD_DIM)
    eye_r = lax.broadcasted_iota(jnp.int32, (LANES, LANES), 0)
    eye_c = lax.broadcasted_iota(jnp.int32, (LANES, LANES), 1)
    eye = (eye_r == eye_c).astype(F32).astype(BF16)
    ones_rows = jnp.ones((BF16_ROWS, 2 * SUB_BLOCK), BF16)
    head_row = lax.broadcasted_iota(jnp.int32, (H_G, SUB_BLOCK), 0)

    pad = jnp.zeros((LANES - H_G, SUB_BLOCK), F32)
    slabs = [slice(p * LANES, (p + 1) * LANES) for p in range(GROUP_SLABS)]

    if dil == 1:
        blocks = [([(slice(0, SUB_BLOCK),)], kp_ref, vp_ref, [(slice(0, SUB_BLOCK),)], has_prev)]
        blocks += [([(slice(i * SUB_BLOCK, (i + 1) * SUB_BLOCK),)], kc_ref, vc_ref,
                    [(slice((i - 1) * SUB_BLOCK, i * SUB_BLOCK),)], True) for i in range(1, sub_blocks)]
    else:
        blocks = [(residue_rows(i), kp_ref, vp_ref, residue_rows(i), has_prev)
                  for i in range(sub_blocks)]

    def score_stage(blks):
        scores, vts = [], []
        for rows, kp_src, vp_src, prev_rows, _ in blks:
            for sl in slabs:
                q = load(q_ref, rows, sl)
                qq = jnp.concatenate([jnp.where(hl, q, jnp.zeros_like(q)) for hl in head_lanes], axis=0)
                kk = jnp.concatenate([load(kc_ref, rows, sl), load(kp_src, prev_rows, sl)], axis=0)
                vv = jnp.concatenate([load(vc_ref, rows, sl), load(vp_src, prev_rows, sl)], axis=0)
                scores.append(_dot_nt(kk, qq))
                vts.append(jnp.concatenate([_dot_nt(eye, vv).astype(BF16), ones_rows], axis=0))
        return scores, vts

    def softmax_stage(blks, scores):
        maxes, probs = [], []
        for bi, (_, _, _, _, has_prev) in enumerate(blks):
            visible = jnp.logical_or(vis_cur, jnp.logical_and(vis_prev, has_prev))
            for s in scores[bi * GROUP_SLABS:(bi + 1) * GROUP_SLABS]:
                s = jnp.where(visible, s, NEG)
                m = jnp.max(s, axis=0, keepdims=True)
                maxes.append(m)
                probs.append(jnp.exp2(s - m).astype(BF16))
        return maxes, probs

    def value_stage(vts, probs):
        return [_dot(vt, e) for vt, e in zip(vts, probs)]

    def store_stage(blks, outs, maxes):
        for bi, (rows, _, _, _, _) in enumerate(blks):
            lse_rows = jnp.zeros((H_G, SUB_BLOCK), F32)
            for p, sl in enumerate(slabs):
                ot, m = outs[bi * GROUP_SLABS + p], maxes[bi * GROUP_SLABS + p]
                den = ot[LANES:LANES + 1, :]
                lse = m * LN2 + jnp.log(den)
                inv = 1.0 / den
                o_halves = []
                for half in range(HEADS_PER_VREG):
                    qs = slice(half * SUB_BLOCK, (half + 1) * SUB_BLOCK)
                    o_halves.append(ot[half * HEAD_DIM:(half + 1) * HEAD_DIM, qs] * inv[:, qs])
                    lse_rows = jnp.where(head_row == p * HEADS_PER_VREG + half, lse[:, qs], lse_rows)
                store(o_ref, rows, sl, jnp.concatenate(o_halves, axis=0).T.astype(BF16))
            store(lse_ref, rows, slice(None), jnp.concatenate([lse_rows, pad], axis=0).T)

    first, second = blocks[:sub_blocks // 2], blocks[sub_blocks // 2:]
    scores_a, vts_a = score_stage(first)
    scores_b, vts_b = score_stage(second)
    for ti, (task_dil, first_head) in enumerate(tasks):
        slots = task_ins[4 * ti].shape[0]
        _sample_attention(*task_ins[4 * ti:4 * ti + 4], *task_outs[3 * ti:3 * ti + 3], step * slots,
                          dil=task_dil, first_head=first_head, t_new=t_new)
    maxes_a, probs_a = softmax_stage(first, scores_a)
    outs_a = value_stage(vts_a, probs_a)
    maxes_b, probs_b = softmax_stage(second, scores_b)
    outs_b = value_stage(vts_b, probs_b)
    store_stage(first, outs_a, maxes_a)
    store_stage(second, outs_b, maxes_b)


def _attn_call(q, k, v, gi, batch, seq, sample_tasks, t_new):
    dil = GROUPS[gi][1]
    n_tok = batch * seq
    sub_blocks = ATTN_SUB_BLOCKS if dil == 1 else math.gcd(dil, ATTN_SUB_BLOCKS)
    if dil == 1:
        rows, parts = sub_blocks * SUB_BLOCK, 1
        assert seq % rows == 0
        steps = seq // rows
        view = lambda width: (n_tok, width)
        cur_map = lambda b, s, p: (b * steps + s, 0)
        cur = lambda width: pl.BlockSpec((rows, width), cur_map)
        prev = pl.BlockSpec((SUB_BLOCK, D_GROUP),
                            lambda b, s, p: (jnp.maximum((b * steps + s) * sub_blocks - 1, 0), 0))
    else:
        span, parts = SUB_BLOCK * dil, dil // sub_blocks
        assert span % TOKEN_TILE == 0 and seq % span == 0
        assert TOKEN_TILE % (dil * BF16_ROWS) == 0
        steps, span_tiles, part_rows = seq // span, span // TOKEN_TILE, TOKEN_TILE // parts
        view = lambda width: (n_tok // TOKEN_TILE, parts, part_rows, width)
        cur_map = lambda b, s, p: (b * steps + s, p, 0, 0)
        cur = lambda width: pl.BlockSpec((span_tiles, 1, part_rows, width), cur_map)
        prev = pl.BlockSpec((span_tiles, 1, part_rows, D_GROUP),
                            lambda b, s, p: (b * steps + jnp.maximum(s - 1, 0), p, 0, 0))
    q, k, v = (t.reshape(view(D_GROUP)) for t in (q, k, v))
    in_specs = [cur(D_GROUP), cur(D_GROUP), prev, cur(D_GROUP), prev]
    out_specs = [cur(D_GROUP), cur(LANES)]
    out_shape = [jax.ShapeDtypeStruct(view(D_GROUP), BF16), jax.ShapeDtypeStruct(view(LANES), F32)]
    operands, alias_operands, aliases = [q, k, k, v, v], [], {}
    seq_of = lambda b, s, p: (b * steps + s) * parts + p
    for task in sample_tasks:
        n_seq, q_rows, _ = task["q"].shape
        assert n_seq % (batch * steps * parts) == 0
        slots = n_seq // (batch * steps * parts)
        wb = task["buf"].shape[4]
        n_h, h0 = task["n_heads"], task["first_head"]
        cols = n_h * HEAD_DIM
        col_block = (task["group"] * D_GROUP + h0 * HEAD_DIM) // cols
        buf_spec = pl.BlockSpec((slots, 2, n_h, HEAD_DIM, wb),
                                lambda b, s, p, hb=h0 // n_h: (seq_of(b, s, p), 0, hb, 0, 0))
        new_spec = pl.BlockSpec((cols, LANES), lambda b, s, p, cb=col_block: (cb, 0))
        in_specs += [pl.BlockSpec((slots, q_rows, cols), lambda b, s, p, cb=col_block: (seq_of(b, s, p), 0, cb)),
                     new_spec, new_spec, buf_spec]
        operands += [task["q"], task["knt"], task["vnt"], task["buf"]]
        if task.get("nbuf") is not None:
            aliases[len(alias_operands)] = len(out_shape) + 2
            alias_operands.append(task["nbuf"])
        out_specs += [pl.BlockSpec((slots, q_rows, cols), lambda b, s, p: (seq_of(b, s, p), 0, 0)),
                      pl.BlockSpec((slots, q_rows, LANES), lambda b, s, p: (seq_of(b, s, p), 0, 0)), buf_spec]
        out_shape += [jax.ShapeDtypeStruct((n_seq, q_rows, cols), F32),
                      jax.ShapeDtypeStruct((n_seq, q_rows, LANES), F32),
                      jax.ShapeDtypeStruct(task["buf"].shape, F32)]
    n_main = len(operands)
    res = pl.pallas_call(
        functools.partial(_attn_kernel, dil=dil, sub_blocks=sub_blocks, n_alias=len(alias_operands), t_new=t_new,
                          tasks=tuple((GROUPS[t["group"]][1], t["first_head"]) for t in sample_tasks)),
        grid=(batch, steps, parts),
        in_specs=in_specs + [pl.BlockSpec(memory_space=pl.ANY)] * len(alias_operands),
        out_specs=out_specs,
        out_shape=out_shape,
        input_output_aliases={n_main + i: o for i, o in aliases.items()},
        compiler_params=pltpu.CompilerParams(
            dimension_semantics=("arbitrary", "arbitrary", "arbitrary"), vmem_limit_bytes=VMEM_LIMIT),
        name=f"attn_g{gi}",
    )(*operands, *alias_operands)
    task_res = [tuple(res[2 + 3 * i:5 + 3 * i]) for i in range(len(sample_tasks))]
    return res[0].reshape(n_tok, D_GROUP), res[1].reshape(n_tok, LANES), task_res


def _sample_attention(q_ref, knt_ref, vnt_ref, buf_ref, o_ref, lse_ref, nbuf_ref, first_seq, **kw):
    for slot in range(q_ref.shape[0]):
        _sample_attention_one(q_ref, knt_ref, vnt_ref, buf_ref, o_ref, lse_ref, nbuf_ref, slot,
                              first_seq + slot, **kw)


def _sample_attention_one(q_ref, knt_ref, vnt_ref, buf_ref, o_ref, lse_ref, nbuf_ref, slot, seq_index, *,
                          dil, first_head, t_new):
    assert dil & (dil - 1) == 0
    q_rows = q_ref.shape[1]
    n_heads, wb = buf_ref.shape[2], buf_ref.shape[4]
    width = wb + LANES
    shift = lax.rem(LANES - t_new * seq_index, LANES)
    knt = pltpu.roll(knt_ref[...], shift, 1)
    vnt = pltpu.roll(vnt_ref[...], shift, 1)
    q = q_ref[slot]

    t_row = lax.broadcasted_iota(jnp.int32, (q_rows, width), 0)
    p_col = lax.broadcasted_iota(jnp.int32, (q_rows, width), 1)
    back = wb + t_row - p_col
    valid = jnp.logical_and(jnp.logical_and(back >= 0, (back & (dil - 1)) == 0),
                            jnp.logical_and(back <= (N_KEYS - 1) * dil, p_col < wb + t_new))
    valid = jnp.concatenate([valid] * n_heads, axis=0)
    lane = lax.broadcasted_iota(jnp.int32, (q_rows, LANES), 1)
    cols = n_heads * HEAD_DIM

    kx = jnp.concatenate([buf_ref[slot, 0].reshape(cols, wb), knt], axis=1)
    vx = jnp.concatenate([buf_ref[slot, 1].reshape(cols, wb), vnt], axis=1)
    nbuf_ref[slot, 0] = pltpu.roll(kx, width - t_new, 1)[:, :wb].reshape(n_heads, HEAD_DIM, wb)
    nbuf_ref[slot, 1] = pltpu.roll(vx, width - t_new, 1)[:, :wb].reshape(n_heads, HEAD_DIM, wb)

    q_all = jnp.concatenate([q] * n_heads, axis=0)
    r_head = lax.broadcasted_iota(jnp.int32, q_all.shape, 0) // q_rows
    c_head = lax.broadcasted_iota(jnp.int32, q_all.shape, 1) // HEAD_DIM
    q_all = jnp.where(r_head == c_head, q_all, 0.0).astype(BF16)
    s = jnp.where(valid, _dot(q_all, kx.astype(BF16)), NEG)
    m = jnp.max(s, axis=1, keepdims=True)
    e = jnp.exp2(s - m)
    den = jnp.sum(e, axis=1, keepdims=True)
    o_all = _dot_nt(e.astype(BF16), vx.astype(BF16)) / den
    lse = m * LN2 + jnp.log(den)
    o_ref[slot] = jnp.concatenate(
        [o_all[h * q_rows:(h + 1) * q_rows, h * HEAD_DIM:(h + 1) * HEAD_DIM] for h in range(n_heads)], axis=1)
    lse_ref[slot] = functools.reduce(jnp.add, [
        jnp.where(lane == first_head + h, lse[h * q_rows:(h + 1) * q_rows], 0.0) for h in range(n_heads)])


def _merge_kernel(x_ref, o0_ref, o1_ref, o2_ref, l0_ref, l1_ref, l2_ref, pe_ref,
                  gmix_ref, wu_ref, wva_ref, wga0_ref, wga1_ref, wgb0_ref, wgb1_ref,
                  gv_ref, ws_ref, bias_ref, woa_ref, wob_ref, wo_ref,
                  gffn_ref, wff1_ref, wff2_ref, gple_ref, wpg_ref, wpp_ref,
                  *rest, prompt, tm, n_tiles, sample):
    if sample is None:
        (y_ref, va_ref), scratch = rest[:2], rest[2:]
    else:
        task_ins, (y_ref, va_ref), task_outs, scratch = rest[:4], rest[4:6], rest[6:9], rest[9:]
    d_model = x_ref.shape[1]
    d_a = gv_ref.shape[1]
    c_g = d_a // G_A
    x = x_ref[...]
    h = _rms(x, gmix_ref[...]).astype(BF16)

    u = jax.nn.gelu(_dot(h, wu_ref[...]))
    va = _rms(jax.nn.gelu(_dot(h, wva_ref[...])), gv_ref[...])
    if prompt:
        @pl.when(pl.program_id(1) == n_tiles - 1)
        def _():
            va_ref[0] = va[tm - CHUNK:tm, :]
    else:
        va_ref[...] = va
    va16 = va.astype(BF16)
    r_i = lax.broadcasted_iota(jnp.int32, (CHUNK, CHUNK), 0)
    c_i = lax.broadcasted_iota(jnp.int32, (CHUNK, CHUNK), 1)
    w_tril = [jnp.where(r_i >= c_i, ws_ref[g], jnp.zeros((CHUNK, CHUNK), BF16)) for g in range(G_A)]
    mix_rows = []
    for ci in range(tm // CHUNK):
        rs = slice(ci * CHUNK, (ci + 1) * CHUNK)
        cols = [_dot(w_tril[g], va16[rs, g * c_g:(g + 1) * c_g]) for g in range(G_A)]
        mix_rows.append(jnp.concatenate(cols, axis=1) + bias_ref[...])
    a = u * jnp.concatenate(mix_rows, axis=0)

    o_refs, l_refs = (o0_ref, o1_ref, o2_ref), (l0_ref, l1_ref, l2_ref)
    o_slabs, lses = [], []
    for gi, (_, dil) in enumerate(GROUPS):
        if not prompt or dil == 1:
            o_slabs.append([o_refs[gi][:, cc * LANES:(cc + 1) * LANES].astype(F32)
                            for cc in range(GROUP_SLABS)])
            lses.append(l_refs[gi][...])
            continue
        o_sc, l_sc = scratch[2 * (gi - 1)], scratch[2 * (gi - 1) + 1]
        piece = tm // dil
        for r in range(dil):
            rows = slice(r * piece, (r + 1) * piece)
            for cc in range(GROUP_SLABS):
                o_sc[cc, pl.ds(r, piece, stride=dil), :] = (
                    o_refs[gi][rows, cc * LANES:(cc + 1) * LANES].astype(F32))
            l_sc[pl.ds(r, piece, stride=dil), :] = l_refs[gi][rows, :]
        o_slabs.append([o_sc[cc] for cc in range(GROUP_SLABS)])
        lses.append(l_sc[...])

    m = jnp.maximum(jnp.maximum(lses[0], lses[1]), lses[2])
    ws = [jnp.exp(l - m) for l in lses]
    inv = 1.0 / (ws[0] + ws[1] + ws[2])
    ws = [w * inv for w in ws]
    lane = lax.broadcasted_iota(jnp.int32, (tm, LANES), 1)
    first_head = lane < HEAD_DIM
    b_cols = []
    for p in range(GROUP_SLABS):
        acc = jnp.zeros((tm, LANES), F32)
        for w, slabs in zip(ws, o_slabs):
            w_pair = jnp.where(first_head, w[:, 2 * p:2 * p + 1], w[:, 2 * p + 1:2 * p + 2])
            acc = acc + w_pair * slabs[p]
        b_cols.append(acc)
    b = jnp.concatenate(b_cols, axis=1)

    gate_a = jax.nn.sigmoid(jnp.concatenate([_dot(h, wga0_ref[...]), _dot(h, wga1_ref[...])], axis=1))
    gate_b = jax.nn.sigmoid(jnp.concatenate([_dot(h, wgb0_ref[...]), _dot(h, wgb1_ref[...])], axis=1))
    merged = gate_a * _dot(a.astype(BF16), woa_ref[...]) + gate_b * _dot(b.astype(BF16), wob_ref[...])
    x = x + _dot(merged.astype(BF16), wo_ref[...])

    h2 = _rms(x, gffn_ref[...]).astype(BF16)
    d_ff = wff1_ref.shape[1]
    ff_step = 1024
    ffn = jnp.zeros((tm, d_model), F32)
    for c0 in range(0, d_ff, ff_step):
        hid = jnp.square(jnp.maximum(_dot(h2, wff1_ref[:, c0:c0 + ff_step]), 0.0))
        ffn = ffn + _dot(hid.astype(BF16), wff2_ref[c0:c0 + ff_step, :])
    x = x + ffn

    if sample is not None:
        parts = H_G // task_ins[3].shape[2]
        step = pl.program_id(0) * n_tiles + pl.program_id(1)
        _sample_attention(*task_ins, *task_outs, step // parts, dil=sample[0],
                          first_head=(step % parts) * (H_G // parts), t_new=sample[1])

    gate = jax.nn.sigmoid(_dot(_rms(x, gple_ref[...]).astype(BF16), wpg_ref[...]))
    y_ref[...] = x + gate * _dot(pe_ref[...].astype(BF16), wpp_ref[...])


def _merge_call(x2, outs, lses, pe2, wts, ws_mat, bias, *, prompt, batch, tm, sample_task=None, t_new=None):
    n_tok, d_model = x2.shape
    seq = n_tok // batch
    n_tiles = seq // tm
    d_a = wts["g_v"].shape[1]
    tile_index = lambda b, i: b * n_tiles + i
    tok = lambda b, i: (tile_index(b, i), 0)
    row_spec = lambda width: pl.BlockSpec((tm, width), tok)

    names = ("g_mix", "w_in", "w_in", "w_in", "w_in", "w_in", "w_in", "g_v", "ws", "bias", "w_oa", "w_ob", "w_o",
             "g_ffn", "w_ff1", "w_ff2", "g_ple", "w_ple_gate", "w_ple_proj")
    consts = dict(wts, ws=ws_mat, bias=bias)
    half_model = d_model // 2
    assert d_a == half_model and wts["gate_col0"] % half_model == 0
    g0 = wts["gate_col0"] // half_model
    w_in_blocks = iter([0, 1, g0, g0 + 1, g0 + 2, g0 + 3])
    const_specs = [_column_block(d_model, half_model, next(w_in_blocks)) if k == "w_in"
                   else _resident(consts[k].shape) for k in names]
    in_specs = ([row_spec(d_model)]
                + [row_spec(D_GROUP)] * N_GROUPS + [row_spec(LANES)] * N_GROUPS
                + [row_spec(pe2.shape[1])] + const_specs)
    scratch = []
    if prompt:
        va_shape = jax.ShapeDtypeStruct((batch, CHUNK, d_a), F32)
        va_spec = pl.BlockSpec((1, CHUNK, d_a), lambda b, i: (b, 0, 0))
        for _, dil in GROUPS:
            if dil > 1:
                scratch += [pltpu.VMEM((GROUP_SLABS, tm, LANES), F32), pltpu.VMEM((tm, LANES), F32)]
    else:
        va_shape = jax.ShapeDtypeStruct((n_tok, d_a), F32)
        va_spec = row_spec(d_a)
    operands = [x2, *outs, *lses, pe2, *[consts[k] for k in names]]
    out_specs = [row_spec(d_model), va_spec]
    out_shape = [jax.ShapeDtypeStruct((n_tok, d_model), F32), va_shape]
    sample = None
    if sample_task is not None:
        task = sample_task
        n_seq, q_rows, _ = task["q"].shape
        n_h, wb = task["n_heads"], task["buf"].shape[4]
        parts = H_G // n_h
        assert n_seq * parts == batch * n_tiles
        cols = n_h * HEAD_DIM
        col0 = task["group"] * D_GROUP // cols
        seq_of = lambda b, i: tile_index(b, i) // parts
        part_of = lambda b, i: tile_index(b, i) % parts
        buf_spec = pl.BlockSpec((1, 2, n_h, HEAD_DIM, wb), lambda b, i: (seq_of(b, i), 0, part_of(b, i), 0, 0))
        new_spec = pl.BlockSpec((cols, LANES), lambda b, i: (col0 + part_of(b, i), 0))
        in_specs += [pl.BlockSpec((1, q_rows, cols), lambda b, i: (seq_of(b, i), 0, col0 + part_of(b, i))),
                     new_spec, new_spec, buf_spec]
        operands += [task["q"], task["knt"], task["vnt"], task["buf"]]
        out_specs += [pl.BlockSpec((1, q_rows, cols), lambda b, i: (seq_of(b, i), 0, part_of(b, i))),
                      pl.BlockSpec((1, None, q_rows, LANES), lambda b, i: (seq_of(b, i), part_of(b, i), 0, 0)),
                      buf_spec]
        out_shape += [jax.ShapeDtypeStruct((n_seq, q_rows, D_GROUP), F32),
                      jax.ShapeDtypeStruct((n_seq, parts, q_rows, LANES), F32),
                      jax.ShapeDtypeStruct(task["buf"].shape, F32)]
        sample = (GROUPS[task["group"]][1], t_new)
    return pl.pallas_call(
        functools.partial(_merge_kernel, prompt=prompt, tm=tm, n_tiles=n_tiles, sample=sample),
        grid=(batch, n_tiles),
        in_specs=in_specs,
        out_specs=out_specs,
        out_shape=out_shape,
        scratch_shapes=scratch,
        compiler_params=pltpu.CompilerParams(
            dimension_semantics=("arbitrary", "arbitrary"), vmem_limit_bytes=VMEM_LIMIT),
        name="merge_prompt" if prompt else "merge_sample",
    )(*operands)


def kernel(x_prompt, x_sample, p_prompt, p_sample, cache_kv_w128, cache_kv_w512, cache_kv_w2048,
           g_mix, w_in, g_v, w_s, b_s, g_q, g_k, w_oa, w_ob, w_o,
           g_ffn, w_ff1, w_ff2, g_ple, w_ple_gate, w_ple_proj):
    depth = w_in.shape[0]
    batch, seq, d_model = x_prompt.shape
    dec_batch, t_new, _ = x_sample.shape
    d_a = g_v.shape[1]
    n_sample = dec_batch * t_new
    assert n_sample == CHUNK and w_in.shape[2] == 2 * d_a + 3 * D_QKV + 2 * d_model
    caches = (cache_kv_w128, cache_kv_w512, cache_kv_w2048)

    xp = x_prompt.reshape(batch * seq, d_model)
    xs = x_sample.reshape(n_sample, d_model)
    pos_p = np.arange(seq, dtype=np.int32)
    pos_s = PAST_LEN + np.tile(np.arange(t_new, dtype=np.int32), dec_batch)

    kv_p = [[] for _ in GROUPS]
    kv_s = [[] for _ in GROUPS]
    cv_p, cv_s = [], []
    for l in range(depth):
        w_in16 = w_in[l].astype(BF16)
        qkv_col0 = 2 * d_a
        wts = {
            "g_mix": g_mix[l][None, :], "w_in": w_in16, "gate_col0": qkv_col0 + 3 * D_QKV,
            "g_v": g_v[l][None, :],
            "w_oa": w_oa[l].astype(BF16), "w_ob": w_ob[l].astype(BF16), "w_o": w_o[l].astype(BF16),
            "g_ffn": g_ffn[l][None, :], "w_ff1": w_ff1[l].astype(BF16), "w_ff2": w_ff2[l].astype(BF16),
            "g_ple": g_ple[l][None, :], "w_ple_gate": w_ple_gate[l].astype(BF16),
            "w_ple_proj": w_ple_proj[l].astype(BF16),
        }
        c_g = d_a // G_A
        ws_p = w_s[l].astype(BF16)
        bias_p = jnp.repeat(b_s[l].T, c_g, axis=1)
        tok = np.arange(CHUNK)
        same_seq = jnp.asarray(tok[:, None] // t_new == tok[None, :] // t_new)
        row_of = jnp.asarray(tok[:, None] % t_new == np.arange(t_new)[None, :], dtype=F32)
        ws_s = jnp.einsum("rt,gts,cs->grc", row_of, w_s[l][:, :t_new, :t_new], row_of,
                          precision=lax.Precision.HIGHEST)
        ws_s = jnp.where(same_seq, ws_s, 0.0).astype(BF16)
        reps = CHUNK // t_new
        bias_s = jnp.tile(jnp.repeat(b_s[l][:, :t_new].T, c_g, axis=1), (reps, 1))

        tm = TOKEN_TILE
        res = _qkv_call(xp, pos_p, g_mix[l], w_in16, qkv_col0, g_q[l], g_k[l], prompt=True, batch=batch,
                        tm=QKV_TILE)
        q_g, k_g, v_g, kv_tails = res[0:3], res[3:6], res[6:9], res[9:12]
        qs, knt, vnt = _qkv_call(
            xs, pos_s, g_mix[l], w_in16, qkv_col0, g_q[l], g_k[l], prompt=False, batch=1, tm=n_sample)

        q_rows = 8
        assert LANES % t_new == 0 and t_new <= q_rows and N_GROUPS == 3
        q_pad = jnp.pad(qs.reshape(dec_batch, t_new, D_QKV), ((0, 0), (0, q_rows - t_new), (0, 0)))
        bufs = [jnp.transpose(c[l], (0, 1, 3, 4, 2)) for c in caches]
        task = lambda g, n_h, h0=0, **kw: dict(group=g, first_head=h0, n_heads=n_h, q=q_pad, knt=knt, vnt=vnt,
                                               buf=bufs[g], **kw)
        half = H_G // 2
        o2, l2, ((so1a, sl1a, nbuf1),) = _attn_call(q_g[2], k_g[2], v_g[2], 2, batch, seq,
                                                  [task(1, half)], t_new)
        o1, l1, ((so1b, sl1b, nbuf1),) = _attn_call(q_g[1], k_g[1], v_g[1], 1, batch, seq,
                                                  [task(1, half, half, nbuf=nbuf1)], t_new)
        so1, sl1 = jnp.concatenate([so1a, so1b], axis=2), sl1a + sl1b
        o0, l0, ((so0, sl0, nbuf0),) = _attn_call(q_g[0], k_g[0], v_g[0], 0, batch, seq, [task(0, H_G)], t_new)

        xp, va_p, so2, sl2, nbuf2 = _merge_call(
            xp, (o0, o1, o2), (l0, l1, l2), p_prompt[l].reshape(batch * seq, -1), wts, ws_p, bias_p,
            prompt=True, batch=batch, tm=tm, sample_task=task(2, H_G // 2), t_new=t_new)
        s_outs = (so0, so1, so2)
        s_lses = (sl0, sl1, jnp.sum(sl2, axis=1))
        for gi, nbuf in enumerate((nbuf0, nbuf1, nbuf2)):
            kv_s[gi].append(jnp.transpose(nbuf, (0, 1, 4, 2, 3)))
        for gi, st in enumerate(kv_tails):
            kv_p[gi].append(jnp.transpose(st, (0, 1, 4, 2, 3)))
        cv_p.append(va_p)

        s_outs = [o[:, :t_new].reshape(n_sample, D_GROUP) for o in s_outs]
        s_lses = [s[:, :t_new].reshape(n_sample, LANES) for s in s_lses]
        xs, va_s = _merge_call(xs, s_outs, s_lses, p_sample[l].reshape(n_sample, -1), wts, ws_s, bias_s,
                               prompt=False, batch=1, tm=n_sample)
        cv_s.append(va_s.reshape(dec_batch, t_new, d_a))

    return (xp.reshape(batch, seq, d_model), xs.reshape(dec_batch, t_new, d_model),
            jnp.stack(kv_p[0]), jnp.stack(kv_p[1]), jnp.stack(kv_p[2]),
            jnp.stack(kv_s[0]), jnp.stack(kv_s[1]), jnp.stack(kv_s[2]),
            jnp.stack(cv_p), jnp.stack(cv_s))
```

```python
import functools
import math

import jax
import jax.numpy as jnp
import numpy as np
from jax import lax
from jax.experimental import pallas as pl
from jax.experimental.pallas import tpu as pltpu

F32 = jnp.float32
BF16 = jnp.bfloat16

LANES = 128
BF16_ROWS = 16
MXU_DIM = 256
TOKEN_TILE = 256
QKV_TILE = 512
HEAD_DIM = 64
H_G = 8
HEADS_PER_VREG = LANES // HEAD_DIM
GROUPS = ((128, 1), (512, 4), (2048, 16))
N_GROUPS = len(GROUPS)
D_GROUP = H_G * HEAD_DIM
GROUP_SLABS = D_GROUP // LANES
D_QKV = N_GROUPS * D_GROUP
N_KEYS = 129
SUB_BLOCK = 128
ROT_DIM = HEAD_DIM // 4
ROT_HALF = ROT_DIM // 2
ROPE_THETA = 500000.0
CHUNK = 128
G_A = 4
PAST_LEN = 8192
ATTN_SUB_BLOCKS = 8
EPS = 1e-6
NEG = -1e30
LN2 = math.log(2.0)
LOG2E = 1.0 / LN2
VMEM_LIMIT = 62 * 1024 * 1024


def _rms(x, g):
    return x * lax.rsqrt(jnp.mean(x * x, axis=-1, keepdims=True) + EPS) * g


def _resident(shape):
    zeros = (0,) * len(shape)
    return pl.BlockSpec(shape, lambda *_: zeros, pipeline_mode=pl.Buffered(1))


def _dot(a, b):
    return jnp.dot(a, b, preferred_element_type=F32)


def _dot_nt(a, b):
    return lax.dot_general(a, b, (((1,), (1,)), ((), ())), preferred_element_type=F32)


def _project_qkv(x_ref, gmix_ref, gq_ref, gk_ref, rc_ref, ru_ref, rd_ref, bd_ref, w_refs, perm_refs,
                 dils, emit):
    rows = x_ref.shape[0]
    tile = min(rows, TOKEN_TILE)
    starts = range(0, rows, tile)
    h = _rms(x_ref[...], gmix_ref[...]).astype(BF16)

    def permuted(p_ref):
        parts = [_dot(p_ref[...], h[s:s + tile]).astype(BF16) for s in starts]
        return parts[0] if len(parts) == 1 else jnp.concatenate(parts, axis=0)

    hs = [h if p_ref is None else permuted(p_ref) for p_ref in perm_refs]
    bd = bd_ref[...]
    wide = bd.shape[0]

    def table(ref, dil):
        if dil == 1:
            return ref[...]
        return jnp.concatenate([ref[pl.ds(s + r, tile // dil, stride=dil), :]
                                for s in starts for r in range(dil)], axis=0)

    tables = {dil: tuple(table(ref, dil) for ref in (rc_ref, ru_ref, rd_ref)) for dil in set(dils)}

    for ti, g_ref in enumerate((gq_ref, gk_ref, None)):
        for gi in range(N_GROUPS):
            z = _dot(hs[gi], w_refs[ti * N_GROUPS + gi][...])
            if g_ref is None:
                emit(ti, gi, [z[:, c * LANES:(c + 1) * LANES] for c in range(GROUP_SLABS)])
                continue
            rc, ru, rd = tables[dils[gi]]

            def rot(zn):
                return zn * rc + pltpu.roll(zn, LANES - ROT_HALF, 1) * ru + pltpu.roll(zn, ROT_HALF, 1) * rd

            vals = []
            for c in range(0, D_GROUP, wide):
                zc = z[:, c:c + wide]
                ss = _dot((zc * zc).astype(BF16), bd)
                zn = zc * lax.rsqrt(ss * (1.0 / HEAD_DIM) + EPS)
                vals += [rot(zn[:, j:j + LANES] * g_ref[...]) for j in range(0, wide, LANES)]
            emit(ti, gi, vals)


def _qkv_prompt_kernel(*refs, tm, n_tiles):
    n_perm = sum(dil > 1 for _, dil in GROUPS)
    n_w = 3 * N_GROUPS
    ins, w_refs, perms, rest = refs[:8], refs[8:8 + n_w], refs[8 + n_w:8 + n_w + n_perm], refs[8 + n_w + n_perm:]
    outs, kv_refs = rest[:3 * N_GROUPS], rest[3 * N_GROUPS:4 * N_GROUPS]
    kv_slabs, order_slab = rest[4 * N_GROUPS:4 * N_GROUPS + 2], rest[4 * N_GROUPS + 2]
    perms = list(perms)
    perm_refs = [perms.pop(0) if dil > 1 else None for _, dil in GROUPS]

    def emit(ti, gi, vals):
        out = outs[ti * N_GROUPS + gi]
        for cc, val in enumerate(vals):
            out[:, cc * LANES:(cc + 1) * LANES] = val.astype(BF16)
            if ti > 0:
                kv_slabs[ti - 1][gi * GROUP_SLABS + cc] = val

    _project_qkv(*ins, w_refs, perm_refs, [dil for _, dil in GROUPS], emit)

    i = pl.program_id(1)
    seq = n_tiles * tm
    for gi, (win, dil) in enumerate(GROUPS):
        rows = min(win, tm)
        first = (seq - win) // tm if win >= tm else n_tiles - 1
        piece = TOKEN_TILE // dil

        @pl.when(i >= first)
        def _(gi=gi, dil=dil, rows=rows, piece=piece):
            for kv in range(2):
                for cc in range(GROUP_SLABS):
                    c = gi * GROUP_SLABS + cc
                    if dil > 1:
                        for s in range(0, tm, TOKEN_TILE):
                            for r in range(dil):
                                order_slab[pl.ds(s + r, piece, stride=dil), :] = (
                                    kv_slabs[kv][c, s + r * piece:s + (r + 1) * piece, :])
                        val = order_slab[...]
                    else:
                        val = kv_slabs[kv][c]
                    t = val[tm - rows:tm].T
                    for hh in range(HEADS_PER_VREG):
                        kv_refs[gi][0, kv, cc * HEADS_PER_VREG + hh] = t[hh * HEAD_DIM:(hh + 1) * HEAD_DIM, :]


def _qkv_sample_kernel(*refs):
    n_w = 3 * N_GROUPS
    ins, w_refs, outs = refs[:8], refs[8:8 + n_w], refs[8 + n_w:11 + n_w]

    def emit(ti, gi, vals):
        for cc, val in enumerate(vals):
            c = gi * GROUP_SLABS + cc
            if ti == 0:
                outs[ti][:, c * LANES:(c + 1) * LANES] = val
            else:
                outs[ti][c * LANES:(c + 1) * LANES, :] = val.T

    _project_qkv(*ins, w_refs, [None] * N_GROUPS, [1] * N_GROUPS, emit)


def _residue_major_source(dil, tm):
    row = np.arange(tm, dtype=np.int32)
    piece = tm // dil
    return (row % piece) * dil + row // piece


def _rotary_tables(pos):
    inv = np.float32(ROPE_THETA) ** (-np.arange(ROT_HALF, dtype=np.float32) * np.float32(2.0 / ROT_DIM))
    ang = pos.astype(np.float32)[:, None] * inv[None, :]
    cos, sin = np.cos(ang), np.sin(ang)
    n = pos.shape[0]
    pad = np.zeros((n, HEAD_DIM - ROT_DIM), np.float32)
    zero = np.zeros((n, ROT_HALF), np.float32)
    rc = np.concatenate([cos, cos, pad + 1.0], axis=1)
    ru = np.concatenate([-sin, zero, pad], axis=1)
    rd = np.concatenate([zero, sin, pad], axis=1)
    return [jnp.asarray(np.tile(t, (1, HEADS_PER_VREG)).astype(np.float32)) for t in (rc, ru, rd)]


def _column_block(n_rows, width, index):
    return pl.BlockSpec((n_rows, width), lambda *_: (0, index), pipeline_mode=pl.Buffered(1))


def _qkv_call(x2, pos, g_mix, w_in, qkv_col0, g_q, g_k, *, prompt, batch, tm):
    n_tok, d_model = x2.shape
    seq = n_tok // batch
    n_tiles = seq // tm
    perms = []
    if prompt:
        assert tm % TOKEN_TILE == 0 and all(TOKEN_TILE % (dil * BF16_ROWS) == 0 for _, dil in GROUPS)
        for _, dil in GROUPS:
            if dil > 1:
                src = _residue_major_source(dil, TOKEN_TILE)
                perms.append(jnp.asarray(src[:, None] == np.arange(TOKEN_TILE)[None, :], dtype=BF16))
    rot = _rotary_tables(pos)
    gq = jnp.tile(g_q.astype(F32), HEADS_PER_VREG)[None, :] * (LOG2E * HEAD_DIM ** -0.5)
    gk = jnp.tile(g_k.astype(F32), HEADS_PER_VREG)[None, :]
    lane = np.arange(MXU_DIM)
    bd = jnp.asarray(lane[:, None] // HEAD_DIM == lane[None, :] // HEAD_DIM, dtype=BF16)

    tile_index = lambda b, i: b * n_tiles + i
    tok = lambda b, i: (tile_index(b, i), 0)
    in_specs = [
        pl.BlockSpec((tm, d_model), tok),
        _resident((1, d_model)),
        _resident((1, LANES)),
        _resident((1, LANES)),
        pl.BlockSpec((tm, LANES), lambda b, i: (i, 0)),
        pl.BlockSpec((tm, LANES), lambda b, i: (i, 0)),
        pl.BlockSpec((tm, LANES), lambda b, i: (i, 0)),
        _resident((MXU_DIM, MXU_DIM)),
    ] + [_column_block(d_model, D_GROUP, qkv_col0 // D_GROUP + j) for j in range(3 * N_GROUPS)]
    in_specs += [_resident(p.shape) for p in perms]
    assert qkv_col0 % D_GROUP == 0
    if prompt:
        out_shape = [jax.ShapeDtypeStruct((n_tok, D_GROUP), BF16)] * (3 * N_GROUPS)
        out_specs = [pl.BlockSpec((tm, D_GROUP), tok)] * (3 * N_GROUPS)
        for win, _ in GROUPS:
            rows = min(win, tm)
            first = (seq - win) // tm if win >= tm else n_tiles - 1
            out_shape.append(jax.ShapeDtypeStruct((batch, 2, H_G, HEAD_DIM, win), F32))
            out_specs.append(pl.BlockSpec(
                (1, 2, H_G, HEAD_DIM, rows),
                lambda b, i, first=first: (b, 0, 0, 0, jnp.maximum(i - first, 0))))
        scratch = [pltpu.VMEM((D_QKV // LANES, tm, LANES), F32)] * 2 + [pltpu.VMEM((tm, LANES), F32)]
        body = functools.partial(_qkv_prompt_kernel, tm=tm, n_tiles=n_tiles)
    else:
        assert n_tiles == 1 and batch == 1 and tm == LANES
        out_shape = [jax.ShapeDtypeStruct((n_tok, D_QKV), F32)] + [jax.ShapeDtypeStruct((D_QKV, n_tok), F32)] * 2
        out_specs = [pl.BlockSpec((tm, D_QKV), tok)] + [pl.BlockSpec((D_QKV, tm), lambda b, i: (0, 0))] * 2
        scratch = []
        body = _qkv_sample_kernel

    return pl.pallas_call(
        body,
        grid=(batch, n_tiles),
        in_specs=in_specs,
        out_specs=out_specs,
        out_shape=out_shape,
        scratch_shapes=scratch,
        compiler_params=pltpu.CompilerParams(
            dimension_semantics=("arbitrary", "arbitrary"), vmem_limit_bytes=VMEM_LIMIT),
        name="qkv_prompt" if prompt else "qkv_sample",
    )(x2, g_mix[None, :], gq, gk, *rot, bd, *[w_in] * (3 * N_GROUPS), *perms)


def _attn_kernel(q_ref, kc_ref, kp_ref, vc_ref, vp_ref, *rest, dil, sub_blocks, tasks, n_alias, t_new):
    n_in = 4 * len(tasks)
    n_skip = n_in + n_alias
    task_ins, (o_ref, lse_ref), task_outs = rest[:n_in], rest[n_skip:n_skip + 2], rest[n_skip + 2:]
    step = (pl.program_id(0) * pl.num_programs(1) + pl.program_id(1)) * pl.num_programs(2) + pl.program_id(2)

    has_prev = pl.program_id(1) > 0
    piece = TOKEN_TILE // dil

    span_tiles = SUB_BLOCK // piece

    def residue_rows(i, span=0):
        return [(span * span_tiles + t, 0, slice(i * piece, (i + 1) * piece)) for t in range(span_tiles)]

    def load(ref, rows, sl):
        parts = [ref[rs + (sl,)] for rs in rows]
        return parts[0] if len(parts) == 1 else jnp.concatenate(parts, axis=0)

    def store(ref, rows, sl, val):
        n = val.shape[0] // len(rows)
        for t, rs in enumerate(rows):
            ref[rs + (sl,)] = val[t * n:(t + 1) * n]
    key = lax.broadcasted_iota(jnp.int32, (2 * SUB_BLOCK, HEADS_PER_VREG * SUB_BLOCK), 0)
    qry = lax.broadcasted_iota(jnp.int32, (2 * SUB_BLOCK, HEADS_PER_VREG * SUB_BLOCK), 1) % SUB_BLOCK
    vis_cur = key <= qry
    vis_prev = key - SUB_BLOCK >= qry
    lane16 = lax.broadcasted_iota(jnp.int32, (SUB_BLOCK, LANES), 1).astype(F32).astype(BF16)
    head_lanes = (lane16 < HEAD_DIM, lane16 >= HEAD_DIM)
    eye_r = lax.broadcasted_iota(jnp.int32, (LANES, LANES), 0)
    eye_c = lax.broadcasted_iota(jnp.int32, (LANES, LANES), 1)
    eye = (eye_r == eye_c).astype(F32).astype(BF16)
    ones_rows = jnp.ones((BF16_ROWS, 2 * SUB_BLOCK), BF16)
    head_row = lax.broadcasted_iota(jnp.int32, (H_G, SUB_BLOCK), 0)

    pad = jnp.zeros((LANES - H_G, SUB_BLOCK), F32)
    slabs = [slice(p * LANES, (p + 1) * LANES) for p in range(GROUP_SLABS)]

    if dil == 1:
        blocks = [([(slice(0, SUB_BLOCK),)], kp_ref, vp_ref, [(slice(0, SUB_BLOCK),)], has_prev)]
        blocks += [([(slice(i * SUB_BLOCK, (i + 1) * SUB_BLOCK),)], kc_ref, vc_ref,
                    [(slice((i - 1) * SUB_BLOCK, i * SUB_BLOCK),)], True) for i in range(1, sub_blocks)]
    elif dil >= sub_blocks:
        blocks = [(residue_rows(i), kp_ref, vp_ref, residue_rows(i), has_prev)
                  for i in range(sub_blocks)]
    else:
        blocks = [(residue_rows(i), kp_ref, vp_ref, residue_rows(i), has_prev) for i in range(dil)]
        blocks += [(residue_rows(i, sp), kc_ref, vc_ref, residue_rows(i, sp - 1), True)
                   for sp in range(1, sub_blocks // dil) for i in range(dil)]

    def score_stage(blks):
        scores, vts = [], []
        for rows, kp_src, vp_src, prev_rows, _ in blks:
            for sl in slabs:
                q = load(q_ref, rows, sl)
                qq = jnp.concatenate([jnp.where(hl, q, jnp.zeros_like(q)) for hl in head_lanes], axis=0)
                kk = jnp.concatenate([load(kc_ref, rows, sl), load(kp_src, prev_rows, sl)], axis=0)
                vv = jnp.concatenate([load(vc_ref, rows, sl), load(vp_src, prev_rows, sl)], axis=0)
                scores.append(_dot_nt(kk, qq))
                vts.append(jnp.concatenate([_dot_nt(eye, vv).astype(BF16), ones_rows], axis=0))
        return scores, vts

    def softmax_stage(blks, scores):
        maxes, probs = [], []
        for bi, (_, _, _, _, has_prev) in enumerate(blks):
            visible = jnp.logical_or(vis_cur, jnp.logical_and(vis_prev, has_prev))
            for s in scores[bi * GROUP_SLABS:(bi + 1) * GROUP_SLABS]:
                s = jnp.where(visible, s, NEG)
                m = jnp.max(s, axis=0, keepdims=True)
                maxes.append(m)
                probs.append(jnp.exp2(s - m).astype(BF16))
        return maxes, probs

    def value_stage(vts, probs):
        return [_dot(vt, e) for vt, e in zip(vts, probs)]

    def store_stage(blks, outs, maxes):
        for bi, (rows, _, _, _, _) in enumerate(blks):
            lse_rows = jnp.zeros((H_G, SUB_BLOCK), F32)
            for p, sl in enumerate(slabs):
                ot, m = outs[bi * GROUP_SLABS + p], maxes[bi * GROUP_SLABS + p]
                den = ot[LANES:LANES + 1, :]
                lse = m * LN2 + jnp.log(den)
                inv = 1.0 / den
                o_halves = []
                for half in range(HEADS_PER_VREG):
                    qs = slice(half * SUB_BLOCK, (half + 1) * SUB_BLOCK)
                    o_halves.append(ot[half * HEAD_DIM:(half + 1) * HEAD_DIM, qs] * inv[:, qs])
                    lse_rows = jnp.where(head_row == p * HEADS_PER_VREG + half, lse[:, qs], lse_rows)
                store(o_ref, rows, sl, jnp.concatenate(o_halves, axis=0).T.astype(BF16))
            store(lse_ref, rows, slice(None), jnp.concatenate([lse_rows, pad], axis=0).T)

    first, second = blocks[:sub_blocks // 2], blocks[sub_blocks // 2:]
    scores_a, vts_a = score_stage(first)
    scores_b, vts_b = score_stage(second)
    for ti, (task_dil, first_head) in enumerate(tasks):
        slots = task_ins[4 * ti].shape[0]
        _sample_attention(*task_ins[4 * ti:4 * ti + 4], *task_outs[3 * ti:3 * ti + 3], step * slots,
                          dil=task_dil, first_head=first_head, t_new=t_new)
    maxes_a, probs_a = softmax_stage(first, scores_a)
    outs_a = value_stage(vts_a, probs_a)
    maxes_b, probs_b = softmax_stage(second, scores_b)
    outs_b = value_stage(vts_b, probs_b)
    store_stage(first, outs_a, maxes_a)
    store_stage(second, outs_b, maxes_b)


def _attn_call(q, k, v, gi, batch, seq, sample_tasks, t_new):
    dil = GROUPS[gi][1]
    n_tok = batch * seq
    sub_blocks = ATTN_SUB_BLOCKS
    assert dil % sub_blocks == 0 or sub_blocks % dil == 0
    if dil == 1:
        rows, parts = sub_blocks * SUB_BLOCK, 1
        assert seq % rows == 0
        steps = seq // rows
        view = lambda width: (n_tok, width)
        cur_map = lambda b, s, p: (b * steps + s, 0)
        cur = lambda width: pl.BlockSpec((rows, width), cur_map)
        prev = pl.BlockSpec((SUB_BLOCK, D_GROUP),
                            lambda b, s, p: (jnp.maximum((b * steps + s) * sub_blocks - 1, 0), 0))
    else:
        span, parts, spans = SUB_BLOCK * dil, max(dil // sub_blocks, 1), max(sub_blocks // dil, 1)
        assert span % TOKEN_TILE == 0 and seq % (span * spans) == 0
        assert TOKEN_TILE % (dil * BF16_ROWS) == 0
        steps, span_tiles, part_rows = seq // (span * spans), span // TOKEN_TILE, TOKEN_TILE // parts
        view = lambda width: (n_tok // TOKEN_TILE, parts, part_rows, width)
        cur_map = lambda b, s, p: (b * steps + s, p, 0, 0)
        cur = lambda width: pl.BlockSpec((spans * span_tiles, 1, part_rows, width), cur_map)
        prev = pl.BlockSpec((span_tiles, 1, part_rows, D_GROUP),
                            lambda b, s, p: (jnp.maximum((b * steps + s) * spans - 1, 0), p, 0, 0))
    q, k, v = (t.reshape(view(D_GROUP)) for t in (q, k, v))
    in_specs = [cur(D_GROUP), cur(D_GROUP), prev, cur(D_GROUP), prev]
    out_specs = [cur(D_GROUP), cur(LANES)]
    out_shape = [jax.ShapeDtypeStruct(view(D_GROUP), BF16), jax.ShapeDtypeStruct(view(LANES), F32)]
    operands, alias_operands, aliases = [q, k, k, v, v], [], {}
    seq_of = lambda b, s, p: (b * steps + s) * parts + p
    for task in sample_tasks:
        n_seq, q_rows, _ = task["q"].shape
        assert n_seq % (batch * steps * parts) == 0
        slots = n_seq // (batch * steps * parts)
        wb = task["buf"].shape[4]
        n_h, h0 = task["n_heads"], task["first_head"]
        cols = n_h * HEAD_DIM
        col_block = (task["group"] * D_GROUP + h0 * HEAD_DIM) // cols
        buf_spec = pl.BlockSpec((slots, 2, n_h, HEAD_DIM, wb),
                                lambda b, s, p, hb=h0 // n_h: (seq_of(b, s, p), 0, hb, 0, 0))
        new_spec = pl.BlockSpec((cols, LANES), lambda b, s, p, cb=col_block: (cb, 0))
        in_specs += [pl.BlockSpec((slots, q_rows, cols), lambda b, s, p, cb=col_block: (seq_of(b, s, p), 0, cb)),
                     new_spec, new_spec, buf_spec]
        operands += [task["q"], task["knt"], task["vnt"], task["buf"]]
        if task.get("nbuf") is not None:
            aliases[len(alias_operands)] = len(out_shape) + 2
            alias_operands.append(task["nbuf"])
        out_specs += [pl.BlockSpec((slots, q_rows, cols), lambda b, s, p: (seq_of(b, s, p), 0, 0)),
                      pl.BlockSpec((slots, q_rows, LANES), lambda b, s, p: (seq_of(b, s, p), 0, 0)), buf_spec]
        out_shape += [jax.ShapeDtypeStruct((n_seq, q_rows, cols), F32),
                      jax.ShapeDtypeStruct((n_seq, q_rows, LANES), F32),
                      jax.ShapeDtypeStruct(task["buf"].shape, F32)]
    n_main = len(operands)
    res = pl.pallas_call(
        functools.partial(_attn_kernel, dil=dil, sub_blocks=sub_blocks, n_alias=len(alias_operands), t_new=t_new,
                          tasks=tuple((GROUPS[t["group"]][1], t["first_head"]) for t in sample_tasks)),
        grid=(batch, steps, parts),
        in_specs=in_specs + [pl.BlockSpec(memory_space=pl.ANY)] * len(alias_operands),
        out_specs=out_specs,
        out_shape=out_shape,
        input_output_aliases={n_main + i: o for i, o in aliases.items()},
        compiler_params=pltpu.CompilerParams(
            dimension_semantics=("arbitrary", "arbitrary", "arbitrary"), vmem_limit_bytes=VMEM_LIMIT),
        name=f"attn_g{gi}",
    )(*operands, *alias_operands)
    task_res = [tuple(res[2 + 3 * i:5 + 3 * i]) for i in range(len(sample_tasks))]
    return res[0].reshape(n_tok, D_GROUP), res[1].reshape(n_tok, LANES), task_res


def _sample_attention(q_ref, knt_ref, vnt_ref, buf_ref, o_ref, lse_ref, nbuf_ref, first_seq, **kw):
    for slot in range(q_ref.shape[0]):
        _sample_attention_one(q_ref, knt_ref, vnt_ref, buf_ref, o_ref, lse_ref, nbuf_ref, slot,
                              first_seq + slot, **kw)


def _sample_attention_one(q_ref, knt_ref, vnt_ref, buf_ref, o_ref, lse_ref, nbuf_ref, slot, seq_index, *,
                          dil, first_head, t_new):
    assert dil & (dil - 1) == 0
    q_rows = q_ref.shape[1]
    n_heads, wb = buf_ref.shape[2], buf_ref.shape[4]
    width = wb + LANES
    shift = lax.rem(LANES - t_new * seq_index, LANES)
    knt = pltpu.roll(knt_ref[...], shift, 1)
    vnt = pltpu.roll(vnt_ref[...], shift, 1)
    q = q_ref[slot]

    t_row = lax.broadcasted_iota(jnp.int32, (q_rows, width), 0)
    p_col = lax.broadcasted_iota(jnp.int32, (q_rows, width), 1)
    back = wb + t_row - p_col
    valid = jnp.logical_and(jnp.logical_and(back >= 0, (back & (dil - 1)) == 0),
                            jnp.logical_and(back <= (N_KEYS - 1) * dil, p_col < wb + t_new))
    valid = jnp.concatenate([valid] * n_heads, axis=0)
    lane = lax.broadcasted_iota(jnp.int32, (q_rows, LANES), 1)
    cols = n_heads * HEAD_DIM

    kx = jnp.concatenate([buf_ref[slot, 0].reshape(cols, wb), knt], axis=1)
    vx = jnp.concatenate([buf_ref[slot, 1].reshape(cols, wb), vnt], axis=1)
    nbuf_ref[slot, 0] = pltpu.roll(kx, width - t_new, 1)[:, :wb].reshape(n_heads, HEAD_DIM, wb)
    nbuf_ref[slot, 1] = pltpu.roll(vx, width - t_new, 1)[:, :wb].reshape(n_heads, HEAD_DIM, wb)

    q_all = jnp.concatenate([q] * n_heads, axis=0)
    r_head = lax.broadcasted_iota(jnp.int32, q_all.shape, 0) // q_rows
    c_head = lax.broadcasted_iota(jnp.int32, q_all.shape, 1) // HEAD_DIM
    q_all = jnp.where(r_head == c_head, q_all, 0.0).astype(BF16)
    s = jnp.where(valid, _dot(q_all, kx.astype(BF16)), NEG)
    m = jnp.max(s, axis=1, keepdims=True)
    e = jnp.exp2(s - m)
    den = jnp.sum(e, axis=1, keepdims=True)
    o_all = _dot_nt(e.astype(BF16), vx.astype(BF16)) / den
    lse = m * LN2 + jnp.log(den)
    o_ref[slot] = jnp.concatenate(
        [o_all[h * q_rows:(h + 1) * q_rows, h * HEAD_DIM:(h + 1) * HEAD_DIM] for h in range(n_heads)], axis=1)
    lse_ref[slot] = functools.reduce(jnp.add, [
        jnp.where(lane == first_head + h, lse[h * q_rows:(h + 1) * q_rows], 0.0) for h in range(n_heads)])


def _merge_kernel(x_ref, o0_ref, o1_ref, o2_ref, l0_ref, l1_ref, l2_ref, pe_ref,
                  gmix_ref, wu_ref, wva_ref, wga0_ref, wga1_ref, wgb0_ref, wgb1_ref,
                  gv_ref, ws_ref, bias_ref, woa_ref, wob_ref, wo_ref,
                  gffn_ref, wff1_ref, wff2_ref, gple_ref, wpg_ref, wpp_ref,
                  *rest, prompt, tm, n_tiles, sample):
    if sample is None:
        (y_ref, va_ref), scratch = rest[:2], rest[2:]
    else:
        task_ins, (y_ref, va_ref), task_outs, scratch = rest[:4], rest[4:6], rest[6:9], rest[9:]
    d_model = x_ref.shape[1]
    d_a = gv_ref.shape[1]
    c_g = d_a // G_A
    x = x_ref[...]
    h = _rms(x, gmix_ref[...]).astype(BF16)

    u = jax.nn.gelu(_dot(h, wu_ref[...]))
    va = _rms(jax.nn.gelu(_dot(h, wva_ref[...])), gv_ref[...])
    if prompt:
        @pl.when(pl.program_id(1) == n_tiles - 1)
        def _():
            va_ref[0] = va[tm - CHUNK:tm, :]
    else:
        va_ref[...] = va
    va16 = va.astype(BF16)
    r_i = lax.broadcasted_iota(jnp.int32, (CHUNK, CHUNK), 0)
    c_i = lax.broadcasted_iota(jnp.int32, (CHUNK, CHUNK), 1)
    w_tril = [jnp.where(r_i >= c_i, ws_ref[g], jnp.zeros((CHUNK, CHUNK), BF16)) for g in range(G_A)]
    mix_rows = []
    for ci in range(tm // CHUNK):
        rs = slice(ci * CHUNK, (ci + 1) * CHUNK)
        cols = [_dot(w_tril[g], va16[rs, g * c_g:(g + 1) * c_g]) for g in range(G_A)]
        mix_rows.append(jnp.concatenate(cols, axis=1) + bias_ref[...])
    a = u * jnp.concatenate(mix_rows, axis=0)

    o_refs, l_refs = (o0_ref, o1_ref, o2_ref), (l0_ref, l1_ref, l2_ref)
    o_slabs, lses = [], []
    for gi, (_, dil) in enumerate(GROUPS):
        if not prompt or dil == 1:
            o_slabs.append([o_refs[gi][:, cc * LANES:(cc + 1) * LANES].astype(F32)
                            for cc in range(GROUP_SLABS)])
            lses.append(l_refs[gi][...])
            continue
        o_sc, l_sc = scratch[2 * (gi - 1)], scratch[2 * (gi - 1) + 1]
        piece = tm // dil
        for r in range(dil):
            rows = slice(r * piece, (r + 1) * piece)
            for cc in range(GROUP_SLABS):
                o_sc[cc, pl.ds(r, piece, stride=dil), :] = (
                    o_refs[gi][rows, cc * LANES:(cc + 1) * LANES].astype(F32))
            l_sc[pl.ds(r, piece, stride=dil), :] = l_refs[gi][rows, :]
        o_slabs.append([o_sc[cc] for cc in range(GROUP_SLABS)])
        lses.append(l_sc[...])

    m = jnp.maximum(jnp.maximum(lses[0], lses[1]), lses[2])
    ws = [jnp.exp(l - m) for l in lses]
    inv = 1.0 / (ws[0] + ws[1] + ws[2])
    ws = [w * inv for w in ws]
    lane = lax.broadcasted_iota(jnp.int32, (tm, LANES), 1)
    first_head = lane < HEAD_DIM
    b_cols = []
    for p in range(GROUP_SLABS):
        acc = jnp.zeros((tm, LANES), F32)
        for w, slabs in zip(ws, o_slabs):
            w_pair = jnp.where(first_head, w[:, 2 * p:2 * p + 1], w[:, 2 * p + 1:2 * p + 2])
            acc = acc + w_pair * slabs[p]
        b_cols.append(acc)
    b = jnp.concatenate(b_cols, axis=1)

    gate_a = jax.nn.sigmoid(jnp.concatenate([_dot(h, wga0_ref[...]), _dot(h, wga1_ref[...])], axis=1))
    gate_b = jax.nn.sigmoid(jnp.concatenate([_dot(h, wgb0_ref[...]), _dot(h, wgb1_ref[...])], axis=1))
    merged = gate_a * _dot(a.astype(BF16), woa_ref[...]) + gate_b * _dot(b.astype(BF16), wob_ref[...])
    x = x + _dot(merged.astype(BF16), wo_ref[...])

    h2 = _rms(x, gffn_ref[...]).astype(BF16)
    d_ff = wff1_ref.shape[1]
    ff_step = 1024
    ffn = jnp.zeros((tm, d_model), F32)
    for c0 in range(0, d_ff, ff_step):
        hid = jnp.square(jnp.maximum(_dot(h2, wff1_ref[:, c0:c0 + ff_step]), 0.0))
        ffn = ffn + _dot(hid.astype(BF16), wff2_ref[c0:c0 + ff_step, :])
    x = x + ffn

    if sample is not None:
        parts = H_G // task_ins[3].shape[2]
        step = pl.program_id(0) * n_tiles + pl.program_id(1)
        _sample_attention(*task_ins, *task_outs, step // parts, dil=sample[0],
                          first_head=(step % parts) * (H_G // parts), t_new=sample[1])

    gate = jax.nn.sigmoid(_dot(_rms(x, gple_ref[...]).astype(BF16), wpg_ref[...]))
    y_ref[...] = x + gate * _dot(pe_ref[...].astype(BF16), wpp_ref[...])


def _merge_call(x2, outs, lses, pe2, wts, ws_mat, bias, *, prompt, batch, tm, sample_task=None, t_new=None):
    n_tok, d_model = x2.shape
    seq = n_tok // batch
    n_tiles = seq // tm
    d_a = wts["g_v"].shape[1]
    tile_index = lambda b, i: b * n_tiles + i
    tok = lambda b, i: (tile_index(b, i), 0)
    row_spec = lambda width: pl.BlockSpec((tm, width), tok)

    names = ("g_mix", "w_in", "w_in", "w_in", "w_in", "w_in", "w_in", "g_v", "ws", "bias", "w_oa", "w_ob", "w_o",
             "g_ffn", "w_ff1", "w_ff2", "g_ple", "w_ple_gate", "w_ple_proj")
    consts = dict(wts, ws=ws_mat, bias=bias)
    half_model = d_model // 2
    assert d_a == half_model and wts["gate_col0"] % half_model == 0
    g0 = wts["gate_col0"] // half_model
    w_in_blocks = iter([0, 1, g0, g0 + 1, g0 + 2, g0 + 3])
    const_specs = [_column_block(d_model, half_model, next(w_in_blocks)) if k == "w_in"
                   else _resident(consts[k].shape) for k in names]
    in_specs = ([row_spec(d_model)]
                + [row_spec(D_GROUP)] * N_GROUPS + [row_spec(LANES)] * N_GROUPS
                + [row_spec(pe2.shape[1])] + const_specs)
    scratch = []
    if prompt:
        va_shape = jax.ShapeDtypeStruct((batch, CHUNK, d_a), F32)
        va_spec = pl.BlockSpec((1, CHUNK, d_a), lambda b, i: (b, 0, 0))
        for _, dil in GROUPS:
            if dil > 1:
                scratch += [pltpu.VMEM((GROUP_SLABS, tm, LANES), F32), pltpu.VMEM((tm, LANES), F32)]
    else:
        va_shape = jax.ShapeDtypeStruct((n_tok, d_a), F32)
        va_spec = row_spec(d_a)
    operands = [x2, *outs, *lses, pe2, *[consts[k] for k in names]]
    out_specs = [row_spec(d_model), va_spec]
    out_shape = [jax.ShapeDtypeStruct((n_tok, d_model), F32), va_shape]
    sample = None
    if sample_task is not None:
        task = sample_task
        n_seq, q_rows, _ = task["q"].shape
        n_h, wb = task["n_heads"], task["buf"].shape[4]
        parts = H_G // n_h
        assert n_seq * parts == batch * n_tiles
        cols = n_h * HEAD_DIM
        col0 = task["group"] * D_GROUP // cols
        seq_of = lambda b, i: tile_index(b, i) // parts
        part_of = lambda b, i: tile_index(b, i) % parts
        buf_spec = pl.BlockSpec((1, 2, n_h, HEAD_DIM, wb), lambda b, i: (seq_of(b, i), 0, part_of(b, i), 0, 0))
        new_spec = pl.BlockSpec((cols, LANES), lambda b, i: (col0 + part_of(b, i), 0))
        in_specs += [pl.BlockSpec((1, q_rows, cols), lambda b, i: (seq_of(b, i), 0, col0 + part_of(b, i))),
                     new_spec, new_spec, buf_spec]
        operands += [task["q"], task["knt"], task["vnt"], task["buf"]]
        out_specs += [pl.BlockSpec((1, q_rows, cols), lambda b, i: (seq_of(b, i), 0, part_of(b, i))),
                      pl.BlockSpec((1, None, q_rows, LANES), lambda b, i: (seq_of(b, i), part_of(b, i), 0, 0)),
                      buf_spec]
        out_shape += [jax.ShapeDtypeStruct((n_seq, q_rows, D_GROUP), F32),
                      jax.ShapeDtypeStruct((n_seq, parts, q_rows, LANES), F32),
                      jax.ShapeDtypeStruct(task["buf"].shape, F32)]
        sample = (GROUPS[task["group"]][1], t_new)
    return pl.pallas_call(
        functools.partial(_merge_kernel, prompt=prompt, tm=tm, n_tiles=n_tiles, sample=sample),
        grid=(batch, n_tiles),
        in_specs=in_specs,
        out_specs=out_specs,
        out_shape=out_shape,
        scratch_shapes=scratch,
        compiler_params=pltpu.CompilerParams(
            dimension_semantics=("arbitrary", "arbitrary"), vmem_limit_bytes=VMEM_LIMIT),
        name="merge_prompt" if prompt else "merge_sample",
    )(*operands)


def kernel(x_prompt, x_sample, p_prompt, p_sample, cache_kv_w128, cache_kv_w512, cache_kv_w2048,
           g_mix, w_in, g_v, w_s, b_s, g_q, g_k, w_oa, w_ob, w_o,
           g_ffn, w_ff1, w_ff2, g_ple, w_ple_gate, w_ple_proj):
    depth = w_in.shape[0]
    batch, seq, d_model = x_prompt.shape
    dec_batch, t_new, _ = x_sample.shape
    d_a = g_v.shape[1]
    n_sample = dec_batch * t_new
    assert n_sample == CHUNK and w_in.shape[2] == 2 * d_a + 3 * D_QKV + 2 * d_model
    caches = (cache_kv_w128, cache_kv_w512, cache_kv_w2048)

    xp = x_prompt.reshape(batch * seq, d_model)
    xs = x_sample.reshape(n_sample, d_model)
    pos_p = np.arange(seq, dtype=np.int32)
    pos_s = PAST_LEN + np.tile(np.arange(t_new, dtype=np.int32), dec_batch)

    kv_p = [[] for _ in GROUPS]
    kv_s = [[] for _ in GROUPS]
    cv_p, cv_s = [], []
    for l in range(depth):
        w_in16 = w_in[l].astype(BF16)
        qkv_col0 = 2 * d_a
        wts = {
            "g_mix": g_mix[l][None, :], "w_in": w_in16, "gate_col0": qkv_col0 + 3 * D_QKV,
            "g_v": g_v[l][None, :],
            "w_oa": w_oa[l].astype(BF16), "w_ob": w_ob[l].astype(BF16), "w_o": w_o[l].astype(BF16),
            "g_ffn": g_ffn[l][None, :], "w_ff1": w_ff1[l].astype(BF16), "w_ff2": w_ff2[l].astype(BF16),
            "g_ple": g_ple[l][None, :], "w_ple_gate": w_ple_gate[l].astype(BF16),
            "w_ple_proj": w_ple_proj[l].astype(BF16),
        }
        c_g = d_a // G_A
        ws_p = w_s[l].astype(BF16)
        bias_p = jnp.repeat(b_s[l].T, c_g, axis=1)
        tok = np.arange(CHUNK)
        same_seq = jnp.asarray(tok[:, None] // t_new == tok[None, :] // t_new)
        row_of = jnp.asarray(tok[:, None] % t_new == np.arange(t_new)[None, :], dtype=F32)
        ws_s = jnp.einsum("rt,gts,cs->grc", row_of, w_s[l][:, :t_new, :t_new], row_of,
                          precision=lax.Precision.HIGHEST)
        ws_s = jnp.where(same_seq, ws_s, 0.0).astype(BF16)
        reps = CHUNK // t_new
        bias_s = jnp.tile(jnp.repeat(b_s[l][:, :t_new].T, c_g, axis=1), (reps, 1))

        tm = TOKEN_TILE
        res = _qkv_call(xp, pos_p, g_mix[l], w_in16, qkv_col0, g_q[l], g_k[l], prompt=True, batch=batch,
                        tm=QKV_TILE)
        q_g, k_g, v_g, kv_tails = res[0:3], res[3:6], res[6:9], res[9:12]
        qs, knt, vnt = _qkv_call(
            xs, pos_s, g_mix[l], w_in16, qkv_col0, g_q[l], g_k[l], prompt=False, batch=1, tm=n_sample)

        q_rows = 8
        assert LANES % t_new == 0 and t_new <= q_rows and N_GROUPS == 3
        q_pad = jnp.pad(qs.reshape(dec_batch, t_new, D_QKV), ((0, 0), (0, q_rows - t_new), (0, 0)))
        bufs = [jnp.transpose(c[l], (0, 1, 3, 4, 2)) for c in caches]
        task = lambda g, n_h, h0=0, **kw: dict(group=g, first_head=h0, n_heads=n_h, q=q_pad, knt=knt, vnt=vnt,
                                               buf=bufs[g], **kw)
        half = H_G // 2
        o2, l2, ((so1a, sl1a, nbuf1),) = _attn_call(q_g[2], k_g[2], v_g[2], 2, batch, seq,
                                                  [task(1, half)], t_new)
        o1, l1, ((so1b, sl1b, nbuf1),) = _attn_call(q_g[1], k_g[1], v_g[1], 1, batch, seq,
                                                  [task(1, half, half, nbuf=nbuf1)], t_new)
        so1, sl1 = jnp.concatenate([so1a, so1b], axis=2), sl1a + sl1b
        o0, l0, ((so0, sl0, nbuf0),) = _attn_call(q_g[0], k_g[0], v_g[0], 0, batch, seq, [task(0, H_G)], t_new)

        xp, va_p, so2, sl2, nbuf2 = _merge_call(
            xp, (o0, o1, o2), (l0, l1, l2), p_prompt[l].reshape(batch * seq, -1), wts, ws_p, bias_p,
            prompt=True, batch=batch, tm=tm, sample_task=task(2, H_G // 2), t_new=t_new)
        s_outs = (so0, so1, so2)
        s_lses = (sl0, sl1, jnp.sum(sl2, axis=1))
        for gi, nbuf in enumerate((nbuf0, nbuf1, nbuf2)):
            kv_s[gi].append(jnp.transpose(nbuf, (0, 1, 4, 2, 3)))
        for gi, st in enumerate(kv_tails):
            kv_p[gi].append(jnp.transpose(st, (0, 1, 4, 2, 3)))
        cv_p.append(va_p)

        s_outs = [o[:, :t_new].reshape(n_sample, D_GROUP) for o in s_outs]
        s_lses = [s[:, :t_new].reshape(n_sample, LANES) for s in s_lses]
        xs, va_s = _merge_call(xs, s_outs, s_lses, p_sample[l].reshape(n_sample, -1), wts, ws_s, bias_s,
                               prompt=False, batch=1, tm=n_sample)
        cv_s.append(va_s.reshape(dec_batch, t_new, d_a))

    return (xp.reshape(batch, seq, d_model), xs.reshape(dec_batch, t_new, d_model),
            jnp.stack(kv_p[0]), jnp.stack(kv_p[1]), jnp.stack(kv_p[2]),
            jnp.stack(kv_s[0]), jnp.stack(kv_s[1]), jnp.stack(kv_s[2]),
            jnp.stack(cv_p), jnp.stack(cv_s))
```

```python
import functools
import math

import jax
import jax.numpy as jnp
import numpy as np
from jax import lax
from jax.experimental import pallas as pl
from jax.experimental.pallas import tpu as pltpu

F32 = jnp.float32
BF16 = jnp.bfloat16

LANES = 128
BF16_ROWS = 16
MXU_DIM = 256
TOKEN_TILE = 256
QKV_TILE = 512
HEAD_DIM = 64
H_G = 8
HEADS_PER_VREG = LANES // HEAD_DIM
GROUPS = ((128, 1), (512, 4), (2048, 16))
N_GROUPS = len(GROUPS)
D_GROUP = H_G * HEAD_DIM
GROUP_SLABS = D_GROUP // LANES
D_QKV = N_GROUPS * D_GROUP
N_KEYS = 129
SUB_BLOCK = 128
ROT_DIM = HEAD_DIM // 4
ROT_HALF = ROT_DIM // 2
ROPE_THETA = 500000.0
CHUNK = 128
G_A = 4
PAST_LEN = 8192
ATTN_SUB_BLOCKS = 16
EPS = 1e-6
NEG = -1e30
LN2 = math.log(2.0)
LOG2E = 1.0 / LN2
VMEM_LIMIT = 62 * 1024 * 1024


def _rms(x, g):
    return x * lax.rsqrt(jnp.mean(x * x, axis=-1, keepdims=True) + EPS) * g


def _resident(shape):
    zeros = (0,) * len(shape)
    return pl.BlockSpec(shape, lambda *_: zeros, pipeline_mode=pl.Buffered(1))


def _dot(a, b):
    return jnp.dot(a, b, preferred_element_type=F32)


def _dot_nt(a, b):
    return lax.dot_general(a, b, (((1,), (1,)), ((), ())), preferred_element_type=F32)


def _project_qkv(x_ref, gmix_ref, gq_ref, gk_ref, rc_ref, ru_ref, rd_ref, bd_ref, w_refs, perm_refs,
                 dils, emit):
    rows = x_ref.shape[0]
    tile = min(rows, TOKEN_TILE)
    starts = range(0, rows, tile)
    h = _rms(x_ref[...], gmix_ref[...]).astype(BF16)

    def permuted(p_ref):
        parts = [_dot(p_ref[...], h[s:s + tile]).astype(BF16) for s in starts]
        return parts[0] if len(parts) == 1 else jnp.concatenate(parts, axis=0)

    hs = [h if p_ref is None else permuted(p_ref) for p_ref in perm_refs]
    bd = bd_ref[...]
    wide = bd.shape[0]

    def table(ref, dil):
        if dil == 1:
            return ref[...]
        return jnp.concatenate([ref[pl.ds(s + r, tile // dil, stride=dil), :]
                                for s in starts for r in range(dil)], axis=0)

    tables = {dil: tuple(table(ref, dil) for ref in (rc_ref, ru_ref, rd_ref)) for dil in set(dils)}

    for ti, g_ref in enumerate((gq_ref, gk_ref, None)):
        for gi in range(N_GROUPS):
            z = _dot(hs[gi], w_refs[ti * N_GROUPS + gi][...])
            if g_ref is None:
                emit(ti, gi, [z[:, c * LANES:(c + 1) * LANES] for c in range(GROUP_SLABS)])
                continue
            rc, ru, rd = tables[dils[gi]]

            def rot(zn):
                return zn * rc + pltpu.roll(zn, LANES - ROT_HALF, 1) * ru + pltpu.roll(zn, ROT_HALF, 1) * rd

            vals = []
            for c in range(0, D_GROUP, wide):
                zc = z[:, c:c + wide]
                ss = _dot((zc * zc).astype(BF16), bd)
                zn = zc * lax.rsqrt(ss * (1.0 / HEAD_DIM) + EPS)
                vals += [rot(zn[:, j:j + LANES] * g_ref[...]) for j in range(0, wide, LANES)]
            emit(ti, gi, vals)


def _qkv_prompt_kernel(*refs, tm, n_tiles):
    n_perm = sum(dil > 1 for _, dil in GROUPS)
    n_w = 3 * N_GROUPS
    ins, w_refs, perms, rest = refs[:8], refs[8:8 + n_w], refs[8 + n_w:8 + n_w + n_perm], refs[8 + n_w + n_perm:]
    outs, kv_refs = rest[:3 * N_GROUPS], rest[3 * N_GROUPS:4 * N_GROUPS]
    kv_slabs, order_slab = rest[4 * N_GROUPS:4 * N_GROUPS + 2], rest[4 * N_GROUPS + 2]
    perms = list(perms)
    perm_refs = [perms.pop(0) if dil > 1 else None for _, dil in GROUPS]

    def emit(ti, gi, vals):
        out = outs[ti * N_GROUPS + gi]
        for cc, val in enumerate(vals):
            out[:, cc * LANES:(cc + 1) * LANES] = val.astype(BF16)
            if ti > 0:
                kv_slabs[ti - 1][gi * GROUP_SLABS + cc] = val

    _project_qkv(*ins, w_refs, perm_refs, [dil for _, dil in GROUPS], emit)

    i = pl.program_id(1)
    seq = n_tiles * tm
    for gi, (win, dil) in enumerate(GROUPS):
        rows = min(win, tm)
        first = (seq - win) // tm if win >= tm else n_tiles - 1
        piece = TOKEN_TILE // dil

        @pl.when(i >= first)
        def _(gi=gi, dil=dil, rows=rows, piece=piece):
            for kv in range(2):
                for cc in range(GROUP_SLABS):
                    c = gi * GROUP_SLABS + cc
                    if dil > 1:
                        for s in range(0, tm, TOKEN_TILE):
                            for r in range(dil):
                                order_slab[pl.ds(s + r, piece, stride=dil), :] = (
                                    kv_slabs[kv][c, s + r * piece:s + (r + 1) * piece, :])
                        val = order_slab[...]
                    else:
                        val = kv_slabs[kv][c]
                    t = val[tm - rows:tm].T
                    for hh in range(HEADS_PER_VREG):
                        kv_refs[gi][0, kv, cc * HEADS_PER_VREG + hh] = t[hh * HEAD_DIM:(hh + 1) * HEAD_DIM, :]


def _qkv_sample_kernel(*refs):
    n_w = 3 * N_GROUPS
    ins, w_refs, outs = refs[:8], refs[8:8 + n_w], refs[8 + n_w:11 + n_w]

    def emit(ti, gi, vals):
        for cc, val in enumerate(vals):
            c = gi * GROUP_SLABS + cc
            if ti == 0:
                outs[ti][:, c * LANES:(c + 1) * LANES] = val
            else:
                outs[ti][c * LANES:(c + 1) * LANES, :] = val.T

    _project_qkv(*ins, w_refs, [None] * N_GROUPS, [1] * N_GROUPS, emit)


def _residue_major_source(dil, tm):
    row = np.arange(tm, dtype=np.int32)
    piece = tm // dil
    return (row % piece) * dil + row // piece


def _rotary_tables(pos):
    inv = np.float32(ROPE_THETA) ** (-np.arange(ROT_HALF, dtype=np.float32) * np.float32(2.0 / ROT_DIM))
    ang = pos.astype(np.float32)[:, None] * inv[None, :]
    cos, sin = np.cos(ang), np.sin(ang)
    n = pos.shape[0]
    pad = np.zeros((n, HEAD_DIM - ROT_DIM), np.float32)
    zero = np.zeros((n, ROT_HALF), np.float32)
    rc = np.concatenate([cos, cos, pad + 1.0], axis=1)
    ru = np.concatenate([-sin, zero, pad], axis=1)
    rd = np.concatenate([zero, sin, pad], axis=1)
    return [jnp.asarray(np.tile(t, (1, HEADS_PER_VREG)).astype(np.float32)) for t in (rc, ru, rd)]


def _column_block(n_rows, width, index):
    return pl.BlockSpec((n_rows, width), lambda *_: (0, index), pipeline_mode=pl.Buffered(1))


def _qkv_call(x2, pos, g_mix, w_in, qkv_col0, g_q, g_k, *, prompt, batch, tm):
    n_tok, d_model = x2.shape
    seq = n_tok // batch
    n_tiles = seq // tm
    perms = []
    if prompt:
        assert tm % TOKEN_TILE == 0 and all(TOKEN_TILE % (dil * BF16_ROWS) == 0 for _, dil in GROUPS)
        for _, dil in GROUPS:
            if dil > 1:
                src = _residue_major_source(dil, TOKEN_TILE)
                perms.append(jnp.asarray(src[:, None] == np.arange(TOKEN_TILE)[None, :], dtype=BF16))
    rot = _rotary_tables(pos)
    gq = jnp.tile(g_q.astype(F32), HEADS_PER_VREG)[None, :] * (LOG2E * HEAD_DIM ** -0.5)
    gk = jnp.tile(g_k.astype(F32), HEADS_PER_VREG)[None, :]
    lane = np.arange(MXU_DIM)
    bd = jnp.asarray(lane[:, None] // HEAD_DIM == lane[None, :] // HEAD_DIM, dtype=BF16)

    tile_index = lambda b, i: b * n_tiles + i
    tok = lambda b, i: (tile_index(b, i), 0)
    in_specs = [
        pl.BlockSpec((tm, d_model), tok),
        _resident((1, d_model)),
        _resident((1, LANES)),
        _resident((1, LANES)),
        pl.BlockSpec((tm, LANES), lambda b, i: (i, 0)),
        pl.BlockSpec((tm, LANES), lambda b, i: (i, 0)),
        pl.BlockSpec((tm, LANES), lambda b, i: (i, 0)),
        _resident((MXU_DIM, MXU_DIM)),
    ] + [_column_block(d_model, D_GROUP, qkv_col0 // D_GROUP + j) for j in range(3 * N_GROUPS)]
    in_specs += [_resident(p.shape) for p in perms]
    assert qkv_col0 % D_GROUP == 0
    if prompt:
        out_shape = [jax.ShapeDtypeStruct((n_tok, D_GROUP), BF16)] * (3 * N_GROUPS)
        out_specs = [pl.BlockSpec((tm, D_GROUP), tok)] * (3 * N_GROUPS)
        for win, _ in GROUPS:
            rows = min(win, tm)
            first = (seq - win) // tm if win >= tm else n_tiles - 1
            out_shape.append(jax.ShapeDtypeStruct((batch, 2, H_G, HEAD_DIM, win), F32))
            out_specs.append(pl.BlockSpec(
                (1, 2, H_G, HEAD_DIM, rows),
                lambda b, i, first=first: (b, 0, 0, 0, jnp.maximum(i - first, 0))))
        scratch = [pltpu.VMEM((D_QKV // LANES, tm, LANES), F32)] * 2 + [pltpu.VMEM((tm, LANES), F32)]
        body = functools.partial(_qkv_prompt_kernel, tm=tm, n_tiles=n_tiles)
    else:
        assert n_tiles == 1 and batch == 1 and tm == LANES
        out_shape = [jax.ShapeDtypeStruct((n_tok, D_QKV), F32)] + [jax.ShapeDtypeStruct((D_QKV, n_tok), F32)] * 2
        out_specs = [pl.BlockSpec((tm, D_QKV), tok)] + [pl.BlockSpec((D_QKV, tm), lambda b, i: (0, 0))] * 2
        scratch = []
        body = _qkv_sample_kernel

    return pl.pallas_call(
        body,
        grid=(batch, n_tiles),
        in_specs=in_specs,
        out_specs=out_specs,
        out_shape=out_shape,
        scratch_shapes=scratch,
        compiler_params=pltpu.CompilerParams(
            dimension_semantics=("arbitrary", "arbitrary"), vmem_limit_bytes=VMEM_LIMIT),
        name="qkv_prompt" if prompt else "qkv_sample",
    )(x2, g_mix[None, :], gq, gk, *rot, bd, *[w_in] * (3 * N_GROUPS), *perms)


def _attn_kernel(q_ref, kc_ref, kp_ref, vc_ref, vp_ref, *rest, dil, sub_blocks, tasks, n_alias, t_new):
    n_in = 4 * len(tasks)
    n_skip = n_in + n_alias
    task_ins, (o_ref, lse_ref), task_outs = rest[:n_in], rest[n_skip:n_skip + 2], rest[n_skip + 2:]
    step = (pl.program_id(0) * pl.num_programs(1) + pl.program_id(1)) * pl.num_programs(2) + pl.program_id(2)

    has_prev = pl.program_id(1) > 0
    piece = TOKEN_TILE // dil

    span_tiles = SUB_BLOCK // piece

    def residue_rows(i, span=0):
        return [(span * span_tiles + t, 0, slice(i * piece, (i + 1) * piece)) for t in range(span_tiles)]

    def load(ref, rows, sl):
        parts = [ref[rs + (sl,)] for rs in rows]
        return parts[0] if len(parts) == 1 else jnp.concatenate(parts, axis=0)

    def store(ref, rows, sl, val):
        n = val.shape[0] // len(rows)
        for t, rs in enumerate(rows):
            ref[rs + (sl,)] = val[t * n:(t + 1) * n]
    key = lax.broadcasted_iota(jnp.int32, (2 * SUB_BLOCK, HEADS_PER_VREG * SUB_BLOCK), 0)
    qry = lax.broadcasted_iota(jnp.int32, (2 * SUB_BLOCK, HEADS_PER_VREG * SUB_BLOCK), 1) % SUB_BLOCK
    vis_cur = key <= qry
    vis_prev = key - SUB_BLOCK >= qry
    lane16 = lax.broadcasted_iota(jnp.int32, (SUB_BLOCK, LANES), 1).astype(F32).astype(BF16)
    head_lanes = (lane16 < HEAD_DIM, lane16 >= HEAD_DIM)
    eye_r = lax.broadcasted_iota(jnp.int32, (LANES, LANES), 0)
    eye_c = lax.broadcasted_iota(jnp.int32, (LANES, LANES), 1)
    eye = (eye_r == eye_c).astype(F32).astype(BF16)
    ones_rows = jnp.ones((BF16_ROWS, 2 * SUB_BLOCK), BF16)
    head_row = lax.broadcasted_iota(jnp.int32, (H_G, SUB_BLOCK), 0)

    pad = jnp.zeros((LANES - H_G, SUB_BLOCK), F32)
    slabs = [slice(p * LANES, (p + 1) * LANES) for p in range(GROUP_SLABS)]

    if dil == 1:
        blocks = [([(slice(0, SUB_BLOCK),)], kp_ref, vp_ref, [(slice(0, SUB_BLOCK),)], has_prev)]
        blocks += [([(slice(i * SUB_BLOCK, (i + 1) * SUB_BLOCK),)], kc_ref, vc_ref,
                    [(slice((i - 1) * SUB_BLOCK, i * SUB_BLOCK),)], True) for i in range(1, sub_blocks)]
    elif dil >= sub_blocks:
        blocks = [(residue_rows(i), kp_ref, vp_ref, residue_rows(i), has_prev)
                  for i in range(sub_blocks)]
    else:
        blocks = [(residue_rows(i), kp_ref, vp_ref, residue_rows(i), has_prev) for i in range(dil)]
        blocks += [(residue_rows(i, sp), kc_ref, vc_ref, residue_rows(i, sp - 1), True)
                   for sp in range(1, sub_blocks // dil) for i in range(dil)]

    def score_stage(blks):
        scores, vts = [], []
        for rows, kp_src, vp_src, prev_rows, _ in blks:
            for sl in slabs:
                q = load(q_ref, rows, sl)
                qq = jnp.concatenate([jnp.where(hl, q, jnp.zeros_like(q)) for hl in head_lanes], axis=0)
                kk = jnp.concatenate([load(kc_ref, rows, sl), load(kp_src, prev_rows, sl)], axis=0)
                vv = jnp.concatenate([load(vc_ref, rows, sl), load(vp_src, prev_rows, sl)], axis=0)
                scores.append(_dot_nt(kk, qq))
                vts.append(jnp.concatenate([_dot_nt(eye, vv).astype(BF16), ones_rows], axis=0))
        return scores, vts

    def softmax_stage(blks, scores):
        maxes, probs = [], []
        for bi, (_, _, _, _, has_prev) in enumerate(blks):
            visible = jnp.logical_or(vis_cur, jnp.logical_and(vis_prev, has_prev))
            for s in scores[bi * GROUP_SLABS:(bi + 1) * GROUP_SLABS]:
                s = jnp.where(visible, s, NEG)
                m = jnp.max(s, axis=0, keepdims=True)
                maxes.append(m)
                probs.append(jnp.exp2(s - m).astype(BF16))
        return maxes, probs

    def value_stage(vts, probs):
        return [_dot(vt, e) for vt, e in zip(vts, probs)]

    def store_stage(blks, outs, maxes):
        for bi, (rows, _, _, _, _) in enumerate(blks):
            lse_rows = jnp.zeros((H_G, SUB_BLOCK), F32)
            for p, sl in enumerate(slabs):
                ot, m = outs[bi * GROUP_SLABS + p], maxes[bi * GROUP_SLABS + p]
                den = ot[LANES:LANES + 1, :]
                lse = m * LN2 + jnp.log(den)
                inv = 1.0 / den
                o_halves = []
                for half in range(HEADS_PER_VREG):
                    qs = slice(half * SUB_BLOCK, (half + 1) * SUB_BLOCK)
                    o_halves.append(ot[half * HEAD_DIM:(half + 1) * HEAD_DIM, qs] * inv[:, qs])
                    lse_rows = jnp.where(head_row == p * HEADS_PER_VREG + half, lse[:, qs], lse_rows)
                store(o_ref, rows, sl, jnp.concatenate(o_halves, axis=0).T.astype(BF16))
            store(lse_ref, rows, slice(None), jnp.concatenate([lse_rows, pad], axis=0).T)

    chunks = [blocks[i:i + 2] for i in range(0, sub_blocks, 2)]
    staged = [score_stage(c) for c in chunks[:2]]
    for ti, (task_dil, first_head) in enumerate(tasks):
        slots = task_ins[4 * ti].shape[0]
        _sample_attention(*task_ins[4 * ti:4 * ti + 4], *task_outs[3 * ti:3 * ti + 3], step * slots,
                          dil=task_dil, first_head=first_head, t_new=t_new)
    for ci, chunk in enumerate(chunks):
        if ci + 2 < len(chunks):
            staged.append(score_stage(chunks[ci + 2]))
        scores, vts = staged[ci]
        maxes, probs = softmax_stage(chunk, scores)
        store_stage(chunk, value_stage(vts, probs), maxes)


def _attn_call(q, k, v, gi, batch, seq, sample_tasks, t_new):
    dil = GROUPS[gi][1]
    n_tok = batch * seq
    sub_blocks = ATTN_SUB_BLOCKS
    assert dil % sub_blocks == 0 or sub_blocks % dil == 0
    if dil == 1:
        rows, parts = sub_blocks * SUB_BLOCK, 1
        assert seq % rows == 0
        steps = seq // rows
        view = lambda width: (n_tok, width)
        cur_map = lambda b, s, p: (b * steps + s, 0)
        cur = lambda width: pl.BlockSpec((rows, width), cur_map)
        prev = pl.BlockSpec((SUB_BLOCK, D_GROUP),
                            lambda b, s, p: (jnp.maximum((b * steps + s) * sub_blocks - 1, 0), 0))
    else:
        span, parts, spans = SUB_BLOCK * dil, max(dil // sub_blocks, 1), max(sub_blocks // dil, 1)
        assert span % TOKEN_TILE == 0 and seq % (span * spans) == 0
        assert TOKEN_TILE % (dil * BF16_ROWS) == 0
        steps, span_tiles, part_rows = seq // (span * spans), span // TOKEN_TILE, TOKEN_TILE // parts
        view = lambda width: (n_tok // TOKEN_TILE, parts, part_rows, width)
        cur_map = lambda b, s, p: (b * steps + s, p, 0, 0)
        cur = lambda width: pl.BlockSpec((spans * span_tiles, 1, part_rows, width), cur_map)
        prev = pl.BlockSpec((span_tiles, 1, part_rows, D_GROUP),
                            lambda b, s, p: (jnp.maximum((b * steps + s) * spans - 1, 0), p, 0, 0))
    q, k, v = (t.reshape(view(D_GROUP)) for t in (q, k, v))
    in_specs = [cur(D_GROUP), cur(D_GROUP), prev, cur(D_GROUP), prev]
    out_specs = [cur(D_GROUP), cur(LANES)]
    out_shape = [jax.ShapeDtypeStruct(view(D_GROUP), BF16), jax.ShapeDtypeStruct(view(LANES), F32)]
    operands, alias_operands, aliases = [q, k, k, v, v], [], {}
    seq_of = lambda b, s, p: (b * steps + s) * parts + p
    for task in sample_tasks:
        n_seq, q_rows, _ = task["q"].shape
        assert n_seq % (batch * steps * parts) == 0
        slots = n_seq // (batch * steps * parts)
        wb = task["buf"].shape[4]
        n_h, h0 = task["n_heads"], task["first_head"]
        cols = n_h * HEAD_DIM
        col_block = (task["group"] * D_GROUP + h0 * HEAD_DIM) // cols
        buf_spec = pl.BlockSpec((slots, 2, n_h, HEAD_DIM, wb),
                                lambda b, s, p, hb=h0 // n_h: (seq_of(b, s, p), 0, hb, 0, 0))
        new_spec = pl.BlockSpec((cols, LANES), lambda b, s, p, cb=col_block: (cb, 0))
        in_specs += [pl.BlockSpec((slots, q_rows, cols), lambda b, s, p, cb=col_block: (seq_of(b, s, p), 0, cb)),
                     new_spec, new_spec, buf_spec]
        operands += [task["q"], task["knt"], task["vnt"], task["buf"]]
        if task.get("nbuf") is not None:
            aliases[len(alias_operands)] = len(out_shape) + 2
            alias_operands.append(task["nbuf"])
        out_specs += [pl.BlockSpec((slots, q_rows, cols), lambda b, s, p: (seq_of(b, s, p), 0, 0)),
                      pl.BlockSpec((slots, q_rows, LANES), lambda b, s, p: (seq_of(b, s, p), 0, 0)), buf_spec]
        out_shape += [jax.ShapeDtypeStruct((n_seq, q_rows, cols), F32),
                      jax.ShapeDtypeStruct((n_seq, q_rows, LANES), F32),
                      jax.ShapeDtypeStruct(task["buf"].shape, F32)]
    n_main = len(operands)
    res = pl.pallas_call(
        functools.partial(_attn_kernel, dil=dil, sub_blocks=sub_blocks, n_alias=len(alias_operands), t_new=t_new,
                          tasks=tuple((GROUPS[t["group"]][1], t["first_head"]) for t in sample_tasks)),
        grid=(batch, steps, parts),
        in_specs=in_specs + [pl.BlockSpec(memory_space=pl.ANY)] * len(alias_operands),
        out_specs=out_specs,
        out_shape=out_shape,
        input_output_aliases={n_main + i: o for i, o in aliases.items()},
        compiler_params=pltpu.CompilerParams(
            dimension_semantics=("arbitrary", "arbitrary", "arbitrary"), vmem_limit_bytes=VMEM_LIMIT),
        name=f"attn_g{gi}",
    )(*operands, *alias_operands)
    task_res = [tuple(res[2 + 3 * i:5 + 3 * i]) for i in range(len(sample_tasks))]
    return res[0].reshape(n_tok, D_GROUP), res[1].reshape(n_tok, LANES), task_res


def _sample_attention(q_ref, knt_ref, vnt_ref, buf_ref, o_ref, lse_ref, nbuf_ref, first_seq, **kw):
    for slot in range(q_ref.shape[0]):
        _sample_attention_one(q_ref, knt_ref, vnt_ref, buf_ref, o_ref, lse_ref, nbuf_ref, slot,
                              first_seq + slot, **kw)


def _sample_attention_one(q_ref, knt_ref, vnt_ref, buf_ref, o_ref, lse_ref, nbuf_ref, slot, seq_index, *,
                          dil, first_head, t_new):
    assert dil & (dil - 1) == 0
    q_rows = q_ref.shape[1]
    n_heads, wb = buf_ref.shape[2], buf_ref.shape[4]
    width = wb + LANES
    shift = lax.rem(LANES - t_new * seq_index, LANES)
    knt = pltpu.roll(knt_ref[...], shift, 1)
    vnt = pltpu.roll(vnt_ref[...], shift, 1)
    q = q_ref[slot]

    t_row = lax.broadcasted_iota(jnp.int32, (q_rows, width), 0)
    p_col = lax.broadcasted_iota(jnp.int32, (q_rows, width), 1)
    back = wb + t_row - p_col
    valid = jnp.logical_and(jnp.logical_and(back >= 0, (back & (dil - 1)) == 0),
                            jnp.logical_and(back <= (N_KEYS - 1) * dil, p_col < wb + t_new))
    valid = jnp.concatenate([valid] * n_heads, axis=0)
    lane = lax.broadcasted_iota(jnp.int32, (q_rows, LANES), 1)
    cols = n_heads * HEAD_DIM

    kx = jnp.concatenate([buf_ref[slot, 0].reshape(cols, wb), knt], axis=1)
    vx = jnp.concatenate([buf_ref[slot, 1].reshape(cols, wb), vnt], axis=1)
    nbuf_ref[slot, 0] = pltpu.roll(kx, width - t_new, 1)[:, :wb].reshape(n_heads, HEAD_DIM, wb)
    nbuf_ref[slot, 1] = pltpu.roll(vx, width - t_new, 1)[:, :wb].reshape(n_heads, HEAD_DIM, wb)

    q_all = jnp.concatenate([q] * n_heads, axis=0)
    r_head = lax.broadcasted_iota(jnp.int32, q_all.shape, 0) // q_rows
    c_head = lax.broadcasted_iota(jnp.int32, q_all.shape, 1) // HEAD_DIM
    q_all = jnp.where(r_head == c_head, q_all, 0.0).astype(BF16)
    s = jnp.where(valid, _dot(q_all, kx.astype(BF16)), NEG)
    m = jnp.max(s, axis=1, keepdims=True)
    e = jnp.exp2(s - m)
    den = jnp.sum(e, axis=1, keepdims=True)
    o_all = _dot_nt(e.astype(BF16), vx.astype(BF16)) / den
    lse = m * LN2 + jnp.log(den)
    o_ref[slot] = jnp.concatenate(
        [o_all[h * q_rows:(h + 1) * q_rows, h * HEAD_DIM:(h + 1) * HEAD_DIM] for h in range(n_heads)], axis=1)
    lse_ref[slot] = functools.reduce(jnp.add, [
        jnp.where(lane == first_head + h, lse[h * q_rows:(h + 1) * q_rows], 0.0) for h in range(n_heads)])


def _merge_kernel(x_ref, o0_ref, o1_ref, o2_ref, l0_ref, l1_ref, l2_ref, pe_ref,
                  gmix_ref, wu_ref, wva_ref, wga0_ref, wga1_ref, wgb0_ref, wgb1_ref,
                  gv_ref, ws_ref, bias_ref, woa_ref, wob_ref, wo_ref,
                  gffn_ref, wff1_ref, wff2_ref, gple_ref, wpg_ref, wpp_ref,
                  *rest, prompt, tm, n_tiles, sample):
    if sample is None:
        (y_ref, va_ref), scratch = rest[:2], rest[2:]
    else:
        task_ins, (y_ref, va_ref), task_outs, scratch = rest[:4], rest[4:6], rest[6:9], rest[9:]
    d_model = x_ref.shape[1]
    d_a = gv_ref.shape[1]
    c_g = d_a // G_A
    x = x_ref[...]
    h = _rms(x, gmix_ref[...]).astype(BF16)

    u = jax.nn.gelu(_dot(h, wu_ref[...]))
    va = _rms(jax.nn.gelu(_dot(h, wva_ref[...])), gv_ref[...])
    if prompt:
        @pl.when(pl.program_id(1) == n_tiles - 1)
        def _():
            va_ref[0] = va[tm - CHUNK:tm, :]
    else:
        va_ref[...] = va
    va16 = va.astype(BF16)
    r_i = lax.broadcasted_iota(jnp.int32, (CHUNK, CHUNK), 0)
    c_i = lax.broadcasted_iota(jnp.int32, (CHUNK, CHUNK), 1)
    w_tril = [jnp.where(r_i >= c_i, ws_ref[g], jnp.zeros((CHUNK, CHUNK), BF16)) for g in range(G_A)]
    mix_rows = []
    for ci in range(tm // CHUNK):
        rs = slice(ci * CHUNK, (ci + 1) * CHUNK)
        cols = [_dot(w_tril[g], va16[rs, g * c_g:(g + 1) * c_g]) for g in range(G_A)]
        mix_rows.append(jnp.concatenate(cols, axis=1) + bias_ref[...])
    a = u * jnp.concatenate(mix_rows, axis=0)

    o_refs, l_refs = (o0_ref, o1_ref, o2_ref), (l0_ref, l1_ref, l2_ref)
    o_slabs, lses = [], []
    for gi, (_, dil) in enumerate(GROUPS):
        if not prompt or dil == 1:
            o_slabs.append([o_refs[gi][:, cc * LANES:(cc + 1) * LANES].astype(F32)
                            for cc in range(GROUP_SLABS)])
            lses.append(l_refs[gi][...])
            continue
        o_sc, l_sc = scratch[2 * (gi - 1)], scratch[2 * (gi - 1) + 1]
        piece = tm // dil
        for r in range(dil):
            rows = slice(r * piece, (r + 1) * piece)
            for cc in range(GROUP_SLABS):
                o_sc[cc, pl.ds(r, piece, stride=dil), :] = (
                    o_refs[gi][rows, cc * LANES:(cc + 1) * LANES].astype(F32))
            l_sc[pl.ds(r, piece, stride=dil), :] = l_refs[gi][rows, :]
        o_slabs.append([o_sc[cc] for cc in range(GROUP_SLABS)])
        lses.append(l_sc[...])

    m = jnp.maximum(jnp.maximum(lses[0], lses[1]), lses[2])
    ws = [jnp.exp(l - m) for l in lses]
    inv = 1.0 / (ws[0] + ws[1] + ws[2])
    ws = [w * inv for w in ws]
    lane = lax.broadcasted_iota(jnp.int32, (tm, LANES), 1)
    first_head = lane < HEAD_DIM
    b_cols = []
    for p in range(GROUP_SLABS):
        acc = jnp.zeros((tm, LANES), F32)
        for w, slabs in zip(ws, o_slabs):
            w_pair = jnp.where(first_head, w[:, 2 * p:2 * p + 1], w[:, 2 * p + 1:2 * p + 2])
            acc = acc + w_pair * slabs[p]
        b_cols.append(acc)
    b = jnp.concatenate(b_cols, axis=1)

    gate_a = jax.nn.sigmoid(jnp.concatenate([_dot(h, wga0_ref[...]), _dot(h, wga1_ref[...])], axis=1))
    gate_b = jax.nn.sigmoid(jnp.concatenate([_dot(h, wgb0_ref[...]), _dot(h, wgb1_ref[...])], axis=1))
    merged = gate_a * _dot(a.astype(BF16), woa_ref[...]) + gate_b * _dot(b.astype(BF16), wob_ref[...])
    x = x + _dot(merged.astype(BF16), wo_ref[...])

    h2 = _rms(x, gffn_ref[...]).astype(BF16)
    d_ff = wff1_ref.shape[1]
    ff_step = 1024
    ffn = jnp.zeros((tm, d_model), F32)
    for c0 in range(0, d_ff, ff_step):
        hid = jnp.square(jnp.maximum(_dot(h2, wff1_ref[:, c0:c0 + ff_step]), 0.0))
        ffn = ffn + _dot(hid.astype(BF16), wff2_ref[c0:c0 + ff_step, :])
    x = x + ffn

    if sample is not None:
        parts = H_G // task_ins[3].shape[2]
        step = pl.program_id(0) * n_tiles + pl.program_id(1)
        _sample_attention(*task_ins, *task_outs, step // parts, dil=sample[0],
                          first_head=(step % parts) * (H_G // parts), t_new=sample[1])

    gate = jax.nn.sigmoid(_dot(_rms(x, gple_ref[...]).astype(BF16), wpg_ref[...]))
    y_ref[...] = x + gate * _dot(pe_ref[...].astype(BF16), wpp_ref[...])


def _merge_call(x2, outs, lses, pe2, wts, ws_mat, bias, *, prompt, batch, tm, sample_task=None, t_new=None):
    n_tok, d_model = x2.shape
    seq = n_tok // batch
    n_tiles = seq // tm
    d_a = wts["g_v"].shape[1]
    tile_index = lambda b, i: b * n_tiles + i
    tok = lambda b, i: (tile_index(b, i), 0)
    row_spec = lambda width: pl.BlockSpec((tm, width), tok)

    names = ("g_mix", "w_in", "w_in", "w_in", "w_in", "w_in", "w_in", "g_v", "ws", "bias", "w_oa", "w_ob", "w_o",
             "g_ffn", "w_ff1", "w_ff2", "g_ple", "w_ple_gate", "w_ple_proj")
    consts = dict(wts, ws=ws_mat, bias=bias)
    half_model = d_model // 2
    assert d_a == half_model and wts["gate_col0"] % half_model == 0
    g0 = wts["gate_col0"] // half_model
    w_in_blocks = iter([0, 1, g0, g0 + 1, g0 + 2, g0 + 3])
    const_specs = [_column_block(d_model, half_model, next(w_in_blocks)) if k == "w_in"
                   else _resident(consts[k].shape) for k in names]
    in_specs = ([row_spec(d_model)]
                + [row_spec(D_GROUP)] * N_GROUPS + [row_spec(LANES)] * N_GROUPS
                + [row_spec(pe2.shape[1])] + const_specs)
    scratch = []
    if prompt:
        va_shape = jax.ShapeDtypeStruct((batch, CHUNK, d_a), F32)
        va_spec = pl.BlockSpec((1, CHUNK, d_a), lambda b, i: (b, 0, 0))
        for _, dil in GROUPS:
            if dil > 1:
                scratch += [pltpu.VMEM((GROUP_SLABS, tm, LANES), F32), pltpu.VMEM((tm, LANES), F32)]
    else:
        va_shape = jax.ShapeDtypeStruct((n_tok, d_a), F32)
        va_spec = row_spec(d_a)
    operands = [x2, *outs, *lses, pe2, *[consts[k] for k in names]]
    out_specs = [row_spec(d_model), va_spec]
    out_shape = [jax.ShapeDtypeStruct((n_tok, d_model), F32), va_shape]
    sample = None
    if sample_task is not None:
        task = sample_task
        n_seq, q_rows, _ = task["q"].shape
        n_h, wb = task["n_heads"], task["buf"].shape[4]
        parts = H_G // n_h
        assert n_seq * parts == batch * n_tiles
        cols = n_h * HEAD_DIM
        col0 = task["group"] * D_GROUP // cols
        seq_of = lambda b, i: tile_index(b, i) // parts
        part_of = lambda b, i: tile_index(b, i) % parts
        buf_spec = pl.BlockSpec((1, 2, n_h, HEAD_DIM, wb), lambda b, i: (seq_of(b, i), 0, part_of(b, i), 0, 0))
        new_spec = pl.BlockSpec((cols, LANES), lambda b, i: (col0 + part_of(b, i), 0))
        in_specs += [pl.BlockSpec((1, q_rows, cols), lambda b, i: (seq_of(b, i), 0, col0 + part_of(b, i))),
                     new_spec, new_spec, buf_spec]
        operands += [task["q"], task["knt"], task["vnt"], task["buf"]]
        out_specs += [pl.BlockSpec((1, q_rows, cols), lambda b, i: (seq_of(b, i), 0, part_of(b, i))),
                      pl.BlockSpec((1, None, q_rows, LANES), lambda b, i: (seq_of(b, i), part_of(b, i), 0, 0)),
                      buf_spec]
        out_shape += [jax.ShapeDtypeStruct((n_seq, q_rows, D_GROUP), F32),
                      jax.ShapeDtypeStruct((n_seq, parts, q_rows, LANES), F32),
                      jax.ShapeDtypeStruct(task["buf"].shape, F32)]
        sample = (GROUPS[task["group"]][1], t_new)
    return pl.pallas_call(
        functools.partial(_merge_kernel, prompt=prompt, tm=tm, n_tiles=n_tiles, sample=sample),
        grid=(batch, n_tiles),
        in_specs=in_specs,
        out_specs=out_specs,
        out_shape=out_shape,
        scratch_shapes=scratch,
        compiler_params=pltpu.CompilerParams(
            dimension_semantics=("arbitrary", "arbitrary"), vmem_limit_bytes=VMEM_LIMIT),
        name="merge_prompt" if prompt else "merge_sample",
    )(*operands)


def kernel(x_prompt, x_sample, p_prompt, p_sample, cache_kv_w128, cache_kv_w512, cache_kv_w2048,
           g_mix, w_in, g_v, w_s, b_s, g_q, g_k, w_oa, w_ob, w_o,
           g_ffn, w_ff1, w_ff2, g_ple, w_ple_gate, w_ple_proj):
    depth = w_in.shape[0]
    batch, seq, d_model = x_prompt.shape
    dec_batch, t_new, _ = x_sample.shape
    d_a = g_v.shape[1]
    n_sample = dec_batch * t_new
    assert n_sample == CHUNK and w_in.shape[2] == 2 * d_a + 3 * D_QKV + 2 * d_model
    caches = (cache_kv_w128, cache_kv_w512, cache_kv_w2048)

    xp = x_prompt.reshape(batch * seq, d_model)
    xs = x_sample.reshape(n_sample, d_model)
    pos_p = np.arange(seq, dtype=np.int32)
    pos_s = PAST_LEN + np.tile(np.arange(t_new, dtype=np.int32), dec_batch)

    kv_p = [[] for _ in GROUPS]
    kv_s = [[] for _ in GROUPS]
    cv_p, cv_s = [], []
    for l in range(depth):
        w_in16 = w_in[l].astype(BF16)
        qkv_col0 = 2 * d_a
        wts = {
            "g_mix": g_mix[l][None, :], "w_in": w_in16, "gate_col0": qkv_col0 + 3 * D_QKV,
            "g_v": g_v[l][None, :],
            "w_oa": w_oa[l].astype(BF16), "w_ob": w_ob[l].astype(BF16), "w_o": w_o[l].astype(BF16),
            "g_ffn": g_ffn[l][None, :], "w_ff1": w_ff1[l].astype(BF16), "w_ff2": w_ff2[l].astype(BF16),
            "g_ple": g_ple[l][None, :], "w_ple_gate": w_ple_gate[l].astype(BF16),
            "w_ple_proj": w_ple_proj[l].astype(BF16),
        }
        c_g = d_a // G_A
        ws_p = w_s[l].astype(BF16)
        bias_p = jnp.repeat(b_s[l].T, c_g, axis=1)
        tok = np.arange(CHUNK)
        same_seq = jnp.asarray(tok[:, None] // t_new == tok[None, :] // t_new)
        row_of = jnp.asarray(tok[:, None] % t_new == np.arange(t_new)[None, :], dtype=F32)
        ws_s = jnp.einsum("rt,gts,cs->grc", row_of, w_s[l][:, :t_new, :t_new], row_of,
                          precision=lax.Precision.HIGHEST)
        ws_s = jnp.where(same_seq, ws_s, 0.0).astype(BF16)
        reps = CHUNK // t_new
        bias_s = jnp.tile(jnp.repeat(b_s[l][:, :t_new].T, c_g, axis=1), (reps, 1))

        tm = TOKEN_TILE
        res = _qkv_call(xp, pos_p, g_mix[l], w_in16, qkv_col0, g_q[l], g_k[l], prompt=True, batch=batch,
                        tm=QKV_TILE)
        q_g, k_g, v_g, kv_tails = res[0:3], res[3:6], res[6:9], res[9:12]
        qs, knt, vnt = _qkv_call(
            xs, pos_s, g_mix[l], w_in16, qkv_col0, g_q[l], g_k[l], prompt=False, batch=1, tm=n_sample)

        q_rows = 8
        assert LANES % t_new == 0 and t_new <= q_rows and N_GROUPS == 3
        q_pad = jnp.pad(qs.reshape(dec_batch, t_new, D_QKV), ((0, 0), (0, q_rows - t_new), (0, 0)))
        bufs = [jnp.transpose(c[l], (0, 1, 3, 4, 2)) for c in caches]
        task = lambda g, n_h, h0=0, **kw: dict(group=g, first_head=h0, n_heads=n_h, q=q_pad, knt=knt, vnt=vnt,
                                               buf=bufs[g], **kw)
        half = H_G // 2
        o2, l2, ((so1a, sl1a, nbuf1),) = _attn_call(q_g[2], k_g[2], v_g[2], 2, batch, seq,
                                                  [task(1, half)], t_new)
        o1, l1, ((so1b, sl1b, nbuf1),) = _attn_call(q_g[1], k_g[1], v_g[1], 1, batch, seq,
                                                  [task(1, half, half, nbuf=nbuf1)], t_new)
        so1, sl1 = jnp.concatenate([so1a, so1b], axis=2), sl1a + sl1b
        o0, l0, ((so0, sl0, nbuf0),) = _attn_call(q_g[0], k_g[0], v_g[0], 0, batch, seq, [task(0, H_G)], t_new)

        xp, va_p, so2, sl2, nbuf2 = _merge_call(
            xp, (o0, o1, o2), (l0, l1, l2), p_prompt[l].reshape(batch * seq, -1), wts, ws_p, bias_p,
            prompt=True, batch=batch, tm=tm, sample_task=task(2, H_G // 2), t_new=t_new)
        s_outs = (so0, so1, so2)
        s_lses = (sl0, sl1, jnp.sum(sl2, axis=1))
        for gi, nbuf in enumerate((nbuf0, nbuf1, nbuf2)):
            kv_s[gi].append(jnp.transpose(nbuf, (0, 1, 4, 2, 3)))
        for gi, st in enumerate(kv_tails):
            kv_p[gi].append(jnp.transpose(st, (0, 1, 4, 2, 3)))
        cv_p.append(va_p)

        s_outs = [o[:, :t_new].reshape(n_sample, D_GROUP) for o in s_outs]
        s_lses = [s[:, :t_new].reshape(n_sample, LANES) for s in s_lses]
        xs, va_s = _merge_call(xs, s_outs, s_lses, p_sample[l].reshape(n_sample, -1), wts, ws_s, bias_s,
                               prompt=False, batch=1, tm=n_sample)
        cv_s.append(va_s.reshape(dec_batch, t_new, d_a))

    return (xp.reshape(batch, seq, d_model), xs.reshape(dec_batch, t_new, d_model),
            jnp.stack(kv_p[0]), jnp.stack(kv_p[1]), jnp.stack(kv_p[2]),
            jnp.stack(kv_s[0]), jnp.stack(kv_s[1]), jnp.stack(kv_s[2]),
            jnp.stack(cv_p), jnp.stack(cv_s))
```
